```python
import jax, jax.numpy as jnp
from jax import lax
import numpy as np

D_MODEL = 1024
BATCH = 16
SEQ = 4096
DEPTH = 4

NUM_MIXERS = 3
HEAD_DIM = 128
D_INNER = 2 * D_MODEL
MEM_HEADS = 4
MEM_LEN = 256
MIX_HEADS = D_INNER // HEAD_DIM - MEM_HEADS
A_KV_HEADS = MIX_HEADS // 3
A_GROUP = MIX_HEADS // A_KV_HEADS
A_WINDOW = 128
B_GROUPS = ((128, 1), (512, 4), (2048, 16))
NA_ROWS = 8
NA_COLS = 16
GRID_W = 64
ROPE_THETA = 500000.0
ROPE_DIMS = HEAD_DIM // 4
LN_EPS = 1e-5
ALPHA = (2 * DEPTH) ** 0.25
BETA = (8 * DEPTH) ** -0.25
NEG_INF = -1e30
N_A = len(range(0, DEPTH, NUM_MIXERS))
N_B = len(range(1, DEPTH, NUM_MIXERS))
N_C = len(range(2, DEPTH, NUM_MIXERS))
SPLIT_A = (MIX_HEADS * HEAD_DIM, A_KV_HEADS * HEAD_DIM, A_KV_HEADS * HEAD_DIM, MEM_HEADS * HEAD_DIM, D_INNER)
SPLIT_B = (len(B_GROUPS) * MIX_HEADS * HEAD_DIM, MIX_HEADS * HEAD_DIM, MIX_HEADS * HEAD_DIM, MEM_HEADS * HEAD_DIM, D_INNER)
SPLIT_C = (MIX_HEADS * HEAD_DIM, MIX_HEADS * HEAD_DIM, MIX_HEADS * HEAD_DIM, MEM_HEADS * HEAD_DIM, D_INNER)

kernel_name = "hybrid_window_dilated_neighborhood_encoder"


def layer_norm(x, g, b):
    xf = x.astype(jnp.float32)
    mu = jnp.mean(xf, axis=-1, keepdims=True)
    var = jnp.mean(jnp.square(xf - mu), axis=-1, keepdims=True)
    y = (xf - mu) * lax.rsqrt(var + LN_EPS) * g.astype(jnp.float32) + b.astype(jnp.float32)
    return y.astype(x.dtype)


def split_proj(h, sizes):
    idx = np.cumsum(sizes)[:-1].tolist()
    return jnp.split(h, idx, axis=-1)


def to_heads(t, n):
    b, l, _ = t.shape
    return t.reshape(b, l, n, HEAD_DIM).transpose(0, 2, 1, 3)


def apply_rope(x, pos):
    half = ROPE_DIMS // 2
    inv = ROPE_THETA ** (-jnp.arange(half, dtype=jnp.float32) / half)
    ang = pos.astype(jnp.float32)[:, None] * inv[None, :]
    cos, sin = jnp.cos(ang).astype(x.dtype), jnp.sin(ang).astype(x.dtype)
    x1, x2, rest = x[..., :half], x[..., half:ROPE_DIMS], x[..., ROPE_DIMS:]
    return jnp.concatenate([x1 * cos - x2 * sin, x2 * cos + x1 * sin, rest], axis=-1)


def band_attention(q, k, v, w, sink=None, return_lse=False):
    L, dh = q.shape[-2], q.shape[-1]
    nb = -(-L // w)
    pad = nb * w - L
    lead_q, lead_k = q.shape[:-2], k.shape[:-2]
    qb = jnp.pad(q, [(0, 0)] * len(lead_q) + [(0, pad), (0, 0)]).reshape(lead_q + (nb, w, dh))

    def blocks3(t):
        tp = jnp.pad(t, [(0, 0)] * len(lead_k) + [(w, w + pad), (0, 0)]).reshape(lead_k + (nb + 2, w, dh))
        return jnp.concatenate([tp[..., :-2, :, :], tp[..., 1:-1, :, :], tp[..., 2:, :, :]], axis=-2)

    kb, vb = blocks3(k), blocks3(v)
    s = jnp.einsum('...gnid,...njd->...gnij', qb, kb).astype(jnp.float32) * (dh ** -0.5)
    blk = jnp.arange(nb)[:, None, None]
    t_pos = blk * w + jnp.arange(w)[None, :, None]
    s_pos = (blk - 1) * w + jnp.arange(3 * w)[None, None, :]
    valid = (jnp.abs(s_pos - t_pos) <= w) & (s_pos >= 0) & (s_pos < L)
    s = jnp.where(valid, s, NEG_INF)
    m = jnp.max(s, axis=-1, keepdims=True)
    if sink is not None:
        m = jnp.maximum(m, sink)
    p = jnp.exp(s - m)
    z = jnp.sum(p, axis=-1, keepdims=True)
    if sink is not None:
        z = z + jnp.exp(sink - m)
    o = jnp.einsum('...gnij,...njd->...gnid', (p / z).astype(v.dtype), vb)
    o = o.reshape(lead_q + (nb * w, dh))[..., :L, :]
    if return_lse:
        lse = (m + jnp.log(z))[..., 0].reshape(lead_q + (nb * w,))[..., :L]
        return o, lse
    return o


def dilate(t, d):
    L, D = t.shape[-2], t.shape[-1]
    return t.reshape(t.shape[:-2] + (L // d, d, D)).swapaxes(-3, -2)


def undilate(t, d):
    n, D = t.shape[-2], t.shape[-1]
    return t.swapaxes(-3, -2).reshape(t.shape[:-3] + (n * d, D))


def undilate_rows(t, d):
    n = t.shape[-1]
    return t.swapaxes(-2, -1).reshape(t.shape[:-2] + (n * d,))


def mixer_a(q, k, v, sink, pos):
    b, _, l, _ = q.shape
    q, k = apply_rope(q, pos), apply_rope(k, pos)
    qg = q.reshape(b, A_KV_HEADS, A_GROUP, l, HEAD_DIM)
    s = sink.astype(jnp.float32).reshape(A_KV_HEADS, A_GROUP, 1, 1, 1)
    o = band_attention(qg, k, v, A_WINDOW, sink=s)
    return o.reshape(b, MIX_HEADS, l, HEAD_DIM)


def mixer_b(q, k, v, pos):
    q, k = apply_rope(q, pos), apply_rope(k, pos)
    outs, lses = [], []
    for g, (window, dil) in enumerate(B_GROUPS):
        qg = q[:, g * MIX_HEADS:(g + 1) * MIX_HEADS]
        o, lse = band_attention(dilate(qg, dil)[..., None, :, :], dilate(k, dil), dilate(v, dil),
                                window // (2 * dil), return_lse=True)
        outs.append(undilate(o[..., 0, :, :], dil))
        lses.append(undilate_rows(lse[..., 0, :], dil))
    wts = jax.nn.softmax(jnp.stack(lses), axis=0)
    o = jnp.einsum('gbhl,gbhld->bhld', wts, jnp.stack(outs).astype(jnp.float32))
    return o.astype(q.dtype)


def mixer_c(q, k, v, rpb):
    b, h, L, dh = q.shape
    rows = L // GRID_W
    kh, kw = min(NA_ROWS, rows), NA_COLS
    kr, kc = 2 * kh - 1, 2 * kw - 1
    nrb, ncb = -(-rows // kh), GRID_W // kw
    qg = jnp.pad(q.reshape(b, h, rows, GRID_W, dh), ((0, 0), (0, 0), (0, nrb * kh - rows), (0, 0), (0, 0)))
    qb = qg.reshape(b, h, nrb, kh, ncb, kw, dh).transpose(0, 1, 2, 4, 3, 5, 6)
    qrow = jnp.arange(nrb * kh)
    qcol = jnp.arange(GRID_W)
    rs = jnp.clip(qrow - kh // 2, 0, rows - kh)
    cs = jnp.clip(qcol - kw // 2, 0, GRID_W - kw)
    krow = rs[::kh][:, None] + jnp.arange(kr)[None, :]
    kcol = cs[::kw][:, None] + jnp.arange(kc)[None, :]
    ridx = jnp.minimum(krow, rows - 1)
    cidx = jnp.minimum(kcol, GRID_W - 1)
    kg = k.reshape(b, h, rows, GRID_W, dh)
    vg = v.reshape(b, h, rows, GRID_W, dh)
    gi_r, gi_c = ridx[:, None, :, None], cidx[None, :, None, :]
    kb = kg[:, :, gi_r, gi_c].reshape(b, h, nrb, ncb, kr * kc, dh)
    vb = vg[:, :, gi_r, gi_c].reshape(b, h, nrb, ncb, kr * kc, dh)
    qr = qrow.reshape(nrb, kh)
    qc = qcol.reshape(ncb, kw)
    rsq, csq = rs.reshape(nrb, kh), cs.reshape(ncb, kw)
    rv = (krow[:, None, :] >= rsq[:, :, None]) & (krow[:, None, :] < rsq[:, :, None] + kh)
    cv = (kcol[:, None, :] >= csq[:, :, None]) & (kcol[:, None, :] < csq[:, :, None] + kw)
    mask = rv[:, None, :, None, :, None] & cv[None, :, None, :, None, :]
    dri = jnp.clip(krow[:, None, :] - qr[:, :, None] + NA_ROWS - 1, 0, 2 * NA_ROWS - 2)
    dci = jnp.clip(kcol[:, None, :] - qc[:, :, None] + NA_COLS - 1, 0, 2 * NA_COLS - 2)
    bias = rpb.astype(jnp.float32)[:, dri[:, None, :, None, :, None], dci[None, :, None, :, None, :]]
    s = jnp.einsum('bhnmijd,bhnmkd->bhnmijk', qb, kb).astype(jnp.float32) * (dh ** -0.5)
    s = s + bias.reshape((h, nrb, ncb, kh, kw, kr * kc))[None]
    s = jnp.where(mask.reshape(nrb, ncb, kh, kw, kr * kc), s, NEG_INF)
    p = jax.nn.softmax(s, axis=-1).astype(v.dtype)
    o = jnp.einsum('bhnmijk,bhnmkd->bhnmijd', p, vb)
    o = o.transpose(0, 1, 2, 4, 3, 5, 6).reshape(b, h, nrb * kh, GRID_W, dh)[:, :, :rows]
    return o.reshape(b, h, L, dh)


def memory_attention(mq, mem, w_mkv):
    mk, mv = jnp.split(jnp.einsum('bmd,de->bme', mem, w_mkv), 2, axis=-1)
    mk, mv = to_heads(mk, MEM_HEADS), to_heads(mv, MEM_HEADS)
    s = jnp.einsum('bhld,bhmd->bhlm', mq, mk).astype(jnp.float32) * (HEAD_DIM ** -0.5)
    p = jax.nn.softmax(s, axis=-1).astype(mv.dtype)
    return jnp.einsum('bhlm,bhmd->bhld', p, mv)


def setup_inputs(seed: int = 0) -> dict:
    key = jax.random.key(seed)
    ks = jax.random.split(key, 12)
    nrm = jax.random.normal
    f32 = jnp.float32
    x = nrm(ks[0], (BATCH, SEQ, D_MODEL), f32)
    mem = nrm(ks[1], (BATCH, MEM_LEN, D_MODEL), f32)
    w_in_a = nrm(ks[2], (N_A, D_MODEL, sum(SPLIT_A)), f32) * D_MODEL ** -0.5
    sink_a = nrm(ks[3], (N_A, MIX_HEADS), f32)
    w_in_b = nrm(ks[4], (N_B, D_MODEL, sum(SPLIT_B)), f32) * D_MODEL ** -0.5
    w_in_c = nrm(ks[5], (N_C, D_MODEL, sum(SPLIT_C)), f32) * D_MODEL ** -0.5
    rpb_c = nrm(ks[6], (N_C, MIX_HEADS, 2 * NA_ROWS - 1, 2 * NA_COLS - 1), f32) * 0.1
    w_mkv = nrm(ks[7], (DEPTH, D_MODEL, 2 * MEM_HEADS * HEAD_DIM), f32) * D_MODEL ** -0.5
    w_out = nrm(ks[8], (DEPTH, D_INNER, D_MODEL), f32) * (D_INNER ** -0.5 * BETA)
    ln_g = 1.0 + 0.05 * nrm(ks[9], (DEPTH, D_MODEL), f32)
    ln_b = 0.02 * nrm(ks[10], (DEPTH, D_MODEL), f32)
    return {"x": x, "mem": mem, "w_in_a": w_in_a, "sink_a": sink_a, "w_in_b": w_in_b,
            "w_in_c": w_in_c, "rpb_c": rpb_c, "w_mkv": w_mkv, "w_out": w_out,
            "ln_g": ln_g, "ln_b": ln_b}


def reference(x, mem, w_in_a, sink_a, w_in_b, w_in_c, rpb_c, w_mkv, w_out, ln_g, ln_b):
    b, L, _ = x.shape
    pos = jnp.arange(L)
    for i in range(DEPTH):
        kind, j = i % NUM_MIXERS, i // NUM_MIXERS
        if kind == 0:
            q, k, v, mq, gate = split_proj(jnp.einsum('bld,de->ble', x, w_in_a[j]), SPLIT_A)
            o_mix = mixer_a(to_heads(q, MIX_HEADS), to_heads(k, A_KV_HEADS), to_heads(v, A_KV_HEADS), sink_a[j], pos)
        elif kind == 1:
            q, k, v, mq, gate = split_proj(jnp.einsum('bld,de->ble', x, w_in_b[j]), SPLIT_B)
            o_mix = mixer_b(to_heads(q, len(B_GROUPS) * MIX_HEADS), to_heads(k, MIX_HEADS), to_heads(v, MIX_HEADS), pos)
        else:
            q, k, v, mq, gate = split_proj(jnp.einsum('bld,de->ble', x, w_in_c[j]), SPLIT_C)
            o_mix = mixer_c(to_heads(q, MIX_HEADS), to_heads(k, MIX_HEADS), to_heads(v, MIX_HEADS), rpb_c[j])
        o_mem = memory_attention(to_heads(mq, MEM_HEADS), mem, w_mkv[i])
        y = jnp.concatenate([o_mix, o_mem], axis=1).transpose(0, 2, 1, 3).reshape(b, L, D_INNER)
        y = y * jax.nn.silu(gate)
        branch = jnp.einsum('ble,ed->bld', y, w_out[i])
        x = layer_norm(ALPHA * x + branch, ln_g[i], ln_b[i])
    return x
```

```python
import functools

import jax
import jax.numpy as jnp
from jax import lax
from jax.experimental import pallas as pl
from jax.experimental.pallas import tpu as pltpu

D_MODEL = 1024
DEPTH = 4
NUM_MIXERS = 3
HEAD_DIM = 128
D_INNER = 2 * D_MODEL
MEM_HEADS = 4
MIX_HEADS = D_INNER // HEAD_DIM - MEM_HEADS
A_KV_HEADS = MIX_HEADS // 3
A_GROUP = MIX_HEADS // A_KV_HEADS
A_WINDOW = 128
B_DILATIONS = (1, 4, 16)
B_HALF_WINDOW = 64
NA_ROWS = 8
NA_COLS = 16
GRID_W = 64
ROPE_THETA = 500000.0
ROPE_DIMS = HEAD_DIM // 4
ROPE_HALF = ROPE_DIMS // 2
LN_EPS = 1e-5
ALPHA = (2 * DEPTH) ** 0.25
NEG_INF = -1e30
SCALE = HEAD_DIM ** -0.5

MIX_W = MIX_HEADS * HEAD_DIM
MEM_W = MEM_HEADS * HEAD_DIM

SUB_R = 4
SUB_C = 8
SUB = SUB_R * SUB_C
SUBS_PER_ROW = GRID_W // SUB_C
QBLK = 128

VMEM_LIMIT = 56 * 1024 * 1024
PROJ_TM = 1024
PROJ_TN = 512
OUT_TM = 512

_f32 = jnp.float32
_bf16 = jnp.bfloat16


def _cparams(n_grid):
    return pltpu.CompilerParams(dimension_semantics=("arbitrary",) * n_grid,
                                vmem_limit_bytes=VMEM_LIMIT)


def _rope(blk, cos, sin):
    lane = lax.broadcasted_iota(jnp.int32, blk.shape, 1)
    up = pltpu.roll(blk, HEAD_DIM - ROPE_HALF, 1)
    down = pltpu.roll(blk, ROPE_HALF, 1)
    swapped = jnp.where(lane < ROPE_HALF, up, down)
    return blk * cos + swapped * sin


def _proj_kernel(x_ref, w_ref, cos_ref, sin_ref, o_ref, xs_ref, tmp_ref, *, perm, dil, n_rope_tiles):
    j = pl.program_id(1)
    tm = xs_ref.shape[0]

    @pl.when(j == 0)
    def _():
        if perm == "subblock":
            grp = SUB * SUBS_PER_ROW
            for g in range(tm // grp):
                for c8 in range(SUBS_PER_ROW):
                    parts = [x_ref[0, g * grp + i4 * GRID_W + c8 * SUB_C:
                                   g * grp + i4 * GRID_W + (c8 + 1) * SUB_C, :] for i4 in range(SUB_R)]
                    dst = (g * SUBS_PER_ROW + c8) * SUB
                    xs_ref[dst:dst + SUB, :] = jnp.concatenate(parts, axis=0).astype(_bf16)
        else:
            xs_ref[...] = x_ref[0].astype(_bf16)

    acc = jnp.dot(xs_ref[...], w_ref[...], preferred_element_type=_f32)

    def write(val, c0):
        c1 = c0 + HEAD_DIM
        if perm == "dilate" and dil > 1:
            n = tm // dil
            tmp_ref[...] = val
            for r in range(dil):
                o_ref[0, r, :, c0:c1] = tmp_ref[pl.ds(r, n, stride=dil), :].astype(o_ref.dtype)
        elif perm == "dilate":
            o_ref[0, 0, :, c0:c1] = val.astype(o_ref.dtype)
        else:
            o_ref[0, :, c0:c1] = val.astype(o_ref.dtype)

    heads = range(acc.shape[1] // HEAD_DIM)

    def plain():
        for c in heads:
            write(acc[:, c * HEAD_DIM:(c + 1) * HEAD_DIM], c * HEAD_DIM)

    if n_rope_tiles == 0:
        plain()
    else:
        @pl.when(j < n_rope_tiles)
        def _():
            cos = cos_ref[...]
            sin = sin_ref[...]
            for c in heads:
                write(_rope(acc[:, c * HEAD_DIM:(c + 1) * HEAD_DIM], cos, sin), c * HEAD_DIM)

        @pl.when(j >= n_rope_tiles)
        def _():
            plain()


def _proj(x, w, cos, sin, *, perm="none", dil=1, n_rope_cols=0):
    b, l, d = x.shape
    n = w.shape[1]
    tm, tn = PROJ_TM, PROJ_TN
    assert l % tm == 0 and n % tn == 0 and n_rope_cols % tn == 0
    tiles = l // tm
    if perm == "dilate":
        out_shape = jax.ShapeDtypeStruct((b, dil, l // dil, n), _bf16)
        out_spec = pl.BlockSpec((1, dil, tm // dil, tn), lambda i, j: (i // tiles, 0, i % tiles, j))
    else:
        out_shape = jax.ShapeDtypeStruct((b, l, n), _bf16)
        out_spec = pl.BlockSpec((1, tm, tn), lambda i, j: (i // tiles, i % tiles, j))
    kern = functools.partial(_proj_kernel, perm=perm, dil=dil, n_rope_tiles=n_rope_cols // tn)
    return pl.pallas_call(
        kern,
        out_shape=out_shape,
        grid=(b * tiles, n // tn),
        in_specs=[
            pl.BlockSpec((1, tm, d), lambda i, j: (i // tiles, i % tiles, 0)),
            pl.BlockSpec((d, tn), lambda i, j: (0, j)),
            pl.BlockSpec((tm, HEAD_DIM), lambda i, j: (i % tiles, 0)),
            pl.BlockSpec((tm, HEAD_DIM), lambda i, j: (i % tiles, 0)),
        ],
        out_specs=out_spec,
        scratch_shapes=[pltpu.VMEM((tm, d), _bf16), pltpu.VMEM((tm, HEAD_DIM), _f32)],
        compiler_params=_cparams(2),
    )(x, w, cos, sin)


def _rope_tables(l):
    inv = ROPE_THETA ** (-jnp.arange(ROPE_HALF, dtype=_f32) / ROPE_HALF)
    ang = jnp.arange(l).astype(_f32)[:, None] * inv[None, :]
    cos, sin = jnp.cos(ang), jnp.sin(ang)
    pad = HEAD_DIM - ROPE_DIMS
    cos_t = jnp.concatenate([cos, cos, jnp.ones((l, pad), _f32)], axis=1)
    sin_t = jnp.concatenate([-sin, sin, jnp.zeros((l, pad), _f32)], axis=1)
    return cos_t, sin_t


def _memkv_kernel(m_ref, w_ref, o_ref):
    o_ref[0] = jnp.dot(m_ref[...].astype(_bf16), w_ref[0], preferred_element_type=_f32).astype(o_ref.dtype)


def _memkv(mem2d, w_mkv):
    rows, d = mem2d.shape
    depth, _, n = w_mkv.shape
    tm = min(rows, PROJ_TM)
    assert rows % tm == 0
    return pl.pallas_call(
        _memkv_kernel,
        out_shape=jax.ShapeDtypeStruct((depth, rows, n), _bf16),
        grid=(depth, rows // tm),
        in_specs=[pl.BlockSpec((tm, d), lambda li, i: (i, 0)),
                  pl.BlockSpec((1, d, n), lambda li, i: (li, 0, 0))],
        out_specs=pl.BlockSpec((1, tm, n), lambda li, i: (li, i, 0)),
        compiler_params=_cparams(2),
    )(mem2d, w_mkv)


def _qk(q, k):
    return lax.dot_general(q, k, (((1,), (1,)), ((), ())), preferred_element_type=_f32)


def _band_block(q, k, v, rel, delta, half):
    s = _qk(q, k) * SCALE
    s = jnp.where(jnp.abs(rel + delta) <= half, s, NEG_INF)
    m = jnp.max(s, axis=-1, keepdims=True)
    return s, m


def _attn_a_kernel(q_ref, k_ref, v_ref, sink_ref, o_ref):
    l = k_ref.shape[1]
    nk = 3 * A_WINDOW
    rel = (lax.broadcasted_iota(jnp.int32, (QBLK, nk), 1) - lax.broadcasted_iota(jnp.int32, (QBLK, nk), 0))

    def body(i, carry):
        q0 = pl.multiple_of(i * QBLK, QBLK)
        ks = pl.multiple_of(jnp.clip(q0 - A_WINDOW, 0, l - nk), QBLK)
        kw = k_ref[0, pl.ds(ks, nk), :]
        vw = v_ref[0, pl.ds(ks, nk), :]
        for a in range(A_GROUP):
            q = q_ref[0, pl.ds(q0, QBLK), a * HEAD_DIM:(a + 1) * HEAD_DIM]
            s, m = _band_block(q, kw, vw, rel, ks - q0, A_WINDOW)
            sink = sink_ref[0, a:a + 1, 0:1]
            m = jnp.maximum(m, sink)
            p = jnp.exp(s - m)
            z = jnp.sum(p, axis=-1, keepdims=True) + jnp.exp(sink - m)
            o = jnp.dot(p.astype(_bf16), vw, preferred_element_type=_f32) / z
            o_ref[0, pl.ds(q0, QBLK), a * HEAD_DIM:(a + 1) * HEAD_DIM] = o.astype(o_ref.dtype)
        return carry

    lax.fori_loop(0, l // QBLK, body, 0)


def _attn_a(h, sink):
    b, l, _ = h.shape
    gw = A_GROUP * HEAD_DIM
    sink_b = jnp.broadcast_to(sink.astype(_f32).reshape(A_KV_HEADS, A_GROUP, 1), (A_KV_HEADS, A_GROUP, HEAD_DIM))
    sink_b = jnp.pad(sink_b, ((0, 0), (0, 8 - A_GROUP), (0, 0)))
    return pl.pallas_call(
        _attn_a_kernel,
        out_shape=jax.ShapeDtypeStruct((b, l, MIX_W), _bf16),
        grid=(b, A_KV_HEADS),
        in_specs=[pl.BlockSpec((1, l, gw), lambda bi, g: (bi, 0, g)),
                  pl.BlockSpec((1, l, HEAD_DIM), lambda bi, g: (bi, 0, MIX_HEADS + g)),
                  pl.BlockSpec((1, l, HEAD_DIM), lambda bi, g: (bi, 0, MIX_HEADS + A_KV_HEADS + g)),
                  pl.BlockSpec((1, 8, HEAD_DIM), lambda bi, g: (g, 0, 0))],
        out_specs=pl.BlockSpec((1, l, gw), lambda bi, g: (bi, 0, g)),
        compiler_params=_cparams(2),
    )(h, h, h, sink_b)


def _attn_b_kernel(*refs):
    qkv = refs[:9]
    o_ref, og_ref, lse_ref = refs[9], refs[10], refs[11]
    l = o_ref.shape[1]
    nk = 4 * B_HALF_WINDOW
    rel = (lax.broadcasted_iota(jnp.int32, (QBLK, nk), 1) - lax.broadcasted_iota(jnp.int32, (QBLK, nk), 0))

    for g, dil in enumerate(B_DILATIONS):
        q_ref, k_ref, v_ref = qkv[3 * g:3 * g + 3]
        n = l // dil
        blocks = n // QBLK

        def body(t, carry, q_ref=q_ref, k_ref=k_ref, v_ref=v_ref, n=n, blocks=blocks, dil=dil, g=g):
            r = t // blocks
            i = t % blocks
            q0 = pl.multiple_of(i * QBLK, QBLK)
            ks = pl.multiple_of(jnp.clip(q0 - B_HALF_WINDOW, 0, n - nk), B_HALF_WINDOW)
            q = q_ref[0, r, pl.ds(q0, QBLK), :]
            kw = k_ref[0, r, pl.ds(ks, nk), :]
            vw = v_ref[0, r, pl.ds(ks, nk), :]
            s, m = _band_block(q, kw, vw, rel, ks - q0, B_HALF_WINDOW)
            p = jnp.exp(s - m)
            z = jnp.sum(p, axis=-1, keepdims=True)
            o = jnp.dot(p.astype(_bf16), vw, preferred_element_type=_f32) / z
            lse = jnp.broadcast_to(m + jnp.log(z), o.shape)
            if dil == 1:
                og_ref[g, pl.ds(q0, QBLK), :] = o
                lse_ref[g, pl.ds(q0, QBLK), :] = lse
            else:
                og_ref[g, pl.ds(q0 * dil + r, QBLK, stride=dil), :] = o
                lse_ref[g, pl.ds(q0 * dil + r, QBLK, stride=dil), :] = lse
            return carry

        lax.fori_loop(0, dil * blocks, body, 0)

    def combine(i, carry):
        r0 = pl.multiple_of(i * QBLK, QBLK)
        ls = [lse_ref[g, pl.ds(r0, QBLK), :] for g in range(3)]
        m = jnp.maximum(jnp.maximum(ls[0], ls[1]), ls[2])
        ws = [jnp.exp(x - m) for x in ls]
        tot = ws[0] + ws[1] + ws[2]
        acc = ws[0] * og_ref[0, pl.ds(r0, QBLK), :]
        acc += ws[1] * og_ref[1, pl.ds(r0, QBLK), :]
        acc += ws[2] * og_ref[2, pl.ds(r0, QBLK), :]
        o_ref[0, pl.ds(r0, QBLK), :] = (acc / tot).astype(o_ref.dtype)
        return carry

    lax.fori_loop(0, l // QBLK, combine, 0)


def _attn_b(hs):
    b = hs[0].shape[0]
    l = hs[0].shape[1] * hs[0].shape[2]
    in_specs, args = [], []
    for hg, dil in zip(hs, B_DILATIONS):
        for part in range(3):
            in_specs.append(pl.BlockSpec((1, dil, l // dil, HEAD_DIM),
                                         lambda bi, h, part=part: (bi, 0, 0, part * MIX_HEADS + h)))
            args.append(hg)
    return pl.pallas_call(
        _attn_b_kernel,
        out_shape=jax.ShapeDtypeStruct((b, l, MIX_W), _bf16),
        grid=(b, MIX_HEADS),
        in_specs=in_specs,
        out_specs=pl.BlockSpec((1, l, HEAD_DIM), lambda bi, h: (bi, 0, h)),
        scratch_shapes=[pltpu.VMEM((3, l, HEAD_DIM), _f32), pltpu.VMEM((3, l, HEAD_DIM), _f32)],
        compiler_params=_cparams(2),
    )(*args)


C_KSEG = 4 * SUB
C_NSEG = 4
C_PAD = SUB


def _attn_c_kernel(q_ref, k_ref, v_ref, bias_ref, mask_ref, o_ref, kp_ref, vp_ref):
    l = k_ref.shape[1]
    zeros = jnp.zeros((C_PAD, HEAD_DIM), _bf16)
    for ref, src in ((kp_ref, k_ref), (vp_ref, v_ref)):
        ref[0:C_PAD, :] = zeros
        ref[C_PAD + l:C_PAD + l + C_PAD, :] = zeros
        ref[C_PAD:C_PAD + l, :] = src[0]
    n_rb = l // GRID_W // NA_ROWS
    n_cb = GRID_W // NA_COLS
    n_r4 = l // GRID_W // SUB_R
    bias = bias_ref[0]

    def body(t, carry):
        rb = t // n_cb
        cb = t % n_cb
        rv = jnp.where(rb == 0, 0, jnp.where(rb == n_rb - 1, 2, 1))
        cv = jnp.where(cb == 0, 0, jnp.where(cb == n_cb - 1, 2, 1))
        q_starts = [pl.multiple_of(((2 * rb + a) * SUBS_PER_ROW + 2 * cb) * SUB, 2 * SUB) for a in range(2)]
        q = jnp.concatenate([q_ref[0, pl.ds(qs, 2 * SUB), :] for qs in q_starts], axis=0)
        kparts, vparts = [], []
        for ar in range(C_NSEG):
            r4 = jnp.clip(2 * rb - 1 + ar, 0, n_r4 - 1)
            st = pl.multiple_of(C_PAD + (r4 * SUBS_PER_ROW + 2 * cb - 1) * SUB, SUB)
            kparts.append(kp_ref[pl.ds(st, C_KSEG), :])
            vparts.append(vp_ref[pl.ds(st, C_KSEG), :])
        kw = jnp.concatenate(kparts, axis=0)
        vw = jnp.concatenate(vparts, axis=0)
        s = _qk(q, kw) * SCALE + bias + mask_ref[rv * 3 + cv]
        m = jnp.max(s, axis=-1, keepdims=True)
        p = jnp.exp(s - m)
        z = jnp.sum(p, axis=-1, keepdims=True)
        o = (jnp.dot(p.astype(_bf16), vw, preferred_element_type=_f32) / z).astype(o_ref.dtype)
        for a in range(2):
            o_ref[0, pl.ds(q_starts[a], 2 * SUB), :] = o[a * 2 * SUB:(a + 1) * 2 * SUB, :]
        return carry

    lax.fori_loop(0, n_rb * n_cb, body, 0)


def _c_geometry():
    ql = jnp.arange(QBLK)
    qa, qc, qi, qj = ql // 64, (ql // 32) % 2, (ql // 8) % 4, ql % 8
    q_row = SUB_R * qa + qi
    q_col = SUB_C * qc + qj
    kl = jnp.arange(C_NSEG * C_KSEG)
    ka, kc, ki, kj = kl // C_KSEG, (kl // SUB) % 4, (kl // 8) % 4, kl % 8
    k_row = SUB_R * (ka - 1) + ki
    k_col = SUB_C * (kc - 1) + kj
    return q_row, q_col, k_row, k_col


def _c_bias(rpb):
    q_row, q_col, k_row, k_col = _c_geometry()
    dr = jnp.clip(k_row[None, :] - q_row[:, None] + NA_ROWS - 1, 0, 2 * NA_ROWS - 2)
    dc = jnp.clip(k_col[None, :] - q_col[:, None] + NA_COLS - 1, 0, 2 * NA_COLS - 2)
    return rpb.astype(_f32)[:, dr, dc]


def _c_masks(rows):
    q_row, q_col, k_row, k_col = _c_geometry()
    out = []
    for rb in (0, 1, rows // NA_ROWS - 1):
        qr, kr = NA_ROWS * rb + q_row, NA_ROWS * rb + k_row
        rs = jnp.clip(qr - NA_ROWS // 2, 0, rows - NA_ROWS)
        rvalid = (kr[None, :] >= rs[:, None]) & (kr[None, :] < rs[:, None] + NA_ROWS) & (kr[None, :] >= 0) & (kr[None, :] < rows)
        for cb in (0, 1, GRID_W // NA_COLS - 1):
            qc, kc = NA_COLS * cb + q_col, NA_COLS * cb + k_col
            cs = jnp.clip(qc - NA_COLS // 2, 0, GRID_W - NA_COLS)
            cvalid = (kc[None, :] >= cs[:, None]) & (kc[None, :] < cs[:, None] + NA_COLS) & (kc[None, :] >= 0) & (kc[None, :] < GRID_W)
            out.append(jnp.where(rvalid & cvalid, 0.0, NEG_INF).astype(_f32))
    return jnp.stack(out)


def _attn_c(h, rpb):
    b, l, _ = h.shape
    bias = _c_bias(rpb)
    masks = _c_masks(l // GRID_W)
    nkeys = C_NSEG * C_KSEG
    return pl.pallas_call(
        _attn_c_kernel,
        out_shape=jax.ShapeDtypeStruct((b, l, MIX_W), _bf16),
        grid=(b, MIX_HEADS),
        in_specs=[pl.BlockSpec((1, l, HEAD_DIM), lambda bi, h: (bi, 0, h)),
                  pl.BlockSpec((1, l, HEAD_DIM), lambda bi, h: (bi, 0, MIX_HEADS + h)),
                  pl.BlockSpec((1, l, HEAD_DIM), lambda bi, h: (bi, 0, 2 * MIX_HEADS + h)),
                  pl.BlockSpec((1, QBLK, nkeys), lambda bi, h: (h, 0, 0)),
                  pl.BlockSpec((9, QBLK, nkeys), lambda bi, h: (0, 0, 0))],
        out_specs=pl.BlockSpec((1, l, HEAD_DIM), lambda bi, h: (bi, 0, h)),
        scratch_shapes=[pltpu.VMEM((l + 2 * C_PAD, HEAD_DIM), _bf16),
                        pltpu.VMEM((l + 2 * C_PAD, HEAD_DIM), _bf16)],
        compiler_params=_cparams(2),
    )(h, h, h, bias, masks)


def _subblock_chunks(tm):
    grp = SUB * SUBS_PER_ROW
    pairs = []
    for g in range(tm // grp):
        for c8 in range(SUBS_PER_ROW):
            for i4 in range(SUB_R):
                pairs.append((g * grp + i4 * GRID_W + c8 * SUB_C, (g * SUBS_PER_ROW + c8) * SUB + i4 * SUB_C))
    return pairs


def _out_kernel(om_ref, gm_ref, mq_ref, kv_ref, x_ref, w_ref, g_ref, b_ref, o_ref, y_ref, xs_ref, *, subblock):
    tm = om_ref.shape[1]

    def silu_gate(c0, c1):
        gate = gm_ref[0, :, c0:c1].astype(_f32)
        return gate / (1.0 + jnp.exp(-gate))

    for c0 in range(0, MIX_W, MEM_W):
        y_ref[:, c0:c0 + MEM_W] = (om_ref[0, :, c0:c0 + MEM_W].astype(_f32) * silu_gate(c0, c0 + MEM_W)).astype(_bf16)
    for h in range(MEM_HEADS):
        c0, c1 = h * HEAD_DIM, (h + 1) * HEAD_DIM
        s = _qk(mq_ref[0, :, c0:c1], kv_ref[0, :, c0:c1]) * SCALE
        m = jnp.max(s, axis=-1, keepdims=True)
        p = jnp.exp(s - m)
        z = jnp.sum(p, axis=-1, keepdims=True)
        o = jnp.dot(p.astype(_bf16), kv_ref[0, :, MEM_W + c0:MEM_W + c1], preferred_element_type=_f32) / z
        y_ref[:, MIX_W + c0:MIX_W + c1] = (o * silu_gate(MIX_W + c0, MIX_W + c1)).astype(_bf16)
    branch = jnp.dot(y_ref[...], w_ref[...], preferred_element_type=_f32)
    if subblock:
        for nat, sub in _subblock_chunks(tm):
            xs_ref[sub:sub + SUB_C, :] = x_ref[0, nat:nat + SUB_C, :]
        xin = xs_ref[...]
    else:
        xin = x_ref[0]
    z = ALPHA * xin + branch
    mu = jnp.mean(z, axis=-1, keepdims=True)
    zc = z - mu
    var = jnp.mean(zc * zc, axis=-1, keepdims=True)
    out = zc * lax.rsqrt(var + LN_EPS) * g_ref[...] + b_ref[...]
    if subblock:
        xs_ref[...] = out
        for nat, sub in _subblock_chunks(tm):
            o_ref[0, nat:nat + SUB_C, :] = xs_ref[sub:sub + SUB_C, :]
    else:
        o_ref[0] = out


def _out(o_mix, hg, kv, x, w_out, ln_g, ln_b, *, subblock):
    b, l, d = x.shape
    tm = OUT_TM
    mlen = kv.shape[1]
    kern = functools.partial(_out_kernel, subblock=subblock)
    return pl.pallas_call(
        kern,
        out_shape=jax.ShapeDtypeStruct((b, l, d), _f32),
        grid=(b, l // tm),
        in_specs=[pl.BlockSpec((1, tm, MIX_W), lambda bi, i: (bi, i, 0)),
                  pl.BlockSpec((1, tm, D_INNER), lambda bi, i: (bi, i, 0)),
                  pl.BlockSpec((1, tm, MEM_W), lambda bi, i: (bi, i, D_INNER // MEM_W)),
                  pl.BlockSpec((1, mlen, 2 * MEM_W), lambda bi, i: (bi, 0, 0)),
                  pl.BlockSpec((1, tm, d), lambda bi, i: (bi, i, 0)),
                  pl.BlockSpec((D_INNER, d), lambda bi, i: (0, 0)),
                  pl.BlockSpec((1, d), lambda bi, i: (0, 0)),
                  pl.BlockSpec((1, d), lambda bi, i: (0, 0))],
        out_specs=pl.BlockSpec((1, tm, d), lambda bi, i: (bi, i, 0)),
        scratch_shapes=[pltpu.VMEM((tm, D_INNER), _bf16), pltpu.VMEM((tm, d), _f32)],
        compiler_params=_cparams(2),
    )(o_mix, hg, hg, kv, x, w_out, ln_g.reshape(1, d), ln_b.reshape(1, d))


def _split_cols(w, sizes):
    out, c = [], 0
    for s in sizes:
        out.append(w[:, c:c + s])
        c += s
    return out


def kernel(x, mem, w_in_a, sink_a, w_in_b, w_in_c, rpb_c, w_mkv, w_out, ln_g, ln_b):
    b, l, d = x.shape
    mlen = mem.shape[1]
    kv_all = _memkv(mem.reshape(b * mlen, d), w_mkv.astype(_bf16)).reshape(DEPTH, b, mlen, 2 * MEM_W)
    cos1, sin1 = _rope_tables(l)

    for i in range(DEPTH):
        kind, j = i % NUM_MIXERS, i // NUM_MIXERS
        if kind == 0:
            wq, wk, wv, wm, wg = _split_cols(w_in_a[j].astype(_bf16), (MIX_W, MEM_W, MEM_W, MEM_W, D_INNER))
            h = _proj(x, jnp.concatenate([wq, wk, wv], axis=1), cos1, sin1, n_rope_cols=MIX_W + MEM_W)
            o_mix = _attn_a(h, sink_a[j])
        elif kind == 1:
            wqs = _split_cols(w_in_b[j].astype(_bf16), (MIX_W, MIX_W, MIX_W, MIX_W, MIX_W, MEM_W, D_INNER))
            wk, wv, wm, wg = wqs[3:]
            hs = []
            for g, dil in enumerate(B_DILATIONS):
                hs.append(_proj(x, jnp.concatenate([wqs[g], wk, wv], axis=1), cos1, sin1,
                                perm="dilate", dil=dil, n_rope_cols=2 * MIX_W))
            o_mix = _attn_b(hs)
        else:
            wq, wk, wv, wm, wg = _split_cols(w_in_c[j].astype(_bf16), (MIX_W, MIX_W, MIX_W, MEM_W, D_INNER))
            h = _proj(x, jnp.concatenate([wq, wk, wv], axis=1), cos1, sin1, perm="subblock")
            o_mix = _attn_c(h, rpb_c[j])
        perm = "subblock" if kind == 2 else "none"
        hg = _proj(x, jnp.concatenate([wg, wm], axis=1), cos1, sin1, perm=perm)
        x = _out(o_mix, hg, kv_all[i], x, w_out[i].astype(_bf16), ln_g[i], ln_b[i], subblock=(kind == 2))
    return x
```

```python
import functools
import math

import jax
import jax.numpy as jnp
from jax import lax
from jax.experimental import pallas as pl
from jax.experimental.pallas import tpu as pltpu

D_MODEL = 1024
DEPTH = 4
NUM_MIXERS = 3
HEAD_DIM = 128
D_INNER = 2 * D_MODEL
MEM_HEADS = 4
MIX_HEADS = D_INNER // HEAD_DIM - MEM_HEADS
A_KV_HEADS = MIX_HEADS // 3
A_GROUP = MIX_HEADS // A_KV_HEADS
A_WINDOW = 128
B_DILATIONS = (1, 4, 16)
B_HALF_WINDOW = 64
NA_ROWS = 8
NA_COLS = 16
GRID_W = 64
ROPE_THETA = 500000.0
ROPE_DIMS = HEAD_DIM // 4
ROPE_HALF = ROPE_DIMS // 2
LN_EPS = 1e-5
ALPHA = (2 * DEPTH) ** 0.25
NEG_INF = -1e30
LOG2E = math.log2(math.e)
QSCALE = HEAD_DIM ** -0.5 * LOG2E

MIX_W = MIX_HEADS * HEAD_DIM
MEM_W = MEM_HEADS * HEAD_DIM

SUB_R = 4
SUB_C = 8
SUB = SUB_R * SUB_C
SUBS_PER_ROW = GRID_W // SUB_C
QBLK = 128

VMEM_LIMIT = 56 * 1024 * 1024
PROJ_TM = 1024
PROJ_TN = 512
OUT_TM = 512

_f32 = jnp.float32
_bf16 = jnp.bfloat16


def _cparams(n_grid):
    return pltpu.CompilerParams(dimension_semantics=("arbitrary",) * n_grid,
                                vmem_limit_bytes=VMEM_LIMIT)


def _rope(blk, cos, sin):
    lane = lax.broadcasted_iota(jnp.int32, blk.shape, 1)
    up = pltpu.roll(blk, HEAD_DIM - ROPE_HALF, 1)
    down = pltpu.roll(blk, ROPE_HALF, 1)
    swapped = jnp.where(lane < ROPE_HALF, up, down)
    return blk * cos + swapped * sin


def _proj_kernel(*refs, perm, dil, n_q_tiles, n_rope_tiles):
    if n_rope_tiles:
        x_ref, w_ref, cos_ref, sin_ref, o_ref, xs_ref, tmp_ref = refs
    else:
        x_ref, w_ref, o_ref, xs_ref, tmp_ref = refs
    j = pl.program_id(1)
    tm = xs_ref.shape[0]

    @pl.when(j == 0)
    def _():
        if perm == "subblock":
            grp = SUB * SUBS_PER_ROW
            for g in range(tm // grp):
                for c8 in range(SUBS_PER_ROW):
                    parts = [x_ref[0, g * grp + i4 * GRID_W + c8 * SUB_C:
                                   g * grp + i4 * GRID_W + (c8 + 1) * SUB_C, :] for i4 in range(SUB_R)]
                    dst = (g * SUBS_PER_ROW + c8) * SUB
                    xs_ref[dst:dst + SUB, :] = jnp.concatenate(parts, axis=0).astype(_bf16)
        else:
            xs_ref[...] = x_ref[0].astype(_bf16)

    acc = jnp.dot(xs_ref[...], w_ref[...], preferred_element_type=_f32)

    def write(val, c0):
        c1 = c0 + HEAD_DIM
        if perm == "dilate":
            n = tm // dil
            tmp_ref[...] = val
            for r in range(dil):
                o_ref[0, r, :, c0:c1] = tmp_ref[pl.ds(r, n, stride=dil), :].astype(o_ref.dtype)
        else:
            o_ref[0, :, c0:c1] = val.astype(o_ref.dtype)

    heads = range(acc.shape[1] // HEAD_DIM)

    def plain(scale):
        for c in heads:
            val = acc[:, c * HEAD_DIM:(c + 1) * HEAD_DIM]
            write(val if scale is None else val * scale, c * HEAD_DIM)

    if n_rope_tiles:
        @pl.when(j < n_rope_tiles)
        def _():
            cos = cos_ref[0]
            sin = sin_ref[0]
            for c in heads:
                write(_rope(acc[:, c * HEAD_DIM:(c + 1) * HEAD_DIM], cos, sin), c * HEAD_DIM)

        @pl.when(j >= n_rope_tiles)
        def _():
            plain(None)
    elif n_q_tiles:
        @pl.when(j < n_q_tiles)
        def _():
            plain(QSCALE)

        @pl.when(j >= n_q_tiles)
        def _():
            plain(None)
    else:
        plain(None)


def _proj(x, w, tables=None, *, perm="none", dil=1, n_q_cols=0, n_rope_cols=0):
    b, l, d = x.shape
    n = w.shape[1]
    tm, tn = PROJ_TM, PROJ_TN
    assert l % tm == 0 and n % tn == 0 and n_rope_cols % tn == 0 and n_q_cols % tn == 0
    tiles = l // tm
    n_q_tiles, n_rope_tiles = n_q_cols // tn, n_rope_cols // tn
    if perm == "dilate":
        out_shape = jax.ShapeDtypeStruct((b, dil, l // dil, n), _bf16)
        out_spec = pl.BlockSpec((1, dil, tm // dil, tn), lambda i, j: (i // tiles, 0, i % tiles, j))
    else:
        out_shape = jax.ShapeDtypeStruct((b, l, n), _bf16)
        out_spec = pl.BlockSpec((1, tm, tn), lambda i, j: (i // tiles, i % tiles, j))
    in_specs = [pl.BlockSpec((1, tm, d), lambda i, j: (i // tiles, i % tiles, 0)),
                pl.BlockSpec((d, tn), lambda i, j: (0, j))]
    args = [x, w]
    if n_rope_tiles:
        tab_spec = pl.BlockSpec((1, tm, HEAD_DIM), lambda i, j: (jnp.where(j < n_q_tiles, 0, 1), i % tiles, 0))
        in_specs += [tab_spec, tab_spec]
        args += list(tables)
    kern = functools.partial(_proj_kernel, perm=perm, dil=dil, n_q_tiles=n_q_tiles, n_rope_tiles=n_rope_tiles)
    return pl.pallas_call(
        kern,
        out_shape=out_shape,
        grid=(b * tiles, n // tn),
        in_specs=in_specs,
        out_specs=out_spec,
        scratch_shapes=[pltpu.VMEM((tm, d), _bf16), pltpu.VMEM((tm, HEAD_DIM), _f32)],
        compiler_params=_cparams(2),
    )(*args)


def _rope_tables(l):
    inv = ROPE_THETA ** (-jnp.arange(ROPE_HALF, dtype=_f32) / ROPE_HALF)
    ang = jnp.arange(l).astype(_f32)[:, None] * inv[None, :]
    cos, sin = jnp.cos(ang), jnp.sin(ang)
    pad = HEAD_DIM - ROPE_DIMS
    cos_t = jnp.concatenate([cos, cos, jnp.ones((l, pad), _f32)], axis=1)
    sin_t = jnp.concatenate([-sin, sin, jnp.zeros((l, pad), _f32)], axis=1)
    return jnp.stack([cos_t * QSCALE, cos_t]), jnp.stack([sin_t * QSCALE, sin_t])


def _memkv_kernel(m_ref, w_ref, o_ref):
    o_ref[0] = jnp.dot(m_ref[...].astype(_bf16), w_ref[0], preferred_element_type=_f32).astype(o_ref.dtype)


def _memkv(mem2d, w_mkv):
    rows, d = mem2d.shape
    depth, _, n = w_mkv.shape
    tm = min(rows, PROJ_TM)
    assert rows % tm == 0
    return pl.pallas_call(
        _memkv_kernel,
        out_shape=jax.ShapeDtypeStruct((depth, rows, n), _bf16),
        grid=(depth, rows // tm),
        in_specs=[pl.BlockSpec((tm, d), lambda li, i: (i, 0)),
                  pl.BlockSpec((1, d, n), lambda li, i: (li, 0, 0))],
        out_specs=pl.BlockSpec((1, tm, n), lambda li, i: (li, i, 0)),
        compiler_params=_cparams(2),
    )(mem2d, w_mkv)


def _qk(q, k):
    return lax.dot_general(q, k, (((1,), (1,)), ((), ())), preferred_element_type=_f32)


def _band_masks(mask_ref, nk, half, deltas):
    rel = (lax.broadcasted_iota(jnp.int32, (QBLK, nk), 1) - lax.broadcasted_iota(jnp.int32, (QBLK, nk), 0))
    for v, delta in enumerate(deltas):
        mask_ref[v] = jnp.where(jnp.abs(rel + delta) <= half, 0.0, NEG_INF).astype(_f32)


def _edge_variant(i, n):
    return jnp.where(i == 0, 0, jnp.where(i == n - 1, 2, 1))


def _softmax_pv(scores, values, floors=None):
    ms = [jnp.max(s, axis=-1, keepdims=True) for s in scores]
    if floors is not None:
        ms = [jnp.maximum(m, f) for m, f in zip(ms, floors)]
    ps = [jnp.exp2(s - m).astype(_bf16) for s, m in zip(scores, ms)]
    accs = [jnp.dot(p, v, preferred_element_type=_f32) for p, v in zip(ps, values)]
    return ms, accs


A_UNROLL = 2


def _attn_a_kernel(q_ref, k_ref, v_ref, sink_ref, o_ref, v1_ref, mask_ref):
    l = k_ref.shape[1]
    nk = 3 * A_WINDOW
    nblk = l // QBLK
    v1_ref[:, 0:HEAD_DIM] = v_ref[0]
    v1_ref[:, HEAD_DIM:2 * HEAD_DIM] = jnp.ones((l, HEAD_DIM), _bf16)
    _band_masks(mask_ref, nk, A_WINDOW, (0, -A_WINDOW, -2 * A_WINDOW))
    sinks = [sink_ref[0, a:a + 1, 0:1] * LOG2E for a in range(A_GROUP)]

    def body(t, carry):
        scores, values, floors, dests = [], [], [], []
        for u in range(A_UNROLL):
            i = t * A_UNROLL + u
            q0 = pl.multiple_of(i * QBLK, QBLK)
            ks = pl.multiple_of(jnp.clip(q0 - A_WINDOW, 0, l - nk), QBLK)
            kw = k_ref[0, pl.ds(ks, nk), :]
            vw = v1_ref[pl.ds(ks, nk), :]
            msk = mask_ref[_edge_variant(i, nblk)]
            for a in range(A_GROUP):
                q = q_ref[0, pl.ds(q0, QBLK), a * HEAD_DIM:(a + 1) * HEAD_DIM]
                scores.append(_qk(q, kw) + msk)
                values.append(vw)
                floors.append(sinks[a])
                dests.append((q0, a))
        ms, accs = _softmax_pv(scores, values, floors)
        for m, acc, sink, (q0, a) in zip(ms, accs, floors, dests):
            z = acc[:, HEAD_DIM:] + jnp.exp2(sink - m)
            o = acc[:, :HEAD_DIM] / z
            o_ref[0, pl.ds(q0, QBLK), a * HEAD_DIM:(a + 1) * HEAD_DIM] = o.astype(o_ref.dtype)
        return carry

    lax.fori_loop(0, nblk // A_UNROLL, body, 0)


def _attn_a(h, sink):
    b, l, _ = h.shape
    gw = A_GROUP * HEAD_DIM
    sink_b = jnp.broadcast_to(sink.astype(_f32).reshape(A_KV_HEADS, A_GROUP, 1), (A_KV_HEADS, A_GROUP, HEAD_DIM))
    sink_b = jnp.pad(sink_b, ((0, 0), (0, 8 - A_GROUP), (0, 0)))
    return pl.pallas_call(
        _attn_a_kernel,
        out_shape=jax.ShapeDtypeStruct((b, l, MIX_W), _bf16),
        grid=(b, A_KV_HEADS),
        in_specs=[pl.BlockSpec((1, l, gw), lambda bi, g: (bi, 0, g)),
                  pl.BlockSpec((1, l, HEAD_DIM), lambda bi, g: (bi, 0, MIX_HEADS + g)),
                  pl.BlockSpec((1, l, HEAD_DIM), lambda bi, g: (bi, 0, MIX_HEADS + A_KV_HEADS + g)),
                  pl.BlockSpec((1, 8, HEAD_DIM), lambda bi, g: (g, 0, 0))],
        out_specs=pl.BlockSpec((1, l, gw), lambda bi, g: (bi, 0, g)),
        scratch_shapes=[pltpu.VMEM((l, 2 * HEAD_DIM), _bf16),
                        pltpu.VMEM((3, QBLK, 3 * A_WINDOW), _f32)],
        compiler_params=_cparams(2),
    )(h, h, h, sink_b)


B_UNROLL = 8


def _attn_b_kernel(q0_ref, k_ref, v_ref, q1_ref, q2_ref, o_ref, kd_ref, v1_ref, tmp_ref, og_ref, lse_ref, mask_ref):
    l = o_ref.shape[1]
    nk = 4 * B_HALF_WINDOW
    q_refs = (q0_ref, q1_ref, q2_ref)
    _band_masks(mask_ref, nk, B_HALF_WINDOW, (0, -B_HALF_WINDOW, -2 * B_HALF_WINDOW))

    v1_ref[0, :, 0:HEAD_DIM] = v_ref[0]
    for g in range(3):
        v1_ref[g, :, HEAD_DIM:2 * HEAD_DIM] = jnp.ones((l, HEAD_DIM), _bf16)
    for src, is_key in ((k_ref, True), (v_ref, False)):
        tmp_ref[...] = src[0].astype(_f32)
        for g, dil in enumerate(B_DILATIONS):
            if dil == 1:
                continue
            n = l // dil
            for r in range(dil):
                rows = tmp_ref[pl.ds(r, n, stride=dil), :].astype(_bf16)
                if is_key:
                    kd_ref[g - 1, r * n:(r + 1) * n, :] = rows
                else:
                    v1_ref[g, r * n:(r + 1) * n, 0:HEAD_DIM] = rows

    for g, dil in enumerate(B_DILATIONS):
        n = l // dil
        blocks = n // QBLK

        def body(t, carry, g=g, dil=dil, n=n, blocks=blocks):
            scores, values, dests = [], [], []
            for u in range(B_UNROLL):
                bt = t * B_UNROLL + u
                r = bt // blocks
                i = bt % blocks
                q0 = pl.multiple_of(i * QBLK, QBLK)
                ks = pl.multiple_of(jnp.clip(q0 - B_HALF_WINDOW, 0, n - nk), B_HALF_WINDOW)
                base = pl.multiple_of(r * n + ks, B_HALF_WINDOW)
                if g == 0:
                    q = q0_ref[0, pl.ds(q0, QBLK), :]
                    kw = k_ref[0, pl.ds(ks, nk), :]
                else:
                    q = q_refs[g][0, r, pl.ds(q0, QBLK), :]
                    kw = kd_ref[g - 1, pl.ds(base, nk), :]
                scores.append(_qk(q, kw) + mask_ref[_edge_variant(i, blocks)])
                values.append(v1_ref[g, pl.ds(base, nk), :])
                dests.append((q0, r))
            ms, accs = _softmax_pv(scores, values)
            for m, acc, (q0, r) in zip(ms, accs, dests):
                z = acc[:, HEAD_DIM:]
                o = acc[:, :HEAD_DIM] / z
                lse = m + jnp.log2(z)
                if dil == 1:
                    og_ref[g, pl.ds(q0, QBLK), :] = o
                    lse_ref[g, pl.ds(q0, QBLK), :] = lse
                else:
                    og_ref[g, pl.ds(q0 * dil + r, QBLK, stride=dil), :] = o
                    lse_ref[g, pl.ds(q0 * dil + r, QBLK, stride=dil), :] = lse
            return carry

        lax.fori_loop(0, dil * blocks // B_UNROLL, body, 0)

    def combine(i, carry):
        r0 = pl.multiple_of(i * QBLK, QBLK)
        ls = [lse_ref[g, pl.ds(r0, QBLK), :] for g in range(3)]
        m = jnp.maximum(jnp.maximum(ls[0], ls[1]), ls[2])
        ws = [jnp.exp2(x - m) for x in ls]
        tot = ws[0] + ws[1] + ws[2]
        acc = ws[0] * og_ref[0, pl.ds(r0, QBLK), :]
        acc += ws[1] * og_ref[1, pl.ds(r0, QBLK), :]
        acc += ws[2] * og_ref[2, pl.ds(r0, QBLK), :]
        o_ref[0, pl.ds(r0, QBLK), :] = (acc / tot).astype(o_ref.dtype)
        return carry

    lax.fori_loop(0, l // QBLK, combine, 0)


def _attn_b(h0, hq1, hq2):
    b, l, _ = h0.shape
    in_specs = [pl.BlockSpec((1, l, HEAD_DIM), lambda bi, h: (bi, 0, h)),
                pl.BlockSpec((1, l, HEAD_DIM), lambda bi, h: (bi, 0, MIX_HEADS + h)),
                pl.BlockSpec((1, l, HEAD_DIM), lambda bi, h: (bi, 0, 2 * MIX_HEADS + h))]
    for dil in B_DILATIONS[1:]:
        in_specs.append(pl.BlockSpec((1, dil, l // dil, HEAD_DIM), lambda bi, h: (bi, 0, 0, h)))
    return pl.pallas_call(
        _attn_b_kernel,
        out_shape=jax.ShapeDtypeStruct((b, l, MIX_W), _bf16),
        grid=(b, MIX_HEADS),
        in_specs=in_specs,
        out_specs=pl.BlockSpec((1, l, HEAD_DIM), lambda bi, h: (bi, 0, h)),
        scratch_shapes=[pltpu.VMEM((2, l, HEAD_DIM), _bf16),
                        pltpu.VMEM((3, l, 2 * HEAD_DIM), _bf16),
                        pltpu.VMEM((l, HEAD_DIM), _f32),
                        pltpu.VMEM((3, l, HEAD_DIM), _f32),
                        pltpu.VMEM((3, l, HEAD_DIM), _f32),
                        pltpu.VMEM((3, QBLK, 4 * B_HALF_WINDOW), _f32)],
        compiler_params=_cparams(2),
    )(h0, h0, h0, hq1, hq2)


C_KSEG = 4 * SUB
C_NSEG = 4
C_PAD = SUB
C_NKEYS = C_NSEG * C_KSEG


def _attn_c_kernel(q_ref, k_ref, v_ref, bias_ref, mask_ref, o_ref, kp_ref, vp_ref, bm_ref):
    l = k_ref.shape[1]
    n_rb = l // GRID_W // NA_ROWS
    n_cb = GRID_W // NA_COLS
    n_r4 = l // GRID_W // SUB_R
    kp_ref[0:C_PAD, :] = jnp.zeros((C_PAD, HEAD_DIM), _bf16)
    kp_ref[C_PAD + l:C_PAD + l + C_PAD, :] = jnp.zeros((C_PAD, HEAD_DIM), _bf16)
    kp_ref[C_PAD:C_PAD + l, :] = k_ref[0]
    vp_ref[0:C_PAD, 0:HEAD_DIM] = jnp.zeros((C_PAD, HEAD_DIM), _bf16)
    vp_ref[C_PAD + l:C_PAD + l + C_PAD, 0:HEAD_DIM] = jnp.zeros((C_PAD, HEAD_DIM), _bf16)
    vp_ref[C_PAD:C_PAD + l, 0:HEAD_DIM] = v_ref[0]
    vp_ref[:, HEAD_DIM:2 * HEAD_DIM] = jnp.ones((l + 2 * C_PAD, HEAD_DIM), _bf16)
    for v in range(9):
        bm_ref[v] = bias_ref[0] + mask_ref[v]

    def body(rb, carry):
        rv = _edge_variant(rb, n_rb)
        scores, values, dests = [], [], []
        for cb in range(n_cb):
            cv = 0 if cb == 0 else (2 if cb == n_cb - 1 else 1)
            q_starts = [pl.multiple_of(((2 * rb + a) * SUBS_PER_ROW + 2 * cb) * SUB, 2 * SUB) for a in range(2)]
            q = jnp.concatenate([q_ref[0, pl.ds(qs, 2 * SUB), :] for qs in q_starts], axis=0)
            kparts, vparts = [], []
            for ar in range(C_NSEG):
                r4 = jnp.clip(2 * rb - 1 + ar, 0, n_r4 - 1)
                st = pl.multiple_of(C_PAD + (r4 * SUBS_PER_ROW + 2 * cb - 1) * SUB, SUB)
                kparts.append(kp_ref[pl.ds(st, C_KSEG), :])
                vparts.append(vp_ref[pl.ds(st, C_KSEG), :])
            scores.append(_qk(q, jnp.concatenate(kparts, axis=0)) + bm_ref[rv * 3 + cv])
            values.append(jnp.concatenate(vparts, axis=0))
            dests.append(q_starts)
        ms, accs = _softmax_pv(scores, values)
        for acc, q_starts in zip(accs, dests):
            o = (acc[:, :HEAD_DIM] / acc[:, HEAD_DIM:]).astype(o_ref.dtype)
            for a in range(2):
                o_ref[0, pl.ds(q_starts[a], 2 * SUB), :] = o[a * 2 * SUB:(a + 1) * 2 * SUB, :]
        return carry

    lax.fori_loop(0, n_rb, body, 0)


def _c_geometry():
    ql = jnp.arange(QBLK)
    qa, qc, qi, qj = ql // 64, (ql // 32) % 2, (ql // 8) % 4, ql % 8
    q_row = SUB_R * qa + qi
    q_col = SUB_C * qc + qj
    kl = jnp.arange(C_NKEYS)
    ka, kc, ki, kj = kl // C_KSEG, (kl // SUB) % 4, (kl // 8) % 4, kl % 8
    k_row = SUB_R * (ka - 1) + ki
    k_col = SUB_C * (kc - 1) + kj
    return q_row, q_col, k_row, k_col


def _c_bias(rpb):
    q_row, q_col, k_row, k_col = _c_geometry()
    nr, nc = 2 * NA_ROWS - 1, 2 * NA_COLS - 1
    dr = jnp.clip(k_row[None, :] - q_row[:, None] + NA_ROWS - 1, 0, nr - 1)
    dc = jnp.clip(k_col[None, :] - q_col[:, None] + NA_COLS - 1, 0, nc - 1)
    oh_r = (dr[:, :, None] == jnp.arange(nr)).astype(_f32)
    oh_c = (dc[:, :, None] == jnp.arange(nc)).astype(_f32)
    rows = jnp.einsum("hrc,qkr->hqkc", rpb.astype(_f32), oh_r, precision=lax.Precision.HIGHEST)
    return jnp.sum(rows * oh_c[None], axis=-1) * LOG2E


def _c_masks(rows):
    q_row, q_col, k_row, k_col = _c_geometry()
    out = []
    for rb in (0, 1, rows // NA_ROWS - 1):
        qr, kr = NA_ROWS * rb + q_row, NA_ROWS * rb + k_row
        rs = jnp.clip(qr - NA_ROWS // 2, 0, rows - NA_ROWS)
        rvalid = (kr[None, :] >= rs[:, None]) & (kr[None, :] < rs[:, None] + NA_ROWS) & (kr[None, :] >= 0) & (kr[None, :] < rows)
        for cb in (0, 1, GRID_W // NA_COLS - 1):
            qc, kc = NA_COLS * cb + q_col, NA_COLS * cb + k_col
            cs = jnp.clip(qc - NA_COLS // 2, 0, GRID_W - NA_COLS)
            cvalid = (kc[None, :] >= cs[:, None]) & (kc[None, :] < cs[:, None] + NA_COLS) & (kc[None, :] >= 0) & (kc[None, :] < GRID_W)
            out.append(jnp.where(rvalid & cvalid, 0.0, NEG_INF).astype(_f32))
    return jnp.stack(out)


def _attn_c(h, rpb):
    b, l, _ = h.shape
    bias = _c_bias(rpb)
    masks = _c_masks(l // GRID_W)
    return pl.pallas_call(
        _attn_c_kernel,
        out_shape=jax.ShapeDtypeStruct((b, l, MIX_W), _bf16),
        grid=(b, MIX_HEADS),
        in_specs=[pl.BlockSpec((1, l, HEAD_DIM), lambda bi, h: (bi, 0, h)),
                  pl.BlockSpec((1, l, HEAD_DIM), lambda bi, h: (bi, 0, MIX_HEADS + h)),
                  pl.BlockSpec((1, l, HEAD_DIM), lambda bi, h: (bi, 0, 2 * MIX_HEADS + h)),
                  pl.BlockSpec((1, QBLK, C_NKEYS), lambda bi, h: (h, 0, 0)),
                  pl.BlockSpec((9, QBLK, C_NKEYS), lambda bi, h: (0, 0, 0))],
        out_specs=pl.BlockSpec((1, l, HEAD_DIM), lambda bi, h: (bi, 0, h)),
        scratch_shapes=[pltpu.VMEM((l + 2 * C_PAD, HEAD_DIM), _bf16),
                        pltpu.VMEM((l + 2 * C_PAD, 2 * HEAD_DIM), _bf16),
                        pltpu.VMEM((9, QBLK, C_NKEYS), _f32)],
        compiler_params=_cparams(2),
    )(h, h, h, bias, masks)


def _subblock_chunks(tm):
    grp = SUB * SUBS_PER_ROW
    pairs = []
    for g in range(tm // grp):
        for c8 in range(SUBS_PER_ROW):
            for i4 in range(SUB_R):
                pairs.append((g * grp + i4 * GRID_W + c8 * SUB_C, (g * SUBS_PER_ROW + c8) * SUB + i4 * SUB_C))
    return pairs


def _out_kernel(om_ref, gm_ref, mq_ref, kv_ref, x_ref, w_ref, g_ref, b_ref, o_ref, y_ref, xs_ref, *, subblock):
    tm = om_ref.shape[1]

    def silu_gate(c0, c1):
        gate = gm_ref[0, :, c0:c1].astype(_f32)
        return gate / (1.0 + jnp.exp(-gate))

    for c0 in range(0, MIX_W, MEM_W):
        y_ref[:, c0:c0 + MEM_W] = (om_ref[0, :, c0:c0 + MEM_W].astype(_f32) * silu_gate(c0, c0 + MEM_W)).astype(_bf16)
    ones = jnp.ones((kv_ref.shape[1], HEAD_DIM), _bf16)
    scores = [_qk(mq_ref[0, :, h * HEAD_DIM:(h + 1) * HEAD_DIM], kv_ref[0, :, h * HEAD_DIM:(h + 1) * HEAD_DIM]) * QSCALE
              for h in range(MEM_HEADS)]
    values = [jnp.concatenate([kv_ref[0, :, MEM_W + h * HEAD_DIM:MEM_W + (h + 1) * HEAD_DIM], ones], axis=1)
              for h in range(MEM_HEADS)]
    _, accs = _softmax_pv(scores, values)
    for h, acc in enumerate(accs):
        c0, c1 = MIX_W + h * HEAD_DIM, MIX_W + (h + 1) * HEAD_DIM
        y_ref[:, c0:c1] = (acc[:, :HEAD_DIM] / acc[:, HEAD_DIM:] * silu_gate(c0, c1)).astype(_bf16)
    branch = jnp.dot(y_ref[...], w_ref[...], preferred_element_type=_f32)
    if subblock:
        for nat, sub in _subblock_chunks(tm):
            xs_ref[sub:sub + SUB_C, :] = x_ref[0, nat:nat + SUB_C, :]
        xin = xs_ref[...]
    else:
        xin = x_ref[0]
    z = ALPHA * xin + branch
    mu = jnp.mean(z, axis=-1, keepdims=True)
    zc = z - mu
    var = jnp.mean(zc * zc, axis=-1, keepdims=True)
    out = zc * lax.rsqrt(var + LN_EPS) * g_ref[...] + b_ref[...]
    if subblock:
        xs_ref[...] = out
        for nat, sub in _subblock_chunks(tm):
            o_ref[0, nat:nat + SUB_C, :] = xs_ref[sub:sub + SUB_C, :]
    else:
        o_ref[0] = out


def _out(o_mix, hg, kv, x, w_out, ln_g, ln_b, *, subblock):
    b, l, d = x.shape
    tm = OUT_TM
    mlen = kv.shape[1]
    kern = functools.partial(_out_kernel, subblock=subblock)
    return pl.pallas_call(
        kern,
        out_shape=jax.ShapeDtypeStruct((b, l, d), _f32),
        grid=(b, l // tm),
        in_specs=[pl.BlockSpec((1, tm, MIX_W), lambda bi, i: (bi, i, 0)),
                  pl.BlockSpec((1, tm, D_INNER), lambda bi, i: (bi, i, 0)),
                  pl.BlockSpec((1, tm, MEM_W), lambda bi, i: (bi, i, D_INNER // MEM_W)),
                  pl.BlockSpec((1, mlen, 2 * MEM_W), lambda bi, i: (bi, 0, 0)),
                  pl.BlockSpec((1, tm, d), lambda bi, i: (bi, i, 0)),
                  pl.BlockSpec((D_INNER, d), lambda bi, i: (0, 0)),
                  pl.BlockSpec((1, d), lambda bi, i: (0, 0)),
                  pl.BlockSpec((1, d), lambda bi, i: (0, 0))],
        out_specs=pl.BlockSpec((1, tm, d), lambda bi, i: (bi, i, 0)),
        scratch_shapes=[pltpu.VMEM((tm, D_INNER), _bf16), pltpu.VMEM((tm, d), _f32)],
        compiler_params=_cparams(2),
    )(o_mix, hg, hg, kv, x, w_out, ln_g.reshape(1, d), ln_b.reshape(1, d))


def _split_cols(w, sizes):
    out, c = [], 0
    for s in sizes:
        out.append(w[:, c:c + s])
        c += s
    return out


def kernel(x, mem, w_in_a, sink_a, w_in_b, w_in_c, rpb_c, w_mkv, w_out, ln_g, ln_b):
    b, l, d = x.shape
    mlen = mem.shape[1]
    kv_all = _memkv(mem.reshape(b * mlen, d), w_mkv.astype(_bf16)).reshape(DEPTH, b, mlen, 2 * MEM_W)
    tables = _rope_tables(l)

    for i in range(DEPTH):
        kind, j = i % NUM_MIXERS, i // NUM_MIXERS
        if kind == 0:
            wq, wk, wv, wm, wg = _split_cols(w_in_a[j].astype(_bf16), (MIX_W, MEM_W, MEM_W, MEM_W, D_INNER))
            h = _proj(x, jnp.concatenate([wq, wk, wv], axis=1), tables,
                      n_q_cols=MIX_W, n_rope_cols=MIX_W + MEM_W)
            o_mix = _attn_a(h, sink_a[j])
        elif kind == 1:
            wq0, wq1, wq2, wk, wv, wm, wg = _split_cols(
                w_in_b[j].astype(_bf16), (MIX_W, MIX_W, MIX_W, MIX_W, MIX_W, MEM_W, D_INNER))
            h0 = _proj(x, jnp.concatenate([wq0, wk, wv], axis=1), tables,
                       n_q_cols=MIX_W, n_rope_cols=2 * MIX_W)
            hq1 = _proj(x, wq1, tables, perm="dilate", dil=B_DILATIONS[1], n_q_cols=MIX_W, n_rope_cols=MIX_W)
            hq2 = _proj(x, wq2, tables, perm="dilate", dil=B_DILATIONS[2], n_q_cols=MIX_W, n_rope_cols=MIX_W)
            o_mix = _attn_b(h0, hq1, hq2)
        else:
            wq, wk, wv, wm, wg = _split_cols(w_in_c[j].astype(_bf16), (MIX_W, MIX_W, MIX_W, MEM_W, D_INNER))
            h = _proj(x, jnp.concatenate([wq, wk, wv], axis=1), perm="subblock", n_q_cols=MIX_W)
            o_mix = _attn_c(h, rpb_c[j])
        hg = _proj(x, jnp.concatenate([wg, wm], axis=1), perm="subblock" if kind == 2 else "none")
        x = _out(o_mix, hg, kv_all[i], x, w_out[i].astype(_bf16), ln_g[i], ln_b[i], subblock=(kind == 2))
    return x
```

```python
import functools
import math

import jax
import jax.numpy as jnp
from jax import lax
from jax.experimental import pallas as pl
from jax.experimental.pallas import tpu as pltpu

D_MODEL = 1024
DEPTH = 4
NUM_MIXERS = 3
HEAD_DIM = 128
D_INNER = 2 * D_MODEL
MEM_HEADS = 4
MIX_HEADS = D_INNER // HEAD_DIM - MEM_HEADS
A_KV_HEADS = MIX_HEADS // 3
A_GROUP = MIX_HEADS // A_KV_HEADS
A_WINDOW = 128
B_DILATIONS = (1, 4, 16)
B_HALF_WINDOW = 64
NA_ROWS = 8
NA_COLS = 16
GRID_W = 64
ROPE_THETA = 500000.0
ROPE_DIMS = HEAD_DIM // 4
ROPE_HALF = ROPE_DIMS // 2
LN_EPS = 1e-5
ALPHA = (2 * DEPTH) ** 0.25
NEG_INF = -1e30
LOG2E = math.log2(math.e)
QSCALE = HEAD_DIM ** -0.5 * LOG2E

MIX_W = MIX_HEADS * HEAD_DIM
MEM_W = MEM_HEADS * HEAD_DIM

SUB_R = 4
SUB_C = 8
SUB = SUB_R * SUB_C
SUBS_PER_ROW = GRID_W // SUB_C
QBLK = 128

VMEM_LIMIT = 56 * 1024 * 1024
PROJ_TM = 512
PROJ_TN = 512
OUT_TM = 512

_f32 = jnp.float32
_bf16 = jnp.bfloat16


def _cparams(n_grid):
    return pltpu.CompilerParams(dimension_semantics=("arbitrary",) * n_grid,
                                vmem_limit_bytes=VMEM_LIMIT)


def _rope(blk, cos, sin):
    lane = lax.broadcasted_iota(jnp.int32, blk.shape, 1)
    up = pltpu.roll(blk, HEAD_DIM - ROPE_HALF, 1)
    down = pltpu.roll(blk, ROPE_HALF, 1)
    swapped = jnp.where(lane < ROPE_HALF, up, down)
    return blk * cos + swapped * sin


def _proj_kernel(*refs, perm, dil, n_q_tiles, n_rope_tiles):
    if n_rope_tiles:
        x_ref, w_ref, cos_ref, sin_ref, o_ref, xs_ref, tmp_ref = refs
    else:
        x_ref, w_ref, o_ref, xs_ref, tmp_ref = refs
    tm = xs_ref.shape[0]
    tn = PROJ_TN

    if perm == "subblock":
        grp = SUB * SUBS_PER_ROW
        for g in range(tm // grp):
            for c8 in range(SUBS_PER_ROW):
                parts = [x_ref[0, g * grp + i4 * GRID_W + c8 * SUB_C:
                               g * grp + i4 * GRID_W + (c8 + 1) * SUB_C, :] for i4 in range(SUB_R)]
                dst = (g * SUBS_PER_ROW + c8) * SUB
                xs_ref[dst:dst + SUB, :] = jnp.concatenate(parts, axis=0).astype(_bf16)
    else:
        xs_ref[...] = x_ref[0].astype(_bf16)

    def write(val, c0):
        c1 = c0 + HEAD_DIM
        if perm == "dilate":
            n = tm // dil
            tmp_ref[...] = val
            for r in range(dil):
                o_ref[0, r, :, c0:c1] = tmp_ref[pl.ds(r, n, stride=dil), :].astype(o_ref.dtype)
        else:
            o_ref[0, :, c0:c1] = val.astype(o_ref.dtype)

    for c in range(w_ref.shape[1] // tn):
        acc = jnp.dot(xs_ref[...], w_ref[:, c * tn:(c + 1) * tn], preferred_element_type=_f32)
        for hd in range(tn // HEAD_DIM):
            val = acc[:, hd * HEAD_DIM:(hd + 1) * HEAD_DIM]
            if c < n_rope_tiles:
                t = 0 if c < n_q_tiles else 1
                val = _rope(val, cos_ref[t], sin_ref[t])
            elif c < n_q_tiles:
                val = val * QSCALE
            write(val, c * tn + hd * HEAD_DIM)


def _proj(x, w, tables=None, *, perm="none", dil=1, n_q_cols=0, n_rope_cols=0):
    b, l, d = x.shape
    n = w.shape[1]
    tm, tn = PROJ_TM, PROJ_TN
    assert l % tm == 0 and n % tn == 0 and n_rope_cols % tn == 0 and n_q_cols % tn == 0
    tiles = l // tm
    n_q_tiles, n_rope_tiles = n_q_cols // tn, n_rope_cols // tn
    if perm == "dilate":
        out_shape = jax.ShapeDtypeStruct((b, dil, l // dil, n), _bf16)
        out_spec = pl.BlockSpec((1, dil, tm // dil, n), lambda i: (i // tiles, 0, i % tiles, 0))
    else:
        out_shape = jax.ShapeDtypeStruct((b, l, n), _bf16)
        out_spec = pl.BlockSpec((1, tm, n), lambda i: (i // tiles, i % tiles, 0))
    in_specs = [pl.BlockSpec((1, tm, d), lambda i: (i // tiles, i % tiles, 0)),
                pl.BlockSpec((d, n), lambda i: (0, 0), pipeline_mode=pl.Buffered(1))]
    args = [x, w]
    if n_rope_tiles:
        tab_spec = pl.BlockSpec((2, tm, HEAD_DIM), lambda i: (0, i % tiles, 0))
        in_specs += [tab_spec, tab_spec]
        args += list(tables)
    kern = functools.partial(_proj_kernel, perm=perm, dil=dil, n_q_tiles=n_q_tiles, n_rope_tiles=n_rope_tiles)
    return pl.pallas_call(
        kern,
        out_shape=out_shape,
        grid=(b * tiles,),
        in_specs=in_specs,
        out_specs=out_spec,
        scratch_shapes=[pltpu.VMEM((tm, d), _bf16), pltpu.VMEM((tm, HEAD_DIM), _f32)],
        compiler_params=_cparams(1),
    )(*args)


def _rope_tables(l):
    inv = ROPE_THETA ** (-jnp.arange(ROPE_HALF, dtype=_f32) / ROPE_HALF)
    ang = jnp.arange(l).astype(_f32)[:, None] * inv[None, :]
    cos, sin = jnp.cos(ang), jnp.sin(ang)
    pad = HEAD_DIM - ROPE_DIMS
    cos_t = jnp.concatenate([cos, cos, jnp.ones((l, pad), _f32)], axis=1)
    sin_t = jnp.concatenate([-sin, sin, jnp.zeros((l, pad), _f32)], axis=1)
    return jnp.stack([cos_t * QSCALE, cos_t]), jnp.stack([sin_t * QSCALE, sin_t])


def _memkv_kernel(m_ref, w_ref, o_ref):
    o_ref[0] = jnp.dot(m_ref[...].astype(_bf16), w_ref[0], preferred_element_type=_f32).astype(o_ref.dtype)


def _memkv(mem2d, w_mkv):
    rows, d = mem2d.shape
    depth, _, n = w_mkv.shape
    tm = min(rows, PROJ_TM)
    assert rows % tm == 0
    return pl.pallas_call(
        _memkv_kernel,
        out_shape=jax.ShapeDtypeStruct((depth, rows, n), _bf16),
        grid=(depth, rows // tm),
        in_specs=[pl.BlockSpec((tm, d), lambda li, i: (i, 0)),
                  pl.BlockSpec((1, d, n), lambda li, i: (li, 0, 0))],
        out_specs=pl.BlockSpec((1, tm, n), lambda li, i: (li, i, 0)),
        compiler_params=_cparams(2),
    )(mem2d, w_mkv)


def _qk(q, k):
    return lax.dot_general(q, k, (((1,), (1,)), ((), ())), preferred_element_type=_f32)


def _band_masks(mask_ref, nk, half, deltas):
    rel = (lax.broadcasted_iota(jnp.int32, (QBLK, nk), 1) - lax.broadcasted_iota(jnp.int32, (QBLK, nk), 0))
    for v, delta in enumerate(deltas):
        mask_ref[v] = jnp.where(jnp.abs(rel + delta) <= half, 0.0, NEG_INF).astype(_f32)


def _edge_variant(i, n):
    return jnp.where(i == 0, 0, jnp.where(i == n - 1, 2, 1))


def _softmax_pv(scores, values, floors=None):
    def rowmax(s):
        m = s[:, 0:HEAD_DIM]
        for c in range(1, s.shape[1] // HEAD_DIM):
            m = jnp.maximum(m, s[:, c * HEAD_DIM:(c + 1) * HEAD_DIM])
        return jnp.max(m, axis=-1, keepdims=True)

    ms = [rowmax(s) for s in scores]
    if floors is not None:
        ms = [jnp.maximum(m, f) for m, f in zip(ms, floors)]
    ps = [jnp.exp2(s - m).astype(_bf16) for s, m in zip(scores, ms)]
    accs = [jnp.dot(p, v, preferred_element_type=_f32) for p, v in zip(ps, values)]
    return ms, accs


A_UNROLL = 2


def _attn_a_kernel(q_ref, k_ref, v_ref, sink_ref, o_ref, v1_ref, mask_ref):
    l = k_ref.shape[1]
    nk = 3 * A_WINDOW
    nblk = l // QBLK
    v1_ref[:, 0:HEAD_DIM] = v_ref[0]
    v1_ref[:, HEAD_DIM:2 * HEAD_DIM] = jnp.ones((l, HEAD_DIM), _bf16)
    rel = (lax.broadcasted_iota(jnp.int32, (QBLK, nk), 1) - lax.broadcasted_iota(jnp.int32, (QBLK, nk), 0))
    for v, delta in enumerate((0, -A_WINDOW, -2 * A_WINDOW)):
        blk = jnp.where(jnp.abs(rel + delta) <= A_WINDOW, 0.0, NEG_INF).astype(_f32)
        for a in range(A_GROUP):
            mask_ref[v, a * QBLK:(a + 1) * QBLK, :] = blk
    sink_col = jnp.concatenate(
        [jnp.broadcast_to(sink_ref[0, a:a + 1, 0:1] * LOG2E, (QBLK, 1)) for a in range(A_GROUP)], axis=0)

    def body(t, carry):
        scores, values, dests = [], [], []
        for u in range(A_UNROLL):
            i = t * A_UNROLL + u
            q0 = pl.multiple_of(i * QBLK, QBLK)
            ks = pl.multiple_of(jnp.clip(q0 - A_WINDOW, 0, l - nk), QBLK)
            q = jnp.concatenate([q_ref[0, pl.ds(q0, QBLK), a * HEAD_DIM:(a + 1) * HEAD_DIM]
                                 for a in range(A_GROUP)], axis=0)
            scores.append(_qk(q, k_ref[0, pl.ds(ks, nk), :]) + mask_ref[_edge_variant(i, nblk)])
            values.append(v1_ref[pl.ds(ks, nk), :])
            dests.append(q0)
        ms, accs = _softmax_pv(scores, values, [sink_col] * A_UNROLL)
        for m, acc, q0 in zip(ms, accs, dests):
            z = acc[:, HEAD_DIM:] + jnp.exp2(sink_col - m)
            o = (acc[:, :HEAD_DIM] / z).astype(o_ref.dtype)
            for a in range(A_GROUP):
                o_ref[0, pl.ds(q0, QBLK), a * HEAD_DIM:(a + 1) * HEAD_DIM] = o[a * QBLK:(a + 1) * QBLK, :]
        return carry

    lax.fori_loop(0, nblk // A_UNROLL, body, 0)


def _attn_a(h, sink):
    b, l, _ = h.shape
    gw = A_GROUP * HEAD_DIM
    sink_b = jnp.broadcast_to(sink.astype(_f32).reshape(A_KV_HEADS, A_GROUP, 1), (A_KV_HEADS, A_GROUP, HEAD_DIM))
    sink_b = jnp.pad(sink_b, ((0, 0), (0, 8 - A_GROUP), (0, 0)))
    return pl.pallas_call(
        _attn_a_kernel,
        out_shape=jax.ShapeDtypeStruct((b, l, MIX_W), _bf16),
        grid=(b, A_KV_HEADS),
        in_specs=[pl.BlockSpec((1, l, gw), lambda bi, g: (bi, 0, g)),
                  pl.BlockSpec((1, l, HEAD_DIM), lambda bi, g: (bi, 0, MIX_HEADS + g)),
                  pl.BlockSpec((1, l, HEAD_DIM), lambda bi, g: (bi, 0, MIX_HEADS + A_KV_HEADS + g)),
                  pl.BlockSpec((1, 8, HEAD_DIM), lambda bi, g: (g, 0, 0))],
        out_specs=pl.BlockSpec((1, l, gw), lambda bi, g: (bi, 0, g)),
        scratch_shapes=[pltpu.VMEM((l, 2 * HEAD_DIM), _bf16),
                        pltpu.VMEM((3, A_GROUP * QBLK, 3 * A_WINDOW), _f32)],
        compiler_params=_cparams(2),
    )(h, h, h, sink_b)


B_UNROLL = 8


def _attn_b_kernel(q0_ref, k_ref, v_ref, q1_ref, q2_ref, o_ref,
                   kd_ref, v1_ref, tmp_ref, tmp2_ref, og_ref, lse_ref, mask_ref):
    l = o_ref.shape[1]
    nk = 4 * B_HALF_WINDOW
    q_refs = (q0_ref, q1_ref, q2_ref)
    _band_masks(mask_ref, nk, B_HALF_WINDOW, (0, -B_HALF_WINDOW, -2 * B_HALF_WINDOW))

    v1_ref[0, :, 0:HEAD_DIM] = v_ref[0]
    for g in range(3):
        v1_ref[g, :, HEAD_DIM:2 * HEAD_DIM] = jnp.ones((l, HEAD_DIM), _bf16)
    d1 = B_DILATIONS[1]
    assert B_DILATIONS == (1, d1, d1 * d1)
    n1, n2 = l // d1, l // (d1 * d1)
    for src, is_key in ((k_ref, True), (v_ref, False)):
        def put(g, row0, n, rows):
            if is_key:
                kd_ref[g - 1, row0:row0 + n, :] = rows.astype(_bf16)
            else:
                v1_ref[g, row0:row0 + n, 0:HEAD_DIM] = rows.astype(_bf16)

        tmp_ref[...] = src[0].astype(_f32)
        for r in range(d1):
            rows = tmp_ref[pl.ds(r, n1, stride=d1), :]
            tmp2_ref[r * n1:(r + 1) * n1, :] = rows
            put(1, r * n1, n1, rows)
        for r in range(d1 * d1):
            put(2, r * n2, n2, tmp2_ref[pl.ds((r % d1) * n1 + r // d1, n2, stride=d1), :])

    for g, dil in enumerate(B_DILATIONS):
        n = l // dil
        blocks = n // QBLK

        def body(t, carry, g=g, dil=dil, n=n, blocks=blocks):
            scores, values, dests = [], [], []
            for u in range(B_UNROLL):
                bt = t * B_UNROLL + u
                r = bt // blocks
                i = bt % blocks
                q0 = pl.multiple_of(i * QBLK, QBLK)
                ks = pl.multiple_of(jnp.clip(q0 - B_HALF_WINDOW, 0, n - nk), B_HALF_WINDOW)
                base = pl.multiple_of(r * n + ks, B_HALF_WINDOW)
                if g == 0:
                    q = q0_ref[0, pl.ds(q0, QBLK), :]
                    kw = k_ref[0, pl.ds(ks, nk), :]
                else:
                    q = q_refs[g][0, r, pl.ds(q0, QBLK), :]
                    kw = kd_ref[g - 1, pl.ds(base, nk), :]
                scores.append(_qk(q, kw) + mask_ref[_edge_variant(i, blocks)])
                values.append(v1_ref[g, pl.ds(base, nk), :])
                dests.append((q0, r))
            ms, accs = _softmax_pv(scores, values)
            for m, acc, (q0, r) in zip(ms, accs, dests):
                z = acc[:, HEAD_DIM:]
                o = acc[:, :HEAD_DIM] / z
                lse = m + jnp.log2(z)
                if dil == 1:
                    og_ref[g, pl.ds(q0, QBLK), :] = o
                    lse_ref[g, pl.ds(q0, QBLK), :] = lse
                else:
                    og_ref[g, pl.ds(q0 * dil + r, QBLK, stride=dil), :] = o
                    lse_ref[g, pl.ds(q0 * dil + r, QBLK, stride=dil), :] = lse
            return carry

        lax.fori_loop(0, dil * blocks // B_UNROLL, body, 0)

    def combine(i, carry):
        r0 = pl.multiple_of(i * QBLK, QBLK)
        ls = [lse_ref[g, pl.ds(r0, QBLK), :] for g in range(3)]
        m = jnp.maximum(jnp.maximum(ls[0], ls[1]), ls[2])
        ws = [jnp.exp2(x - m) for x in ls]
        tot = ws[0] + ws[1] + ws[2]
        acc = ws[0] * og_ref[0, pl.ds(r0, QBLK), :]
        acc += ws[1] * og_ref[1, pl.ds(r0, QBLK), :]
        acc += ws[2] * og_ref[2, pl.ds(r0, QBLK), :]
        o_ref[0, pl.ds(r0, QBLK), :] = (acc / tot).astype(o_ref.dtype)
        return carry

    lax.fori_loop(0, l // QBLK, combine, 0)


def _attn_b(h0, hq1, hq2):
    b, l, _ = h0.shape
    in_specs = [pl.BlockSpec((1, l, HEAD_DIM), lambda bi, h: (bi, 0, h)),
                pl.BlockSpec((1, l, HEAD_DIM), lambda bi, h: (bi, 0, MIX_HEADS + h)),
                pl.BlockSpec((1, l, HEAD_DIM), lambda bi, h: (bi, 0, 2 * MIX_HEADS + h))]
    for dil in B_DILATIONS[1:]:
        in_specs.append(pl.BlockSpec((1, dil, l // dil, HEAD_DIM), lambda bi, h: (bi, 0, 0, h)))
    return pl.pallas_call(
        _attn_b_kernel,
        out_shape=jax.ShapeDtypeStruct((b, l, MIX_W), _bf16),
        grid=(b, MIX_HEADS),
        in_specs=in_specs,
        out_specs=pl.BlockSpec((1, l, HEAD_DIM), lambda bi, h: (bi, 0, h)),
        scratch_shapes=[pltpu.VMEM((2, l, HEAD_DIM), _bf16),
                        pltpu.VMEM((3, l, 2 * HEAD_DIM), _bf16),
                        pltpu.VMEM((l, HEAD_DIM), _f32),
                        pltpu.VMEM((l, HEAD_DIM), _f32),
                        pltpu.VMEM((3, l, HEAD_DIM), _f32),
                        pltpu.VMEM((3, l, HEAD_DIM), _f32),
                        pltpu.VMEM((3, QBLK, 4 * B_HALF_WINDOW), _f32)],
        compiler_params=_cparams(2),
    )(h0, h0, h0, hq1, hq2)


C_KSEG = 4 * SUB
C_NSEG = 4
C_PAD = SUB
C_NKEYS = C_NSEG * C_KSEG


def _attn_c_kernel(q_ref, k_ref, v_ref, bias_ref, mask_ref, o_ref, kp_ref, vp_ref, bm_ref):
    l = k_ref.shape[1]
    n_rb = l // GRID_W // NA_ROWS
    n_cb = GRID_W // NA_COLS
    n_r4 = l // GRID_W // SUB_R
    kp_ref[0:C_PAD, :] = jnp.zeros((C_PAD, HEAD_DIM), _bf16)
    kp_ref[C_PAD + l:C_PAD + l + C_PAD, :] = jnp.zeros((C_PAD, HEAD_DIM), _bf16)
    kp_ref[C_PAD:C_PAD + l, :] = k_ref[0]
    vp_ref[0:C_PAD, 0:HEAD_DIM] = jnp.zeros((C_PAD, HEAD_DIM), _bf16)
    vp_ref[C_PAD + l:C_PAD + l + C_PAD, 0:HEAD_DIM] = jnp.zeros((C_PAD, HEAD_DIM), _bf16)
    vp_ref[C_PAD:C_PAD + l, 0:HEAD_DIM] = v_ref[0]
    vp_ref[:, HEAD_DIM:2 * HEAD_DIM] = jnp.ones((l + 2 * C_PAD, HEAD_DIM), _bf16)
    for v in range(9):
        bm_ref[v] = bias_ref[0] + mask_ref[v]

    def body(rb, carry):
        rv = _edge_variant(rb, n_rb)
        scores, values, dests = [], [], []
        for cb in range(n_cb):
            cv = 0 if cb == 0 else (2 if cb == n_cb - 1 else 1)
            q_starts = [pl.multiple_of(((2 * rb + a) * SUBS_PER_ROW + 2 * cb) * SUB, 2 * SUB) for a in range(2)]
            q = jnp.concatenate([q_ref[0, pl.ds(qs, 2 * SUB), :] for qs in q_starts], axis=0)
            kparts, vparts = [], []
            for ar in range(C_NSEG):
                r4 = jnp.clip(2 * rb - 1 + ar, 0, n_r4 - 1)
                st = pl.multiple_of(C_PAD + (r4 * SUBS_PER_ROW + 2 * cb - 1) * SUB, SUB)
                kparts.append(kp_ref[pl.ds(st, C_KSEG), :])
                vparts.append(vp_ref[pl.ds(st, C_KSEG), :])
            scores.append(_qk(q, jnp.concatenate(kparts, axis=0)) + bm_ref[rv * 3 + cv])
            values.append(jnp.concatenate(vparts, axis=0))
            dests.append(q_starts)
        ms, accs = _softmax_pv(scores, values)
        for acc, q_starts in zip(accs, dests):
            o = (acc[:, :HEAD_DIM] / acc[:, HEAD_DIM:]).astype(o_ref.dtype)
            for a in range(2):
                o_ref[0, pl.ds(q_starts[a], 2 * SUB), :] = o[a * 2 * SUB:(a + 1) * 2 * SUB, :]
        return carry

    lax.fori_loop(0, n_rb, body, 0)


def _c_geometry():
    ql = jnp.arange(QBLK)
    qa, qc, qi, qj = ql // 64, (ql // 32) % 2, (ql // 8) % 4, ql % 8
    q_row = SUB_R * qa + qi
    q_col = SUB_C * qc + qj
    kl = jnp.arange(C_NKEYS)
    ka, kc, ki, kj = kl // C_KSEG, (kl // SUB) % 4, (kl // 8) % 4, kl % 8
    k_row = SUB_R * (ka - 1) + ki
    k_col = SUB_C * (kc - 1) + kj
    return q_row, q_col, k_row, k_col


def _c_bias(rpb):
    q_row, q_col, k_row, k_col = _c_geometry()
    nr, nc = 2 * NA_ROWS - 1, 2 * NA_COLS - 1
    dr = jnp.clip(k_row[None, :] - q_row[:, None] + NA_ROWS - 1, 0, nr - 1)
    dc = jnp.clip(k_col[None, :] - q_col[:, None] + NA_COLS - 1, 0, nc - 1)
    oh_r = (dr[:, :, None] == jnp.arange(nr)).astype(_f32)
    oh_c = (dc[:, :, None] == jnp.arange(nc)).astype(_f32)
    rows = jnp.einsum("hrc,qkr->hqkc", rpb.astype(_f32), oh_r, precision=lax.Precision.HIGHEST)
    return jnp.sum(rows * oh_c[None], axis=-1) * LOG2E


def _c_masks(rows):
    q_row, q_col, k_row, k_col = _c_geometry()
    out = []
    for rb in (0, 1, rows // NA_ROWS - 1):
        qr, kr = NA_ROWS * rb + q_row, NA_ROWS * rb + k_row
        rs = jnp.clip(qr - NA_ROWS // 2, 0, rows - NA_ROWS)
        rvalid = (kr[None, :] >= rs[:, None]) & (kr[None, :] < rs[:, None] + NA_ROWS) & (kr[None, :] >= 0) & (kr[None, :] < rows)
        for cb in (0, 1, GRID_W // NA_COLS - 1):
            qc, kc = NA_COLS * cb + q_col, NA_COLS * cb + k_col
            cs = jnp.clip(qc - NA_COLS // 2, 0, GRID_W - NA_COLS)
            cvalid = (kc[None, :] >= cs[:, None]) & (kc[None, :] < cs[:, None] + NA_COLS) & (kc[None, :] >= 0) & (kc[None, :] < GRID_W)
            out.append(jnp.where(rvalid & cvalid, 0.0, NEG_INF).astype(_f32))
    return jnp.stack(out)


def _attn_c(h, rpb):
    b, l, _ = h.shape
    bias = _c_bias(rpb)
    masks = _c_masks(l // GRID_W)
    return pl.pallas_call(
        _attn_c_kernel,
        out_shape=jax.ShapeDtypeStruct((b, l, MIX_W), _bf16),
        grid=(b, MIX_HEADS),
        in_specs=[pl.BlockSpec((1, l, HEAD_DIM), lambda bi, h: (bi, 0, h)),
                  pl.BlockSpec((1, l, HEAD_DIM), lambda bi, h: (bi, 0, MIX_HEADS + h)),
                  pl.BlockSpec((1, l, HEAD_DIM), lambda bi, h: (bi, 0, 2 * MIX_HEADS + h)),
                  pl.BlockSpec((1, QBLK, C_NKEYS), lambda bi, h: (h, 0, 0)),
                  pl.BlockSpec((9, QBLK, C_NKEYS), lambda bi, h: (0, 0, 0))],
        out_specs=pl.BlockSpec((1, l, HEAD_DIM), lambda bi, h: (bi, 0, h)),
        scratch_shapes=[pltpu.VMEM((l + 2 * C_PAD, HEAD_DIM), _bf16),
                        pltpu.VMEM((l + 2 * C_PAD, 2 * HEAD_DIM), _bf16),
                        pltpu.VMEM((9, QBLK, C_NKEYS), _f32)],
        compiler_params=_cparams(2),
    )(h, h, h, bias, masks)


def _subblock_chunks(tm):
    grp = SUB * SUBS_PER_ROW
    pairs = []
    for g in range(tm // grp):
        for c8 in range(SUBS_PER_ROW):
            for i4 in range(SUB_R):
                pairs.append((g * grp + i4 * GRID_W + c8 * SUB_C, (g * SUBS_PER_ROW + c8) * SUB + i4 * SUB_C))
    return pairs


def _out_kernel(om_ref, glo_ref, ghi_ref, mq_ref, kv_ref, x_ref, w_ref, g_ref, b_ref, o_ref, xs_ref, *, subblock):
    tm = om_ref.shape[1]
    half = D_INNER // 2

    def silu_gate(c0, c1):
        ref, off = (glo_ref, 0) if c0 < half else (ghi_ref, half)
        gate = ref[0, :, c0 - off:c1 - off].astype(_f32)
        return gate / (1.0 + jnp.exp(-gate))

    ones = jnp.ones((kv_ref.shape[1], HEAD_DIM), _bf16)
    scores = [_qk(mq_ref[0, :, h * HEAD_DIM:(h + 1) * HEAD_DIM], kv_ref[0, :, h * HEAD_DIM:(h + 1) * HEAD_DIM]) * QSCALE
              for h in range(MEM_HEADS)]
    values = [jnp.concatenate([kv_ref[0, :, MEM_W + h * HEAD_DIM:MEM_W + (h + 1) * HEAD_DIM], ones], axis=1)
              for h in range(MEM_HEADS)]
    _, accs = _softmax_pv(scores, values)
    branch = None
    for c0 in range(0, D_INNER, MEM_W):
        if c0 < MIX_W:
            y = om_ref[0, :, c0:c0 + MEM_W].astype(_f32) * silu_gate(c0, c0 + MEM_W)
        else:
            y = jnp.concatenate([acc[:, :HEAD_DIM] / acc[:, HEAD_DIM:] for acc in accs], axis=1) * silu_gate(c0, c0 + MEM_W)
        part = jnp.dot(y.astype(_bf16), w_ref[c0:c0 + MEM_W, :], preferred_element_type=_f32)
        branch = part if branch is None else branch + part
    if subblock:
        for nat, sub in _subblock_chunks(tm):
            xs_ref[sub:sub + SUB_C, :] = x_ref[0, nat:nat + SUB_C, :]
        xin = xs_ref[...]
    else:
        xin = x_ref[0]
    z = ALPHA * xin + branch
    mu = jnp.mean(z, axis=-1, keepdims=True)
    zc = z - mu
    var = jnp.mean(zc * zc, axis=-1, keepdims=True)
    out = zc * lax.rsqrt(var + LN_EPS) * g_ref[...] + b_ref[...]
    if subblock:
        xs_ref[...] = out
        for nat, sub in _subblock_chunks(tm):
            o_ref[0, nat:nat + SUB_C, :] = xs_ref[sub:sub + SUB_C, :]
    else:
        o_ref[0] = out


def _out(o_mix, h, mq_col, kv, x, w_out, ln_g, ln_b, *, subblock):
    b, l, d = x.shape
    tm = OUT_TM
    mlen = kv.shape[1]
    half = D_INNER // 2
    gate_col = mq_col + MEM_W
    assert mq_col % MEM_W == 0 and gate_col % half == 0
    kern = functools.partial(_out_kernel, subblock=subblock)
    return pl.pallas_call(
        kern,
        out_shape=jax.ShapeDtypeStruct((b, l, d), _f32),
        grid=(b, l // tm),
        in_specs=[pl.BlockSpec((1, tm, MIX_W), lambda bi, i: (bi, i, 0)),
                  pl.BlockSpec((1, tm, half), lambda bi, i: (bi, i, gate_col // half)),
                  pl.BlockSpec((1, tm, half), lambda bi, i: (bi, i, gate_col // half + 1)),
                  pl.BlockSpec((1, tm, MEM_W), lambda bi, i: (bi, i, mq_col // MEM_W)),
                  pl.BlockSpec((1, mlen, 2 * MEM_W), lambda bi, i: (bi, 0, 0)),
                  pl.BlockSpec((1, tm, d), lambda bi, i: (bi, i, 0)),
                  pl.BlockSpec((D_INNER, d), lambda bi, i: (0, 0)),
                  pl.BlockSpec((1, d), lambda bi, i: (0, 0)),
                  pl.BlockSpec((1, d), lambda bi, i: (0, 0))],
        out_specs=pl.BlockSpec((1, tm, d), lambda bi, i: (bi, i, 0)),
        scratch_shapes=[pltpu.VMEM((tm, d), _f32)],
        compiler_params=_cparams(2),
    )(o_mix, h, h, h, kv, x, w_out, ln_g.reshape(1, d), ln_b.reshape(1, d))


def _split_cols(w, sizes):
    out, c = [], 0
    for s in sizes:
        out.append(w[:, c:c + s])
        c += s
    return out


def kernel(x, mem, w_in_a, sink_a, w_in_b, w_in_c, rpb_c, w_mkv, w_out, ln_g, ln_b):
    b, l, d = x.shape
    mlen = mem.shape[1]
    kv_all = _memkv(mem.reshape(b * mlen, d), w_mkv.astype(_bf16)).reshape(DEPTH, b, mlen, 2 * MEM_W)
    tables = _rope_tables(l)

    for i in range(DEPTH):
        kind, j = i % NUM_MIXERS, i // NUM_MIXERS
        if kind == 0:
            h = _proj(x, w_in_a[j].astype(_bf16), tables, n_q_cols=MIX_W, n_rope_cols=MIX_W + MEM_W)
            o_mix = _attn_a(h, sink_a[j])
            mq_col = MIX_W + 2 * MEM_W
        elif kind == 1:
            wq0, wq1, wq2, wrest = _split_cols(w_in_b[j].astype(_bf16), (MIX_W, MIX_W, MIX_W, 2 * MIX_W + MEM_W + D_INNER))
            h = _proj(x, jnp.concatenate([wq0, wrest], axis=1), tables, n_q_cols=MIX_W, n_rope_cols=2 * MIX_W)
            hq1 = _proj(x, wq1, tables, perm="dilate", dil=B_DILATIONS[1], n_q_cols=MIX_W, n_rope_cols=MIX_W)
            hq2 = _proj(x, wq2, tables, perm="dilate", dil=B_DILATIONS[2], n_q_cols=MIX_W, n_rope_cols=MIX_W)
            o_mix = _attn_b(h, hq1, hq2)
            mq_col = 3 * MIX_W
        else:
            h = _proj(x, w_in_c[j].astype(_bf16), perm="subblock", n_q_cols=MIX_W)
            o_mix = _attn_c(h, rpb_c[j])
            mq_col = 3 * MIX_W
        x = _out(o_mix, h, mq_col, kv_all[i], x, w_out[i].astype(_bf16), ln_g[i], ln_b[i], subblock=(kind == 2))
    return x
```

```python
import functools
import math

import jax
import jax.numpy as jnp
from jax import lax
from jax.experimental import pallas as pl
from jax.experimental.pallas import tpu as pltpu

D_MODEL = 1024
DEPTH = 4
NUM_MIXERS = 3
HEAD_DIM = 128
D_INNER = 2 * D_MODEL
MEM_HEADS = 4
MIX_HEADS = D_INNER // HEAD_DIM - MEM_HEADS
A_KV_HEADS = MIX_HEADS // 3
A_GROUP = MIX_HEADS // A_KV_HEADS
A_WINDOW = 128
B_DILATIONS = (1, 4, 16)
B_HALF_WINDOW = 64
NA_ROWS = 8
NA_COLS = 16
GRID_W = 64
ROPE_THETA = 500000.0
ROPE_DIMS = HEAD_DIM // 4
ROPE_HALF = ROPE_DIMS // 2
LN_EPS = 1e-5
ALPHA = (2 * DEPTH) ** 0.25
NEG_INF = -1e30
LOG2E = math.log2(math.e)
QSCALE = HEAD_DIM ** -0.5 * LOG2E

MIX_W = MIX_HEADS * HEAD_DIM
MEM_W = MEM_HEADS * HEAD_DIM

SUB_R = 4
SUB_C = 8
SUB = SUB_R * SUB_C
SUBS_PER_ROW = GRID_W // SUB_C
QBLK = 128

VMEM_LIMIT = 56 * 1024 * 1024
PROJ_TM = 512
PROJ_TN = 512
OUT_TM = 512

_f32 = jnp.float32
_bf16 = jnp.bfloat16


def _cparams(n_grid):
    return pltpu.CompilerParams(dimension_semantics=("arbitrary",) * n_grid,
                                vmem_limit_bytes=VMEM_LIMIT)


def _rope(blk, cos, sin):
    lane = lax.broadcasted_iota(jnp.int32, blk.shape, 1)
    up = pltpu.roll(blk, HEAD_DIM - ROPE_HALF, 1)
    down = pltpu.roll(blk, ROPE_HALF, 1)
    swapped = jnp.where(lane < ROPE_HALF, up, down)
    return blk * cos + swapped * sin


def _proj_kernel(*refs, perm, dil, n_q_tiles, n_rope_tiles):
    if n_rope_tiles:
        x_ref, w_ref, cos_ref, sin_ref, o_ref, xs_ref, tmp_ref = refs
    else:
        x_ref, w_ref, o_ref, xs_ref, tmp_ref = refs
    tm = xs_ref.shape[0]
    tn = PROJ_TN

    if perm == "subblock":
        grp = SUB * SUBS_PER_ROW
        for g in range(tm // grp):
            for c8 in range(SUBS_PER_ROW):
                parts = [x_ref[0, g * grp + i4 * GRID_W + c8 * SUB_C:
                               g * grp + i4 * GRID_W + (c8 + 1) * SUB_C, :] for i4 in range(SUB_R)]
                dst = (g * SUBS_PER_ROW + c8) * SUB
                xs_ref[dst:dst + SUB, :] = jnp.concatenate(parts, axis=0).astype(_bf16)
    else:
        xs_ref[...] = x_ref[0].astype(_bf16)

    def write(val, c0):
        c1 = c0 + HEAD_DIM
        if perm == "dilate":
            n = tm // dil
            tmp_ref[0] = val
            if dil == B_DILATIONS[2]:
                d1 = B_DILATIONS[1]
                n1 = tm // d1
                for r in range(d1):
                    tmp_ref[1, r * n1:(r + 1) * n1, :] = tmp_ref[0, pl.ds(r, n1, stride=d1), :]
                for r in range(dil):
                    rows = tmp_ref[1, pl.ds((r % d1) * n1 + r // d1, n, stride=d1), :]
                    o_ref[0, r, :, c0:c1] = rows.astype(o_ref.dtype)
            else:
                for r in range(dil):
                    o_ref[0, r, :, c0:c1] = tmp_ref[0, pl.ds(r, n, stride=dil), :].astype(o_ref.dtype)
        else:
            o_ref[0, :, c0:c1] = val.astype(o_ref.dtype)

    for c in range(w_ref.shape[1] // tn):
        acc = jnp.dot(xs_ref[...], w_ref[:, c * tn:(c + 1) * tn], preferred_element_type=_f32)
        for hd in range(tn // HEAD_DIM):
            val = acc[:, hd * HEAD_DIM:(hd + 1) * HEAD_DIM]
            if c < n_rope_tiles:
                t = 0 if c < n_q_tiles else 1
                val = _rope(val, cos_ref[t], sin_ref[t])
            elif c < n_q_tiles:
                val = val * QSCALE
            write(val, c * tn + hd * HEAD_DIM)


def _proj(x, w, tables=None, *, perm="none", dil=1, n_q_cols=0, n_rope_cols=0):
    b, l, d = x.shape
    n = w.shape[1]
    tm, tn = PROJ_TM, PROJ_TN
    assert l % tm == 0 and n % tn == 0 and n_rope_cols % tn == 0 and n_q_cols % tn == 0
    tiles = l // tm
    n_q_tiles, n_rope_tiles = n_q_cols // tn, n_rope_cols // tn
    if perm == "dilate":
        out_shape = jax.ShapeDtypeStruct((b, dil, l // dil, n), _bf16)
        out_spec = pl.BlockSpec((1, dil, tm // dil, n), lambda i: (i // tiles, 0, i % tiles, 0))
    else:
        out_shape = jax.ShapeDtypeStruct((b, l, n), _bf16)
        out_spec = pl.BlockSpec((1, tm, n), lambda i: (i // tiles, i % tiles, 0))
    in_specs = [pl.BlockSpec((1, tm, d), lambda i: (i // tiles, i % tiles, 0)),
                pl.BlockSpec((d, n), lambda i: (0, 0), pipeline_mode=pl.Buffered(1))]
    args = [x, w]
    if n_rope_tiles:
        tab_spec = pl.BlockSpec((2, tm, HEAD_DIM), lambda i: (0, i % tiles, 0))
        in_specs += [tab_spec, tab_spec]
        args += list(tables)
    kern = functools.partial(_proj_kernel, perm=perm, dil=dil, n_q_tiles=n_q_tiles, n_rope_tiles=n_rope_tiles)
    return pl.pallas_call(
        kern,
        out_shape=out_shape,
        grid=(b * tiles,),
        in_specs=in_specs,
        out_specs=out_spec,
        scratch_shapes=[pltpu.VMEM((tm, d), _bf16), pltpu.VMEM((2, tm, HEAD_DIM), _f32)],
        compiler_params=_cparams(1),
    )(*args)


def _rope_tables(l):
    inv = ROPE_THETA ** (-jnp.arange(ROPE_HALF, dtype=_f32) / ROPE_HALF)
    ang = jnp.arange(l).astype(_f32)[:, None] * inv[None, :]
    cos, sin = jnp.cos(ang), jnp.sin(ang)
    pad = HEAD_DIM - ROPE_DIMS
    cos_t = jnp.concatenate([cos, cos, jnp.ones((l, pad), _f32)], axis=1)
    sin_t = jnp.concatenate([-sin, sin, jnp.zeros((l, pad), _f32)], axis=1)
    return jnp.stack([cos_t * QSCALE, cos_t]), jnp.stack([sin_t * QSCALE, sin_t])


def _memkv_kernel(m_ref, w_ref, o_ref):
    o_ref[0] = jnp.dot(m_ref[...].astype(_bf16), w_ref[0], preferred_element_type=_f32).astype(o_ref.dtype)


def _memkv(mem2d, w_mkv):
    rows, d = mem2d.shape
    depth, _, n = w_mkv.shape
    tm = min(rows, PROJ_TM)
    assert rows % tm == 0
    return pl.pallas_call(
        _memkv_kernel,
        out_shape=jax.ShapeDtypeStruct((depth, rows, n), _bf16),
        grid=(depth, rows // tm),
        in_specs=[pl.BlockSpec((tm, d), lambda li, i: (i, 0)),
                  pl.BlockSpec((1, d, n), lambda li, i: (li, 0, 0))],
        out_specs=pl.BlockSpec((1, tm, n), lambda li, i: (li, i, 0)),
        compiler_params=_cparams(2),
    )(mem2d, w_mkv)


def _qk(q, k):
    return lax.dot_general(q, k, (((1,), (1,)), ((), ())), preferred_element_type=_f32)


def _band_masks(mask_ref, nk, half, deltas):
    rel = (lax.broadcasted_iota(jnp.int32, (QBLK, nk), 1) - lax.broadcasted_iota(jnp.int32, (QBLK, nk), 0))
    for v, delta in enumerate(deltas):
        mask_ref[v] = jnp.where(jnp.abs(rel + delta) <= half, 0.0, NEG_INF).astype(_f32)


def _edge_variant(i, n):
    return jnp.where(i == 0, 0, jnp.where(i == n - 1, 2, 1))


def _softmax_pv(scores, values, floors=None):
    def rowmax(s):
        m = s[:, 0:HEAD_DIM]
        for c in range(1, s.shape[1] // HEAD_DIM):
            m = jnp.maximum(m, s[:, c * HEAD_DIM:(c + 1) * HEAD_DIM])
        return jnp.max(m, axis=-1, keepdims=True)

    ms = [rowmax(s) for s in scores]
    if floors is not None:
        ms = [jnp.maximum(m, f) for m, f in zip(ms, floors)]
    ps = [jnp.exp2(s - m).astype(_bf16) for s, m in zip(scores, ms)]
    accs = [jnp.dot(p, v, preferred_element_type=_f32) for p, v in zip(ps, values)]
    return ms, accs


A_UNROLL = 2


def _attn_a_kernel(q_ref, k_ref, v_ref, sink_ref, o_ref, v1_ref, mask_ref, es_ref):
    l = k_ref.shape[1]
    nk = 3 * A_WINDOW
    nblk = l // QBLK
    v1_ref[:, 0:HEAD_DIM] = v_ref[0]
    v1_ref[:, HEAD_DIM:2 * HEAD_DIM] = jnp.ones((l, HEAD_DIM), _bf16)
    rel = (lax.broadcasted_iota(jnp.int32, (QBLK, nk), 1) - lax.broadcasted_iota(jnp.int32, (QBLK, nk), 0))
    for v, delta in enumerate((0, -A_WINDOW, -2 * A_WINDOW)):
        blk = jnp.where(jnp.abs(rel + delta) <= A_WINDOW, 0.0, NEG_INF).astype(_f32)
        for a in range(A_GROUP):
            mask_ref[v, a * QBLK:(a + 1) * QBLK, :] = blk
    for a in range(A_GROUP):
        es_ref[a * QBLK:(a + 1) * QBLK, :] = jnp.broadcast_to(sink_ref[0, a:a + 1, :] * LOG2E, (QBLK, HEAD_DIM))

    def body(t, carry):
        scores, windows = [], []
        for u in range(A_UNROLL):
            i = t * A_UNROLL + u
            q0 = pl.multiple_of(i * QBLK, QBLK)
            ks = pl.multiple_of(jnp.clip(q0 - A_WINDOW, 0, l - nk), QBLK)
            q = jnp.concatenate([q_ref[0, pl.ds(q0, QBLK), a * HEAD_DIM:(a + 1) * HEAD_DIM]
                                 for a in range(A_GROUP)], axis=0)
            scores.append(_qk(q, k_ref[0, pl.ds(ks, nk), :]) + mask_ref[_edge_variant(i, nblk)])
            windows.append((q0, ks))
        cols = range(nk // HEAD_DIM)
        ms = []
        for s in scores:
            m = s[:, 0:HEAD_DIM]
            for c in cols[1:]:
                m = jnp.maximum(m, s[:, c * HEAD_DIM:(c + 1) * HEAD_DIM])
            m = jnp.broadcast_to(jnp.max(m, axis=-1, keepdims=True), m.shape)
            ms.append(jnp.maximum(m, es_ref[...]))
        ps = [jnp.concatenate([jnp.exp2(s[:, c * HEAD_DIM:(c + 1) * HEAD_DIM] - m) for c in cols], axis=1).astype(_bf16)
              for s, m in zip(scores, ms)]
        accs = [jnp.dot(p, v1_ref[pl.ds(ks, nk), :], preferred_element_type=_f32) for p, (_, ks) in zip(ps, windows)]
        for m, acc, (q0, _) in zip(ms, accs, windows):
            z = acc[:, HEAD_DIM:] + jnp.exp2(es_ref[...] - m)
            o = (acc[:, :HEAD_DIM] / z).astype(o_ref.dtype)
            for a in range(A_GROUP):
                o_ref[0, pl.ds(q0, QBLK), a * HEAD_DIM:(a + 1) * HEAD_DIM] = o[a * QBLK:(a + 1) * QBLK, :]
        return carry

    lax.fori_loop(0, nblk // A_UNROLL, body, 0)


def _attn_a(h, sink):
    b, l, _ = h.shape
    gw = A_GROUP * HEAD_DIM
    sink_b = jnp.broadcast_to(sink.astype(_f32).reshape(A_KV_HEADS, A_GROUP, 1), (A_KV_HEADS, A_GROUP, HEAD_DIM))
    sink_b = jnp.pad(sink_b, ((0, 0), (0, 8 - A_GROUP), (0, 0)))
    return pl.pallas_call(
        _attn_a_kernel,
        out_shape=jax.ShapeDtypeStruct((b, l, MIX_W), _bf16),
        grid=(b, A_KV_HEADS),
        in_specs=[pl.BlockSpec((1, l, gw), lambda bi, g: (bi, 0, g)),
                  pl.BlockSpec((1, l, HEAD_DIM), lambda bi, g: (bi, 0, MIX_HEADS + g)),
                  pl.BlockSpec((1, l, HEAD_DIM), lambda bi, g: (bi, 0, MIX_HEADS + A_KV_HEADS + g)),
                  pl.BlockSpec((1, 8, HEAD_DIM), lambda bi, g: (g, 0, 0))],
        out_specs=pl.BlockSpec((1, l, gw), lambda bi, g: (bi, 0, g)),
        scratch_shapes=[pltpu.VMEM((l, 2 * HEAD_DIM), _bf16),
                        pltpu.VMEM((3, A_GROUP * QBLK, 3 * A_WINDOW), _f32),
                        pltpu.VMEM((A_GROUP * QBLK, HEAD_DIM), _f32)],
        compiler_params=_cparams(2),
    )(h, h, h, sink_b)


B_UNROLL = 8


def _attn_b_kernel(q0_ref, k_ref, v_ref, q1_ref, q2_ref, o_ref,
                   kd_ref, v1_ref, tmp_ref, tmp2_ref, og_ref, lse_ref, mask_ref):
    l = o_ref.shape[1]
    nk = 4 * B_HALF_WINDOW
    q_refs = (q0_ref, q1_ref, q2_ref)
    _band_masks(mask_ref, nk, B_HALF_WINDOW, (0, -B_HALF_WINDOW, -2 * B_HALF_WINDOW))

    v1_ref[0, :, 0:HEAD_DIM] = v_ref[0]
    for g in range(3):
        v1_ref[g, :, HEAD_DIM:2 * HEAD_DIM] = jnp.ones((l, HEAD_DIM), _bf16)
    d1 = B_DILATIONS[1]
    assert B_DILATIONS == (1, d1, d1 * d1)
    n1, n2 = l // d1, l // (d1 * d1)
    for src, is_key in ((k_ref, True), (v_ref, False)):
        def put(g, row0, n, rows):
            if is_key:
                kd_ref[g - 1, row0:row0 + n, :] = rows.astype(_bf16)
            else:
                v1_ref[g, row0:row0 + n, 0:HEAD_DIM] = rows.astype(_bf16)

        tmp_ref[...] = src[0].astype(_f32)
        for r in range(d1):
            rows = tmp_ref[pl.ds(r, n1, stride=d1), :]
            tmp2_ref[r * n1:(r + 1) * n1, :] = rows
            put(1, r * n1, n1, rows)
        for r in range(d1 * d1):
            put(2, r * n2, n2, tmp2_ref[pl.ds((r % d1) * n1 + r // d1, n2, stride=d1), :])

    for g, dil in reversed(list(enumerate(B_DILATIONS))):
        n = l // dil
        blocks = n // QBLK

        def body(t, carry, g=g, dil=dil, n=n, blocks=blocks):
            scores, values, dests = [], [], []
            for u in range(B_UNROLL):
                bt = t * B_UNROLL + u
                r = bt // blocks
                i = bt % blocks
                q0 = pl.multiple_of(i * QBLK, QBLK)
                ks = pl.multiple_of(jnp.clip(q0 - B_HALF_WINDOW, 0, n - nk), B_HALF_WINDOW)
                base = pl.multiple_of(r * n + ks, B_HALF_WINDOW)
                if g == 0:
                    q = q0_ref[0, pl.ds(q0, QBLK), :]
                    kw = k_ref[0, pl.ds(ks, nk), :]
                else:
                    q = q_refs[g][0, r, pl.ds(q0, QBLK), :]
                    kw = kd_ref[g - 1, pl.ds(base, nk), :]
                scores.append(_qk(q, kw) + mask_ref[_edge_variant(i, blocks)])
                values.append(v1_ref[g, pl.ds(base, nk), :])
                dests.append((q0, r))
            ms, accs = _softmax_pv(scores, values)
            for m, acc, (q0, r) in zip(ms, accs, dests):
                z = acc[:, HEAD_DIM:]
                o = acc[:, :HEAD_DIM] / z
                lse = m + jnp.log2(z)
                if dil > 1:
                    og_ref[g - 1, pl.ds(q0 * dil + r, QBLK, stride=dil), :] = o
                    lse_ref[g - 1, pl.ds(q0 * dil + r, QBLK, stride=dil), :] = lse
                else:
                    ls = [lse] + [lse_ref[j, pl.ds(q0, QBLK), :] for j in range(2)]
                    top = jnp.maximum(jnp.maximum(ls[0], ls[1]), ls[2])
                    ws = [jnp.exp2(x - top) for x in ls]
                    num = ws[0] * o + ws[1] * og_ref[0, pl.ds(q0, QBLK), :] + ws[2] * og_ref[1, pl.ds(q0, QBLK), :]
                    o_ref[0, pl.ds(q0, QBLK), :] = (num / (ws[0] + ws[1] + ws[2])).astype(o_ref.dtype)
            return carry

        lax.fori_loop(0, dil * blocks // B_UNROLL, body, 0)


def _attn_b(h0, hq1, hq2):
    b, l, _ = h0.shape
    in_specs = [pl.BlockSpec((1, l, HEAD_DIM), lambda bi, h: (bi, 0, h)),
                pl.BlockSpec((1, l, HEAD_DIM), lambda bi, h: (bi, 0, MIX_HEADS + h)),
                pl.BlockSpec((1, l, HEAD_DIM), lambda bi, h: (bi, 0, 2 * MIX_HEADS + h))]
    for dil in B_DILATIONS[1:]:
        in_specs.append(pl.BlockSpec((1, dil, l // dil, HEAD_DIM), lambda bi, h: (bi, 0, 0, h)))
    return pl.pallas_call(
        _attn_b_kernel,
        out_shape=jax.ShapeDtypeStruct((b, l, MIX_W), _bf16),
        grid=(b, MIX_HEADS),
        in_specs=in_specs,
        out_specs=pl.BlockSpec((1, l, HEAD_DIM), lambda bi, h: (bi, 0, h)),
        scratch_shapes=[pltpu.VMEM((2, l, HEAD_DIM), _bf16),
                        pltpu.VMEM((3, l, 2 * HEAD_DIM), _bf16),
                        pltpu.VMEM((l, HEAD_DIM), _f32),
                        pltpu.VMEM((l, HEAD_DIM), _f32),
                        pltpu.VMEM((2, l, HEAD_DIM), _f32),
                        pltpu.VMEM((2, l, HEAD_DIM), _f32),
                        pltpu.VMEM((3, QBLK, 4 * B_HALF_WINDOW), _f32)],
        compiler_params=_cparams(2),
    )(h0, h0, h0, hq1, hq2)


C_KSEG = 4 * SUB
C_NSEG = 4
C_PAD = SUB
C_NKEYS = C_NSEG * C_KSEG


def _attn_c_kernel(q_ref, k_ref, v_ref, bias_ref, mask_ref, o_ref, kp_ref, vp_ref, bm_ref):
    l = k_ref.shape[1]
    n_rb = l // GRID_W // NA_ROWS
    n_cb = GRID_W // NA_COLS
    n_r4 = l // GRID_W // SUB_R
    kp_ref[0:C_PAD, :] = jnp.zeros((C_PAD, HEAD_DIM), _bf16)
    kp_ref[C_PAD + l:C_PAD + l + C_PAD, :] = jnp.zeros((C_PAD, HEAD_DIM), _bf16)
    kp_ref[C_PAD:C_PAD + l, :] = k_ref[0]
    vp_ref[0:C_PAD, 0:HEAD_DIM] = jnp.zeros((C_PAD, HEAD_DIM), _bf16)
    vp_ref[C_PAD + l:C_PAD + l + C_PAD, 0:HEAD_DIM] = jnp.zeros((C_PAD, HEAD_DIM), _bf16)
    vp_ref[C_PAD:C_PAD + l, 0:HEAD_DIM] = v_ref[0]
    vp_ref[:, HEAD_DIM:2 * HEAD_DIM] = jnp.ones((l + 2 * C_PAD, HEAD_DIM), _bf16)
    for v in range(9):
        bm_ref[v] = bias_ref[0] + mask_ref[v]

    def body(rb, carry):
        rv = _edge_variant(rb, n_rb)
        scores, values, dests = [], [], []
        for cb in range(n_cb):
            cv = 0 if cb == 0 else (2 if cb == n_cb - 1 else 1)
            q_starts = [pl.multiple_of(((2 * rb + a) * SUBS_PER_ROW + 2 * cb) * SUB, 2 * SUB) for a in range(2)]
            q = jnp.concatenate([q_ref[0, pl.ds(qs, 2 * SUB), :] for qs in q_starts], axis=0)
            kparts, vparts = [], []
            for ar in range(C_NSEG):
                r4 = jnp.clip(2 * rb - 1 + ar, 0, n_r4 - 1)
                st = pl.multiple_of(C_PAD + (r4 * SUBS_PER_ROW + 2 * cb - 1) * SUB, SUB)
                kparts.append(kp_ref[pl.ds(st, C_KSEG), :])
                vparts.append(vp_ref[pl.ds(st, C_KSEG), :])
            scores.append(_qk(q, jnp.concatenate(kparts, axis=0)) + bm_ref[rv * 3 + cv])
            values.append(jnp.concatenate(vparts, axis=0))
            dests.append(q_starts)
        ms, accs = _softmax_pv(scores, values)
        for acc, q_starts in zip(accs, dests):
            o = (acc[:, :HEAD_DIM] / acc[:, HEAD_DIM:]).astype(o_ref.dtype)
            for a in range(2):
                o_ref[0, pl.ds(q_starts[a], 2 * SUB), :] = o[a * 2 * SUB:(a + 1) * 2 * SUB, :]
        return carry

    lax.fori_loop(0, n_rb, body, 0)


def _c_geometry():
    ql = jnp.arange(QBLK)
    qa, qc, qi, qj = ql // 64, (ql // 32) % 2, (ql // 8) % 4, ql % 8
    q_row = SUB_R * qa + qi
    q_col = SUB_C * qc + qj
    kl = jnp.arange(C_NKEYS)
    ka, kc, ki, kj = kl // C_KSEG, (kl // SUB) % 4, (kl // 8) % 4, kl % 8
    k_row = SUB_R * (ka - 1) + ki
    k_col = SUB_C * (kc - 1) + kj
    return q_row, q_col, k_row, k_col


def _c_bias(rpb):
    q_row, q_col, k_row, k_col = _c_geometry()
    nr, nc = 2 * NA_ROWS - 1, 2 * NA_COLS - 1
    dr = jnp.clip(k_row[None, :] - q_row[:, None] + NA_ROWS - 1, 0, nr - 1)
    dc = jnp.clip(k_col[None, :] - q_col[:, None] + NA_COLS - 1, 0, nc - 1)
    oh_r = (dr[:, :, None] == jnp.arange(nr)).astype(_f32)
    oh_c = (dc[:, :, None] == jnp.arange(nc)).astype(_f32)
    rows = jnp.einsum("hrc,qkr->hqkc", rpb.astype(_f32), oh_r, precision=lax.Precision.HIGHEST)
    return jnp.sum(rows * oh_c[None], axis=-1) * LOG2E


def _c_masks(rows):
    q_row, q_col, k_row, k_col = _c_geometry()
    out = []
    for rb in (0, 1, rows // NA_ROWS - 1):
        qr, kr = NA_ROWS * rb + q_row, NA_ROWS * rb + k_row
        rs = jnp.clip(qr - NA_ROWS // 2, 0, rows - NA_ROWS)
        rvalid = (kr[None, :] >= rs[:, None]) & (kr[None, :] < rs[:, None] + NA_ROWS) & (kr[None, :] >= 0) & (kr[None, :] < rows)
        for cb in (0, 1, GRID_W // NA_COLS - 1):
            qc, kc = NA_COLS * cb + q_col, NA_COLS * cb + k_col
            cs = jnp.clip(qc - NA_COLS // 2, 0, GRID_W - NA_COLS)
            cvalid = (kc[None, :] >= cs[:, None]) & (kc[None, :] < cs[:, None] + NA_COLS) & (kc[None, :] >= 0) & (kc[None, :] < GRID_W)
            out.append(jnp.where(rvalid & cvalid, 0.0, NEG_INF).astype(_f32))
    return jnp.stack(out)


def _attn_c(h, rpb):
    b, l, _ = h.shape
    bias = _c_bias(rpb)
    masks = _c_masks(l // GRID_W)
    return pl.pallas_call(
        _attn_c_kernel,
        out_shape=jax.ShapeDtypeStruct((b, l, MIX_W), _bf16),
        grid=(b, MIX_HEADS),
        in_specs=[pl.BlockSpec((1, l, HEAD_DIM), lambda bi, h: (bi, 0, h)),
                  pl.BlockSpec((1, l, HEAD_DIM), lambda bi, h: (bi, 0, MIX_HEADS + h)),
                  pl.BlockSpec((1, l, HEAD_DIM), lambda bi, h: (bi, 0, 2 * MIX_HEADS + h)),
                  pl.BlockSpec((1, QBLK, C_NKEYS), lambda bi, h: (h, 0, 0)),
                  pl.BlockSpec((9, QBLK, C_NKEYS), lambda bi, h: (0, 0, 0))],
        out_specs=pl.BlockSpec((1, l, HEAD_DIM), lambda bi, h: (bi, 0, h)),
        scratch_shapes=[pltpu.VMEM((l + 2 * C_PAD, HEAD_DIM), _bf16),
                        pltpu.VMEM((l + 2 * C_PAD, 2 * HEAD_DIM), _bf16),
                        pltpu.VMEM((9, QBLK, C_NKEYS), _f32)],
        compiler_params=_cparams(2),
    )(h, h, h, bias, masks)


def _subblock_chunks(tm):
    grp = SUB * SUBS_PER_ROW
    pairs = []
    for g in range(tm // grp):
        for c8 in range(SUBS_PER_ROW):
            for i4 in range(SUB_R):
                pairs.append((g * grp + i4 * GRID_W + c8 * SUB_C, (g * SUBS_PER_ROW + c8) * SUB + i4 * SUB_C))
    return pairs


def _out_kernel(om_ref, glo_ref, ghi_ref, mq_ref, kv_ref, x_ref, w_ref, g_ref, b_ref, o_ref, xs_ref, *, subblock):
    tm = om_ref.shape[1]
    half = D_INNER // 2

    def silu_gate(c0, c1):
        ref, off = (glo_ref, 0) if c0 < half else (ghi_ref, half)
        hg = 0.5 * ref[0, :, c0 - off:c1 - off].astype(_f32)
        return hg + hg * jnp.tanh(hg)

    ones = jnp.ones((kv_ref.shape[1], HEAD_DIM), _bf16)
    scores = [_qk(mq_ref[0, :, h * HEAD_DIM:(h + 1) * HEAD_DIM], kv_ref[0, :, h * HEAD_DIM:(h + 1) * HEAD_DIM]) * QSCALE
              for h in range(MEM_HEADS)]
    values = [jnp.concatenate([kv_ref[0, :, MEM_W + h * HEAD_DIM:MEM_W + (h + 1) * HEAD_DIM], ones], axis=1)
              for h in range(MEM_HEADS)]
    _, accs = _softmax_pv(scores, values)
    branch = None
    for c0 in range(0, D_INNER, MEM_W):
        if c0 < MIX_W:
            y = om_ref[0, :, c0:c0 + MEM_W].astype(_f32) * silu_gate(c0, c0 + MEM_W)
        else:
            y = jnp.concatenate([acc[:, :HEAD_DIM] / acc[:, HEAD_DIM:] for acc in accs], axis=1) * silu_gate(c0, c0 + MEM_W)
        part = jnp.dot(y.astype(_bf16), w_ref[c0:c0 + MEM_W, :], preferred_element_type=_f32)
        branch = part if branch is None else branch + part
    if subblock:
        for nat, sub in _subblock_chunks(tm):
            xs_ref[sub:sub + SUB_C, :] = x_ref[0, nat:nat + SUB_C, :]
        xin = xs_ref[...]
    else:
        xin = x_ref[0]
    z = ALPHA * xin + branch
    mu = jnp.mean(z, axis=-1, keepdims=True)
    zc = z - mu
    var = jnp.mean(zc * zc, axis=-1, keepdims=True)
    out = zc * lax.rsqrt(var + LN_EPS) * g_ref[...] + b_ref[...]
    if subblock:
        xs_ref[...] = out
        for nat, sub in _subblock_chunks(tm):
            o_ref[0, nat:nat + SUB_C, :] = xs_ref[sub:sub + SUB_C, :]
    else:
        o_ref[0] = out


def _out(o_mix, h, mq_col, kv, x, w_out, ln_g, ln_b, *, subblock):
    b, l, d = x.shape
    tm = OUT_TM
    mlen = kv.shape[1]
    half = D_INNER // 2
    gate_col = mq_col + MEM_W
    assert mq_col % MEM_W == 0 and gate_col % half == 0
    kern = functools.partial(_out_kernel, subblock=subblock)
    return pl.pallas_call(
        kern,
        out_shape=jax.ShapeDtypeStruct((b, l, d), _f32),
        grid=(b, l // tm),
        in_specs=[pl.BlockSpec((1, tm, MIX_W), lambda bi, i: (bi, i, 0)),
                  pl.BlockSpec((1, tm, half), lambda bi, i: (bi, i, gate_col // half)),
                  pl.BlockSpec((1, tm, half), lambda bi, i: (bi, i, gate_col // half + 1)),
                  pl.BlockSpec((1, tm, MEM_W), lambda bi, i: (bi, i, mq_col // MEM_W)),
                  pl.BlockSpec((1, mlen, 2 * MEM_W), lambda bi, i: (bi, 0, 0)),
                  pl.BlockSpec((1, tm, d), lambda bi, i: (bi, i, 0)),
                  pl.BlockSpec((D_INNER, d), lambda bi, i: (0, 0)),
                  pl.BlockSpec((1, d), lambda bi, i: (0, 0)),
                  pl.BlockSpec((1, d), lambda bi, i: (0, 0))],
        out_specs=pl.BlockSpec((1, tm, d), lambda bi, i: (bi, i, 0)),
        scratch_shapes=[pltpu.VMEM((tm, d), _f32)],
        compiler_params=_cparams(2),
    )(o_mix, h, h, h, kv, x, w_out, ln_g.reshape(1, d), ln_b.reshape(1, d))


def _split_cols(w, sizes):
    out, c = [], 0
    for s in sizes:
        out.append(w[:, c:c + s])
        c += s
    return out


def kernel(x, mem, w_in_a, sink_a, w_in_b, w_in_c, rpb_c, w_mkv, w_out, ln_g, ln_b):
    b, l, d = x.shape
    mlen = mem.shape[1]
    kv_all = _memkv(mem.reshape(b * mlen, d), w_mkv.astype(_bf16)).reshape(DEPTH, b, mlen, 2 * MEM_W)
    tables = _rope_tables(l)

    for i in range(DEPTH):
        kind, j = i % NUM_MIXERS, i // NUM_MIXERS
        if kind == 0:
            h = _proj(x, w_in_a[j].astype(_bf16), tables, n_q_cols=MIX_W, n_rope_cols=MIX_W + MEM_W)
            o_mix = _attn_a(h, sink_a[j])
            mq_col = MIX_W + 2 * MEM_W
        elif kind == 1:
            wq0, wq1, wq2, wrest = _split_cols(w_in_b[j].astype(_bf16), (MIX_W, MIX_W, MIX_W, 2 * MIX_W + MEM_W + D_INNER))
            h = _proj(x, jnp.concatenate([wq0, wrest], axis=1), tables, n_q_cols=MIX_W, n_rope_cols=2 * MIX_W)
            hq1 = _proj(x, wq1, tables, perm="dilate", dil=B_DILATIONS[1], n_q_cols=MIX_W, n_rope_cols=MIX_W)
            hq2 = _proj(x, wq2, tables, perm="dilate", dil=B_DILATIONS[2], n_q_cols=MIX_W, n_rope_cols=MIX_W)
            o_mix = _attn_b(h, hq1, hq2)
            mq_col = 3 * MIX_W
        else:
            h = _proj(x, w_in_c[j].astype(_bf16), perm="subblock", n_q_cols=MIX_W)
            o_mix = _attn_c(h, rpb_c[j])
            mq_col = 3 * MIX_W
        x = _out(o_mix, h, mq_col, kv_all[i], x, w_out[i].astype(_bf16), ln_g[i], ln_b[i], subblock=(kind == 2))
    return x
```

```python
import functools
import math

import jax
import jax.numpy as jnp
from jax import lax
from jax.experimental import pallas as pl
from jax.experimental.pallas import tpu as pltpu

D_MODEL = 1024
DEPTH = 4
NUM_MIXERS = 3
HEAD_DIM = 128
D_INNER = 2 * D_MODEL
MEM_HEADS = 4
MIX_HEADS = D_INNER // HEAD_DIM - MEM_HEADS
A_KV_HEADS = MIX_HEADS // 3
A_GROUP = MIX_HEADS // A_KV_HEADS
A_WINDOW = 128
B_DILATIONS = (1, 4, 16)
B_HALF_WINDOW = 64
NA_ROWS = 8
NA_COLS = 16
GRID_W = 64
ROPE_THETA = 500000.0
ROPE_DIMS = HEAD_DIM // 4
ROPE_HALF = ROPE_DIMS // 2
LN_EPS = 1e-5
ALPHA = (2 * DEPTH) ** 0.25
NEG_INF = -1e30
LOG2E = math.log2(math.e)
QSCALE = HEAD_DIM ** -0.5 * LOG2E

MIX_W = MIX_HEADS * HEAD_DIM
MEM_W = MEM_HEADS * HEAD_DIM

SUB_R = 4
SUB_C = 8
SUB = SUB_R * SUB_C
SUBS_PER_ROW = GRID_W // SUB_C
QBLK = 128

VMEM_LIMIT = 56 * 1024 * 1024
PROJ_TM = 512
PROJ_TN = 512
OUT_TM = 512

_f32 = jnp.float32
_bf16 = jnp.bfloat16


def _cparams(n_grid):
    return pltpu.CompilerParams(dimension_semantics=("arbitrary",) * n_grid,
                                vmem_limit_bytes=VMEM_LIMIT)


def _rope(blk, cos, sin):
    lane = lax.broadcasted_iota(jnp.int32, blk.shape, 1)
    up = pltpu.roll(blk, HEAD_DIM - ROPE_HALF, 1)
    down = pltpu.roll(blk, ROPE_HALF, 1)
    swapped = jnp.where(lane < ROPE_HALF, up, down)
    return blk * cos + swapped * sin


def _proj_kernel(*refs, subblock, segments, has_tables):
    x_ref, w_ref = refs[:2]
    pos = 2
    if has_tables:
        cos_ref, sin_ref = refs[2:4]
        pos = 4
    out_refs = refs[pos:pos + len(segments)]
    xs_ref, tmp_ref = refs[pos + len(segments):]
    tm = xs_ref.shape[0]
    tn = PROJ_TN

    if subblock:
        grp = SUB * SUBS_PER_ROW
        for g in range(tm // grp):
            for c8 in range(SUBS_PER_ROW):
                parts = [x_ref[0, g * grp + i4 * GRID_W + c8 * SUB_C:
                               g * grp + i4 * GRID_W + (c8 + 1) * SUB_C, :] for i4 in range(SUB_R)]
                dst = (g * SUBS_PER_ROW + c8) * SUB
                xs_ref[dst:dst + SUB, :] = jnp.concatenate(parts, axis=0).astype(_bf16)
    else:
        xs_ref[...] = x_ref[0].astype(_bf16)

    def write(o_ref, dil, val, c0):
        c1 = c0 + HEAD_DIM
        if dil == 1:
            o_ref[0, :, c0:c1] = val.astype(o_ref.dtype)
            return
        n = tm // dil
        tmp_ref[0] = val
        if dil == B_DILATIONS[2]:
            d1 = B_DILATIONS[1]
            n1 = tm // d1
            for r in range(d1):
                tmp_ref[1, r * n1:(r + 1) * n1, :] = tmp_ref[0, pl.ds(r, n1, stride=d1), :]
            for r in range(dil):
                rows = tmp_ref[1, pl.ds((r % d1) * n1 + r // d1, n, stride=d1), :]
                o_ref[0, r, :, c0:c1] = rows.astype(o_ref.dtype)
        else:
            for r in range(dil):
                o_ref[0, r, :, c0:c1] = tmp_ref[0, pl.ds(r, n, stride=dil), :].astype(o_ref.dtype)

    wcol = 0
    for o_ref, (dil, parts) in zip(out_refs, segments):
        ocol = 0
        for n_cols, mode in parts:
            for _ in range(n_cols // tn):
                acc = jnp.dot(xs_ref[...], w_ref[:, wcol:wcol + tn], preferred_element_type=_f32)
                for hd in range(tn // HEAD_DIM):
                    val = acc[:, hd * HEAD_DIM:(hd + 1) * HEAD_DIM]
                    if mode == "rope_q":
                        val = _rope(val, cos_ref[0], sin_ref[0])
                    elif mode == "rope_k":
                        val = _rope(val, cos_ref[1], sin_ref[1])
                    elif mode == "scale":
                        val = val * QSCALE
                    write(o_ref, dil, val, ocol + hd * HEAD_DIM)
                wcol += tn
                ocol += tn


def _proj(x, w, tables, segments, *, subblock=False):
    b, l, d = x.shape
    tm, tn = PROJ_TM, PROJ_TN
    tiles = l // tm
    assert l % tm == 0
    widths = [sum(n for n, _ in parts) for _, parts in segments]
    assert sum(widths) == w.shape[1] and all(n % tn == 0 for _, parts in segments for n, _ in parts)
    out_shapes, out_specs = [], []
    for (dil, _), n in zip(segments, widths):
        if dil == 1:
            out_shapes.append(jax.ShapeDtypeStruct((b, l, n), _bf16))
            out_specs.append(pl.BlockSpec((1, tm, n), lambda i: (i // tiles, i % tiles, 0)))
        else:
            out_shapes.append(jax.ShapeDtypeStruct((b, dil, l // dil, n), _bf16))
            out_specs.append(pl.BlockSpec((1, dil, tm // dil, n), lambda i: (i // tiles, 0, i % tiles, 0)))
    in_specs = [pl.BlockSpec((1, tm, d), lambda i: (i // tiles, i % tiles, 0)),
                pl.BlockSpec((d, w.shape[1]), lambda i: (0, 0), pipeline_mode=pl.Buffered(1))]
    args = [x, w]
    has_tables = any(mode.startswith("rope") for _, parts in segments for _, mode in parts)
    if has_tables:
        tab_spec = pl.BlockSpec((2, tm, HEAD_DIM), lambda i: (0, i % tiles, 0))
        in_specs += [tab_spec, tab_spec]
        args += list(tables)
    kern = functools.partial(_proj_kernel, subblock=subblock, segments=segments, has_tables=has_tables)
    return pl.pallas_call(
        kern,
        out_shape=out_shapes,
        grid=(b * tiles,),
        in_specs=in_specs,
        out_specs=out_specs,
        scratch_shapes=[pltpu.VMEM((tm, d), _bf16), pltpu.VMEM((2, tm, HEAD_DIM), _f32)],
        compiler_params=_cparams(1),
    )(*args)


def _rope_tables(l):
    inv = ROPE_THETA ** (-jnp.arange(ROPE_HALF, dtype=_f32) / ROPE_HALF)
    ang = jnp.arange(l).astype(_f32)[:, None] * inv[None, :]
    cos, sin = jnp.cos(ang), jnp.sin(ang)
    pad = HEAD_DIM - ROPE_DIMS
    cos_t = jnp.concatenate([cos, cos, jnp.ones((l, pad), _f32)], axis=1)
    sin_t = jnp.concatenate([-sin, sin, jnp.zeros((l, pad), _f32)], axis=1)
    return jnp.stack([cos_t * QSCALE, cos_t]), jnp.stack([sin_t * QSCALE, sin_t])


def _memkv_kernel(m_ref, w_ref, o_ref):
    o_ref[0] = jnp.dot(m_ref[...].astype(_bf16), w_ref[0], preferred_element_type=_f32).astype(o_ref.dtype)


def _memkv(mem2d, w_mkv):
    rows, d = mem2d.shape
    depth, _, n = w_mkv.shape
    tm = min(rows, PROJ_TM)
    assert rows % tm == 0
    return pl.pallas_call(
        _memkv_kernel,
        out_shape=jax.ShapeDtypeStruct((depth, rows, n), _bf16),
        grid=(depth, rows // tm),
        in_specs=[pl.BlockSpec((tm, d), lambda li, i: (i, 0)),
                  pl.BlockSpec((1, d, n), lambda li, i: (li, 0, 0))],
        out_specs=pl.BlockSpec((1, tm, n), lambda li, i: (li, i, 0)),
        compiler_params=_cparams(2),
    )(mem2d, w_mkv)


def _qk(q, k):
    return lax.dot_general(q, k, (((1,), (1,)), ((), ())), preferred_element_type=_f32)


def _band_masks(mask_ref, nk, half, deltas):
    rel = (lax.broadcasted_iota(jnp.int32, (QBLK, nk), 1) - lax.broadcasted_iota(jnp.int32, (QBLK, nk), 0))
    for v, delta in enumerate(deltas):
        mask_ref[v] = jnp.where(jnp.abs(rel + delta) <= half, 0.0, NEG_INF).astype(_f32)


def _edge_variant(i, n):
    return jnp.where(i == 0, 0, jnp.where(i == n - 1, 2, 1))


def _softmax_pv(scores, values, floors=None):
    def rowmax(s):
        m = s[:, 0:HEAD_DIM]
        for c in range(1, s.shape[1] // HEAD_DIM):
            m = jnp.maximum(m, s[:, c * HEAD_DIM:(c + 1) * HEAD_DIM])
        return jnp.max(m, axis=-1, keepdims=True)

    ms = [rowmax(s) for s in scores]
    if floors is not None:
        ms = [jnp.maximum(m, f) for m, f in zip(ms, floors)]
    ps = [jnp.exp2(s - m).astype(_bf16) for s, m in zip(scores, ms)]
    accs = [jnp.dot(p, v, preferred_element_type=_f32) for p, v in zip(ps, values)]
    return ms, accs


A_UNROLL = 4


def _attn_a_kernel(q_ref, k_ref, v_ref, sink_ref, o_ref, v1_ref, mask_ref, es_ref):
    l = k_ref.shape[1]
    nk = 3 * A_WINDOW
    nblk = l // QBLK
    v1_ref[:, 0:HEAD_DIM] = v_ref[0]
    v1_ref[:, HEAD_DIM:2 * HEAD_DIM] = jnp.ones((l, HEAD_DIM), _bf16)
    rel = (lax.broadcasted_iota(jnp.int32, (QBLK, nk), 1) - lax.broadcasted_iota(jnp.int32, (QBLK, nk), 0))
    for v, delta in enumerate((0, -A_WINDOW, -2 * A_WINDOW)):
        blk = jnp.where(jnp.abs(rel + delta) <= A_WINDOW, 0.0, NEG_INF).astype(_f32)
        for a in range(A_GROUP):
            mask_ref[v, a * QBLK:(a + 1) * QBLK, :] = blk
    for a in range(A_GROUP):
        es_ref[a * QBLK:(a + 1) * QBLK, :] = jnp.broadcast_to(sink_ref[0, a:a + 1, :] * LOG2E, (QBLK, HEAD_DIM))

    def body(t, carry):
        scores, windows = [], []
        for u in range(A_UNROLL):
            i = t * A_UNROLL + u
            q0 = pl.multiple_of(i * QBLK, QBLK)
            ks = pl.multiple_of(jnp.clip(q0 - A_WINDOW, 0, l - nk), QBLK)
            q = jnp.concatenate([q_ref[0, pl.ds(q0, QBLK), a * HEAD_DIM:(a + 1) * HEAD_DIM]
                                 for a in range(A_GROUP)], axis=0)
            scores.append(_qk(q, k_ref[0, pl.ds(ks, nk), :]) + mask_ref[_edge_variant(i, nblk)])
            windows.append((q0, ks))
        cols = range(nk // HEAD_DIM)
        ms = []
        for s in scores:
            m = s[:, 0:HEAD_DIM]
            for c in cols[1:]:
                m = jnp.maximum(m, s[:, c * HEAD_DIM:(c + 1) * HEAD_DIM])
            m = jnp.broadcast_to(jnp.max(m, axis=-1, keepdims=True), m.shape)
            ms.append(jnp.maximum(m, es_ref[...]))
        ps = [jnp.concatenate([jnp.exp2(s[:, c * HEAD_DIM:(c + 1) * HEAD_DIM] - m) for c in cols], axis=1).astype(_bf16)
              for s, m in zip(scores, ms)]
        accs = [jnp.dot(p, v1_ref[pl.ds(ks, nk), :], preferred_element_type=_f32) for p, (_, ks) in zip(ps, windows)]
        for m, acc, (q0, _) in zip(ms, accs, windows):
            z = acc[:, HEAD_DIM:] + jnp.exp2(es_ref[...] - m)
            o = (acc[:, :HEAD_DIM] / z).astype(o_ref.dtype)
            for a in range(A_GROUP):
                o_ref[0, pl.ds(q0, QBLK), a * HEAD_DIM:(a + 1) * HEAD_DIM] = o[a * QBLK:(a + 1) * QBLK, :]
        return carry

    lax.fori_loop(0, nblk // A_UNROLL, body, 0)


def _attn_a(h, sink):
    b, l, _ = h.shape
    gw = A_GROUP * HEAD_DIM
    sink_b = jnp.broadcast_to(sink.astype(_f32).reshape(A_KV_HEADS, A_GROUP, 1), (A_KV_HEADS, A_GROUP, HEAD_DIM))
    sink_b = jnp.pad(sink_b, ((0, 0), (0, 8 - A_GROUP), (0, 0)))
    return pl.pallas_call(
        _attn_a_kernel,
        out_shape=jax.ShapeDtypeStruct((b, l, MIX_W), _bf16),
        grid=(b, A_KV_HEADS),
        in_specs=[pl.BlockSpec((1, l, gw), lambda bi, g: (bi, 0, g)),
                  pl.BlockSpec((1, l, HEAD_DIM), lambda bi, g: (bi, 0, MIX_HEADS + g)),
                  pl.BlockSpec((1, l, HEAD_DIM), lambda bi, g: (bi, 0, MIX_HEADS + A_KV_HEADS + g)),
                  pl.BlockSpec((1, 8, HEAD_DIM), lambda bi, g: (g, 0, 0))],
        out_specs=pl.BlockSpec((1, l, gw), lambda bi, g: (bi, 0, g)),
        scratch_shapes=[pltpu.VMEM((l, 2 * HEAD_DIM), _bf16),
                        pltpu.VMEM((3, A_GROUP * QBLK, 3 * A_WINDOW), _f32),
                        pltpu.VMEM((A_GROUP * QBLK, HEAD_DIM), _f32)],
        compiler_params=_cparams(2),
    )(h, h, h, sink_b)


B_UNROLL = 16


def _attn_b_kernel(q0_ref, k_ref, v_ref, q1_ref, q2_ref, o_ref,
                   kd_ref, v1_ref, tmp_ref, tmp2_ref, og_ref, lse_ref, mask_ref):
    l = o_ref.shape[1]
    nk = 4 * B_HALF_WINDOW
    q_refs = (q0_ref, q1_ref, q2_ref)
    _band_masks(mask_ref, nk, B_HALF_WINDOW, (0, -B_HALF_WINDOW, -2 * B_HALF_WINDOW))

    v1_ref[0, :, 0:HEAD_DIM] = v_ref[0]
    for g in range(3):
        v1_ref[g, :, HEAD_DIM:2 * HEAD_DIM] = jnp.ones((l, HEAD_DIM), _bf16)
    d1 = B_DILATIONS[1]
    assert B_DILATIONS == (1, d1, d1 * d1)
    n1, n2 = l // d1, l // (d1 * d1)
    for src, is_key in ((k_ref, True), (v_ref, False)):
        def put(g, row0, n, rows):
            if is_key:
                kd_ref[g - 1, row0:row0 + n, :] = rows.astype(_bf16)
            else:
                v1_ref[g, row0:row0 + n, 0:HEAD_DIM] = rows.astype(_bf16)

        tmp_ref[...] = src[0].astype(_f32)
        for r in range(d1):
            rows = tmp_ref[pl.ds(r, n1, stride=d1), :]
            tmp2_ref[r * n1:(r + 1) * n1, :] = rows
            put(1, r * n1, n1, rows)
        for r in range(d1 * d1):
            put(2, r * n2, n2, tmp2_ref[pl.ds((r % d1) * n1 + r // d1, n2, stride=d1), :])

    for g, dil in reversed(list(enumerate(B_DILATIONS))):
        n = l // dil
        blocks = n // QBLK

        def body(t, carry, g=g, dil=dil, n=n, blocks=blocks):
            scores, values, dests = [], [], []
            for u in range(B_UNROLL):
                bt = t * B_UNROLL + u
                r = bt // blocks
                i = bt % blocks
                q0 = pl.multiple_of(i * QBLK, QBLK)
                ks = pl.multiple_of(jnp.clip(q0 - B_HALF_WINDOW, 0, n - nk), B_HALF_WINDOW)
                base = pl.multiple_of(r * n + ks, B_HALF_WINDOW)
                if g == 0:
                    q = q0_ref[0, pl.ds(q0, QBLK), :]
                    kw = k_ref[0, pl.ds(ks, nk), :]
                else:
                    q = q_refs[g][0, r, pl.ds(q0, QBLK), :]
                    kw = kd_ref[g - 1, pl.ds(base, nk), :]
                scores.append(_qk(q, kw) + mask_ref[_edge_variant(i, blocks)])
                values.append(v1_ref[g, pl.ds(base, nk), :])
                dests.append((q0, r))
            ms, accs = _softmax_pv(scores, values)
            for m, acc, (q0, r) in zip(ms, accs, dests):
                z = acc[:, HEAD_DIM:]
                o = acc[:, :HEAD_DIM] / z
                lse = m + jnp.log2(z)
                if dil > 1:
                    og_ref[g - 1, pl.ds(q0 * dil + r, QBLK, stride=dil), :] = o
                    lse_ref[g - 1, pl.ds(q0 * dil + r, QBLK, stride=dil), :] = lse
                else:
                    ls = [lse] + [lse_ref[j, pl.ds(q0, QBLK), :] for j in range(2)]
                    top = jnp.maximum(jnp.maximum(ls[0], ls[1]), ls[2])
                    ws = [jnp.exp2(x - top) for x in ls]
                    num = ws[0] * o + ws[1] * og_ref[0, pl.ds(q0, QBLK), :] + ws[2] * og_ref[1, pl.ds(q0, QBLK), :]
                    o_ref[0, pl.ds(q0, QBLK), :] = (num / (ws[0] + ws[1] + ws[2])).astype(o_ref.dtype)
            return carry

        lax.fori_loop(0, dil * blocks // B_UNROLL, body, 0)


def _attn_b(h0, hq1, hq2):
    b, l, _ = h0.shape
    in_specs = [pl.BlockSpec((1, l, HEAD_DIM), lambda bi, h: (bi, 0, h)),
                pl.BlockSpec((1, l, HEAD_DIM), lambda bi, h: (bi, 0, MIX_HEADS + h)),
                pl.BlockSpec((1, l, HEAD_DIM), lambda bi, h: (bi, 0, 2 * MIX_HEADS + h))]
    for dil in B_DILATIONS[1:]:
        in_specs.append(pl.BlockSpec((1, dil, l // dil, HEAD_DIM), lambda bi, h: (bi, 0, 0, h)))
    return pl.pallas_call(
        _attn_b_kernel,
        out_shape=jax.ShapeDtypeStruct((b, l, MIX_W), _bf16),
        grid=(b, MIX_HEADS),
        in_specs=in_specs,
        out_specs=pl.BlockSpec((1, l, HEAD_DIM), lambda bi, h: (bi, 0, h)),
        scratch_shapes=[pltpu.VMEM((2, l, HEAD_DIM), _bf16),
                        pltpu.VMEM((3, l, 2 * HEAD_DIM), _bf16),
                        pltpu.VMEM((l, HEAD_DIM), _f32),
                        pltpu.VMEM((l, HEAD_DIM), _f32),
                        pltpu.VMEM((2, l, HEAD_DIM), _f32),
                        pltpu.VMEM((2, l, HEAD_DIM), _f32),
                        pltpu.VMEM((3, QBLK, 4 * B_HALF_WINDOW), _f32)],
        compiler_params=_cparams(2),
    )(h0, h0, h0, hq1, hq2)


C_KSEG = 4 * SUB
C_NSEG = 4
C_PAD = SUB
C_NKEYS = C_NSEG * C_KSEG
C_UNROLL = 2


def _attn_c_kernel(q_ref, k_ref, v_ref, bias_ref, mask_ref, o_ref, kp_ref, vp_ref, bm_ref):
    l = k_ref.shape[1]
    n_rb = l // GRID_W // NA_ROWS
    n_cb = GRID_W // NA_COLS
    n_r4 = l // GRID_W // SUB_R
    kp_ref[0:C_PAD, :] = jnp.zeros((C_PAD, HEAD_DIM), _bf16)
    kp_ref[C_PAD + l:C_PAD + l + C_PAD, :] = jnp.zeros((C_PAD, HEAD_DIM), _bf16)
    kp_ref[C_PAD:C_PAD + l, :] = k_ref[0]
    vp_ref[0:C_PAD, 0:HEAD_DIM] = jnp.zeros((C_PAD, HEAD_DIM), _bf16)
    vp_ref[C_PAD + l:C_PAD + l + C_PAD, 0:HEAD_DIM] = jnp.zeros((C_PAD, HEAD_DIM), _bf16)
    vp_ref[C_PAD:C_PAD + l, 0:HEAD_DIM] = v_ref[0]
    vp_ref[:, HEAD_DIM:2 * HEAD_DIM] = jnp.ones((l + 2 * C_PAD, HEAD_DIM), _bf16)
    for v in range(9):
        bm_ref[v] = bias_ref[0] + mask_ref[v]

    def body(t, carry):
        scores, values, dests = [], [], []
        for u in range(C_UNROLL * n_cb):
            rb = t * C_UNROLL + u // n_cb
            cb = u % n_cb
            rv = _edge_variant(rb, n_rb)
            cv = 0 if cb == 0 else (2 if cb == n_cb - 1 else 1)
            q_starts = [pl.multiple_of(((2 * rb + a) * SUBS_PER_ROW + 2 * cb) * SUB, 2 * SUB) for a in range(2)]
            q = jnp.concatenate([q_ref[0, pl.ds(qs, 2 * SUB), :] for qs in q_starts], axis=0)
            kparts, vparts = [], []
            for ar in range(C_NSEG):
                r4 = jnp.clip(2 * rb - 1 + ar, 0, n_r4 - 1)
                st = pl.multiple_of(C_PAD + (r4 * SUBS_PER_ROW + 2 * cb - 1) * SUB, SUB)
                kparts.append(kp_ref[pl.ds(st, C_KSEG), :])
                vparts.append(vp_ref[pl.ds(st, C_KSEG), :])
            scores.append(_qk(q, jnp.concatenate(kparts, axis=0)) + bm_ref[rv * 3 + cv])
            values.append(jnp.concatenate(vparts, axis=0))
            dests.append(q_starts)
        ms, accs = _softmax_pv(scores, values)
        for acc, q_starts in zip(accs, dests):
            o = (acc[:, :HEAD_DIM] / acc[:, HEAD_DIM:]).astype(o_ref.dtype)
            for a in range(2):
                o_ref[0, pl.ds(q_starts[a], 2 * SUB), :] = o[a * 2 * SUB:(a + 1) * 2 * SUB, :]
        return carry

    lax.fori_loop(0, n_rb // C_UNROLL, body, 0)


def _c_geometry():
    ql = jnp.arange(QBLK)
    qa, qc, qi, qj = ql // 64, (ql // 32) % 2, (ql // 8) % 4, ql % 8
    q_row = SUB_R * qa + qi
    q_col = SUB_C * qc + qj
    kl = jnp.arange(C_NKEYS)
    ka, kc, ki, kj = kl // C_KSEG, (kl // SUB) % 4, (kl // 8) % 4, kl % 8
    k_row = SUB_R * (ka - 1) + ki
    k_col = SUB_C * (kc - 1) + kj
    return q_row, q_col, k_row, k_col


def _c_bias(rpb):
    q_row, q_col, k_row, k_col = _c_geometry()
    nr, nc = 2 * NA_ROWS - 1, 2 * NA_COLS - 1
    dr = jnp.clip(k_row[None, :] - q_row[:, None] + NA_ROWS - 1, 0, nr - 1)
    dc = jnp.clip(k_col[None, :] - q_col[:, None] + NA_COLS - 1, 0, nc - 1)
    oh_r = (dr[:, :, None] == jnp.arange(nr)).astype(_f32)
    oh_c = (dc[:, :, None] == jnp.arange(nc)).astype(_f32)
    rows = jnp.einsum("hrc,qkr->hqkc", rpb.astype(_f32), oh_r, precision=lax.Precision.HIGHEST)
    return jnp.sum(rows * oh_c[None], axis=-1) * LOG2E


def _c_masks(rows):
    q_row, q_col, k_row, k_col = _c_geometry()
    out = []
    for rb in (0, 1, rows // NA_ROWS - 1):
        qr, kr = NA_ROWS * rb + q_row, NA_ROWS * rb + k_row
        rs = jnp.clip(qr - NA_ROWS // 2, 0, rows - NA_ROWS)
        rvalid = (kr[None, :] >= rs[:, None]) & (kr[None, :] < rs[:, None] + NA_ROWS) & (kr[None, :] >= 0) & (kr[None, :] < rows)
        for cb in (0, 1, GRID_W // NA_COLS - 1):
            qc, kc = NA_COLS * cb + q_col, NA_COLS * cb + k_col
            cs = jnp.clip(qc - NA_COLS // 2, 0, GRID_W - NA_COLS)
            cvalid = (kc[None, :] >= cs[:, None]) & (kc[None, :] < cs[:, None] + NA_COLS) & (kc[None, :] >= 0) & (kc[None, :] < GRID_W)
            out.append(jnp.where(rvalid & cvalid, 0.0, NEG_INF).astype(_f32))
    return jnp.stack(out)


def _attn_c(h, rpb):
    b, l, _ = h.shape
    bias = _c_bias(rpb)
    masks = _c_masks(l // GRID_W)
    return pl.pallas_call(
        _attn_c_kernel,
        out_shape=jax.ShapeDtypeStruct((b, l, MIX_W), _bf16),
        grid=(b, MIX_HEADS),
        in_specs=[pl.BlockSpec((1, l, HEAD_DIM), lambda bi, h: (bi, 0, h)),
                  pl.BlockSpec((1, l, HEAD_DIM), lambda bi, h: (bi, 0, MIX_HEADS + h)),
                  pl.BlockSpec((1, l, HEAD_DIM), lambda bi, h: (bi, 0, 2 * MIX_HEADS + h)),
                  pl.BlockSpec((1, QBLK, C_NKEYS), lambda bi, h: (h, 0, 0)),
                  pl.BlockSpec((9, QBLK, C_NKEYS), lambda bi, h: (0, 0, 0))],
        out_specs=pl.BlockSpec((1, l, HEAD_DIM), lambda bi, h: (bi, 0, h)),
        scratch_shapes=[pltpu.VMEM((l + 2 * C_PAD, HEAD_DIM), _bf16),
                        pltpu.VMEM((l + 2 * C_PAD, 2 * HEAD_DIM), _bf16),
                        pltpu.VMEM((9, QBLK, C_NKEYS), _f32)],
        compiler_params=_cparams(2),
    )(h, h, h, bias, masks)


def _subblock_chunks(tm):
    grp = SUB * SUBS_PER_ROW
    pairs = []
    for g in range(tm // grp):
        for c8 in range(SUBS_PER_ROW):
            for i4 in range(SUB_R):
                pairs.append((g * grp + i4 * GRID_W + c8 * SUB_C, (g * SUBS_PER_ROW + c8) * SUB + i4 * SUB_C))
    return pairs


def _out_kernel(om_ref, glo_ref, ghi_ref, mq_ref, kv_ref, x_ref, w_ref, g_ref, b_ref, o_ref, xs_ref, *, subblock):
    tm = om_ref.shape[1]
    half = D_INNER // 2

    def silu_gate(c0, c1):
        ref, off = (glo_ref, 0) if c0 < half else (ghi_ref, half)
        hg = 0.5 * ref[0, :, c0 - off:c1 - off].astype(_f32)
        return hg + hg * jnp.tanh(hg)

    ones = jnp.ones((kv_ref.shape[1], HEAD_DIM), _bf16)
    scores = [_qk(mq_ref[0, :, h * HEAD_DIM:(h + 1) * HEAD_DIM], kv_ref[0, :, h * HEAD_DIM:(h + 1) * HEAD_DIM]) * QSCALE
              for h in range(MEM_HEADS)]
    values = [jnp.concatenate([kv_ref[0, :, MEM_W + h * HEAD_DIM:MEM_W + (h + 1) * HEAD_DIM], ones], axis=1)
              for h in range(MEM_HEADS)]
    _, accs = _softmax_pv(scores, values)
    branch = None
    for c0 in range(0, D_INNER, MEM_W):
        if c0 < MIX_W:
            y = om_ref[0, :, c0:c0 + MEM_W].astype(_f32) * silu_gate(c0, c0 + MEM_W)
        else:
            y = jnp.concatenate([acc[:, :HEAD_DIM] / acc[:, HEAD_DIM:] for acc in accs], axis=1) * silu_gate(c0, c0 + MEM_W)
        part = jnp.dot(y.astype(_bf16), w_ref[c0:c0 + MEM_W, :], preferred_element_type=_f32)
        branch = part if branch is None else branch + part
    if subblock:
        for nat, sub in _subblock_chunks(tm):
            xs_ref[sub:sub + SUB_C, :] = x_ref[0, nat:nat + SUB_C, :]
        xin = xs_ref[...]
    else:
        xin = x_ref[0]
    z = ALPHA * xin + branch
    mu = jnp.mean(z, axis=-1, keepdims=True)
    zc = z - mu
    var = jnp.mean(zc * zc, axis=-1, keepdims=True)
    out = zc * lax.rsqrt(var + LN_EPS) * g_ref[...] + b_ref[...]
    if subblock:
        xs_ref[...] = out
        for nat, sub in _subblock_chunks(tm):
            o_ref[0, nat:nat + SUB_C, :] = xs_ref[sub:sub + SUB_C, :]
    else:
        o_ref[0] = out


def _out(o_mix, h, mq_col, kv, x, w_out, ln_g, ln_b, *, subblock):
    b, l, d = x.shape
    tm = OUT_TM
    mlen = kv.shape[1]
    half = D_INNER // 2
    gate_col = mq_col + MEM_W
    assert mq_col % MEM_W == 0 and gate_col % half == 0
    kern = functools.partial(_out_kernel, subblock=subblock)
    return pl.pallas_call(
        kern,
        out_shape=jax.ShapeDtypeStruct((b, l, d), _f32),
        grid=(b, l // tm),
        in_specs=[pl.BlockSpec((1, tm, MIX_W), lambda bi, i: (bi, i, 0)),
                  pl.BlockSpec((1, tm, half), lambda bi, i: (bi, i, gate_col // half)),
                  pl.BlockSpec((1, tm, half), lambda bi, i: (bi, i, gate_col // half + 1)),
                  pl.BlockSpec((1, tm, MEM_W), lambda bi, i: (bi, i, mq_col // MEM_W)),
                  pl.BlockSpec((1, mlen, 2 * MEM_W), lambda bi, i: (bi, 0, 0)),
                  pl.BlockSpec((1, tm, d), lambda bi, i: (bi, i, 0)),
                  pl.BlockSpec((D_INNER, d), lambda bi, i: (0, 0)),
                  pl.BlockSpec((1, d), lambda bi, i: (0, 0)),
                  pl.BlockSpec((1, d), lambda bi, i: (0, 0))],
        out_specs=pl.BlockSpec((1, tm, d), lambda bi, i: (bi, i, 0)),
        scratch_shapes=[pltpu.VMEM((tm, d), _f32)],
        compiler_params=_cparams(2),
    )(o_mix, h, h, h, kv, x, w_out, ln_g.reshape(1, d), ln_b.reshape(1, d))


def _split_cols(w, sizes):
    out, c = [], 0
    for s in sizes:
        out.append(w[:, c:c + s])
        c += s
    return out


def kernel(x, mem, w_in_a, sink_a, w_in_b, w_in_c, rpb_c, w_mkv, w_out, ln_g, ln_b):
    b, l, d = x.shape
    mlen = mem.shape[1]
    kv_all = _memkv(mem.reshape(b * mlen, d), w_mkv.astype(_bf16)).reshape(DEPTH, b, mlen, 2 * MEM_W)
    tables = _rope_tables(l)

    for i in range(DEPTH):
        kind, j = i % NUM_MIXERS, i // NUM_MIXERS
        tail = MEM_W + D_INNER
        if kind == 0:
            seg = ((1, ((MIX_W, "rope_q"), (MEM_W, "rope_k"), (MEM_W + tail, "plain"))),)
            h, = _proj(x, w_in_a[j].astype(_bf16), tables, seg)
            o_mix = _attn_a(h, sink_a[j])
            mq_col = MIX_W + 2 * MEM_W
        elif kind == 1:
            wq0, wq1, wq2, wrest = _split_cols(w_in_b[j].astype(_bf16), (MIX_W, MIX_W, MIX_W, 2 * MIX_W + tail))
            seg = ((1, ((MIX_W, "rope_q"), (MIX_W, "rope_k"), (MIX_W + tail, "plain"))),
                   (B_DILATIONS[1], ((MIX_W, "rope_q"),)),
                   (B_DILATIONS[2], ((MIX_W, "rope_q"),)))
            h, hq1, hq2 = _proj(x, jnp.concatenate([wq0, wrest, wq1, wq2], axis=1), tables, seg)
            o_mix = _attn_b(h, hq1, hq2)
            mq_col = 3 * MIX_W
        else:
            seg = ((1, ((MIX_W, "scale"), (2 * MIX_W + tail, "plain"))),)
            h, = _proj(x, w_in_c[j].astype(_bf16), tables, seg, subblock=True)
            o_mix = _attn_c(h, rpb_c[j])
            mq_col = 3 * MIX_W
        x = _out(o_mix, h, mq_col, kv_all[i], x, w_out[i].astype(_bf16), ln_g[i], ln_b[i], subblock=(kind == 2))
    return x
```

```python
import functools
import math

import jax
import jax.numpy as jnp
from jax import lax
from jax.experimental import pallas as pl
from jax.experimental.pallas import tpu as pltpu

D_MODEL = 1024
DEPTH = 4
NUM_MIXERS = 3
HEAD_DIM = 128
D_INNER = 2 * D_MODEL
MEM_HEADS = 4
MIX_HEADS = D_INNER // HEAD_DIM - MEM_HEADS
A_KV_HEADS = MIX_HEADS // 3
A_GROUP = MIX_HEADS // A_KV_HEADS
A_WINDOW = 128
B_DILATIONS = (1, 4, 16)
B_HALF_WINDOW = 64
NA_ROWS = 8
NA_COLS = 16
GRID_W = 64
ROPE_THETA = 500000.0
ROPE_DIMS = HEAD_DIM // 4
ROPE_HALF = ROPE_DIMS // 2
LN_EPS = 1e-5
ALPHA = (2 * DEPTH) ** 0.25
NEG_INF = -1e30
LOG2E = math.log2(math.e)
QSCALE = HEAD_DIM ** -0.5 * LOG2E

MIX_W = MIX_HEADS * HEAD_DIM
MEM_W = MEM_HEADS * HEAD_DIM

SUB_R = 4
SUB_C = 8
SUB = SUB_R * SUB_C
SUBS_PER_ROW = GRID_W // SUB_C
QBLK = 128

VMEM_LIMIT = 56 * 1024 * 1024
PROJ_TM = 512
PROJ_TN = 512
OUT_TM = 1024
OUT_SUB = 512

_f32 = jnp.float32
_bf16 = jnp.bfloat16


def _cparams(n_grid):
    return pltpu.CompilerParams(dimension_semantics=("arbitrary",) * n_grid,
                                vmem_limit_bytes=VMEM_LIMIT)


def _rope(blk, cos, sin):
    lane = lax.broadcasted_iota(jnp.int32, blk.shape, 1)
    up = pltpu.roll(blk, HEAD_DIM - ROPE_HALF, 1)
    down = pltpu.roll(blk, ROPE_HALF, 1)
    swapped = jnp.where(lane < ROPE_HALF, up, down)
    return blk * cos + swapped * sin


def _proj_kernel(*refs, subblock, segments, has_tables):
    x_ref, w_ref = refs[:2]
    pos = 2
    if has_tables:
        cos_ref, sin_ref = refs[2:4]
        pos = 4
    out_refs = refs[pos:pos + len(segments)]
    xs_ref, tmp_ref = refs[pos + len(segments):]
    tm = xs_ref.shape[0]
    tn = PROJ_TN

    if subblock:
        grp = SUB * SUBS_PER_ROW
        for g in range(tm // grp):
            for c8 in range(SUBS_PER_ROW):
                parts = [x_ref[0, g * grp + i4 * GRID_W + c8 * SUB_C:
                               g * grp + i4 * GRID_W + (c8 + 1) * SUB_C, :] for i4 in range(SUB_R)]
                dst = (g * SUBS_PER_ROW + c8) * SUB
                xs_ref[dst:dst + SUB, :] = jnp.concatenate(parts, axis=0).astype(_bf16)
    else:
        xs_ref[...] = x_ref[0].astype(_bf16)

    def write(o_ref, dil, val, c0):
        c1 = c0 + HEAD_DIM
        if dil == 1:
            o_ref[0, :, c0:c1] = val.astype(o_ref.dtype)
            return
        n = tm // dil
        tmp_ref[0] = val
        if dil == B_DILATIONS[2]:
            d1 = B_DILATIONS[1]
            n1 = tm // d1
            for r in range(d1):
                tmp_ref[1, r * n1:(r + 1) * n1, :] = tmp_ref[0, pl.ds(r, n1, stride=d1), :]
            for r in range(dil):
                rows = tmp_ref[1, pl.ds((r % d1) * n1 + r // d1, n, stride=d1), :]
                o_ref[0, r, :, c0:c1] = rows.astype(o_ref.dtype)
        else:
            for r in range(dil):
                o_ref[0, r, :, c0:c1] = tmp_ref[0, pl.ds(r, n, stride=dil), :].astype(o_ref.dtype)

    wcol = 0
    for o_ref, (dil, parts) in zip(out_refs, segments):
        ocol = 0
        for n_cols, mode in parts:
            for _ in range(n_cols // tn):
                acc = jnp.dot(xs_ref[...], w_ref[:, wcol:wcol + tn], preferred_element_type=_f32)
                for hd in range(tn // HEAD_DIM):
                    val = acc[:, hd * HEAD_DIM:(hd + 1) * HEAD_DIM]
                    if mode == "rope_q":
                        val = _rope(val, cos_ref[0], sin_ref[0])
                    elif mode == "rope_k":
                        val = _rope(val, cos_ref[1], sin_ref[1])
                    elif mode == "scale":
                        val = val * QSCALE
                    write(o_ref, dil, val, ocol + hd * HEAD_DIM)
                wcol += tn
                ocol += tn


def _proj(x, w, tables, segments, *, subblock=False):
    b, l, d = x.shape
    tm, tn = PROJ_TM, PROJ_TN
    tiles = l // tm
    assert l % tm == 0
    widths = [sum(n for n, _ in parts) for _, parts in segments]
    assert sum(widths) == w.shape[1] and all(n % tn == 0 for _, parts in segments for n, _ in parts)
    out_shapes, out_specs = [], []
    for (dil, _), n in zip(segments, widths):
        if dil == 1:
            out_shapes.append(jax.ShapeDtypeStruct((b, l, n), _bf16))
            out_specs.append(pl.BlockSpec((1, tm, n), lambda i: (i // tiles, i % tiles, 0)))
        else:
            out_shapes.append(jax.ShapeDtypeStruct((b, dil, l // dil, n), _bf16))
            out_specs.append(pl.BlockSpec((1, dil, tm // dil, n), lambda i: (i // tiles, 0, i % tiles, 0)))
    in_specs = [pl.BlockSpec((1, tm, d), lambda i: (i // tiles, i % tiles, 0)),
                pl.BlockSpec((d, w.shape[1]), lambda i: (0, 0), pipeline_mode=pl.Buffered(1))]
    args = [x, w]
    has_tables = any(mode.startswith("rope") for _, parts in segments for _, mode in parts)
    if has_tables:
        tab_spec = pl.BlockSpec((2, tm, HEAD_DIM), lambda i: (0, i % tiles, 0))
        in_specs += [tab_spec, tab_spec]
        args += list(tables)
    kern = functools.partial(_proj_kernel, subblock=subblock, segments=segments, has_tables=has_tables)
    return pl.pallas_call(
        kern,
        out_shape=out_shapes,
        grid=(b * tiles,),
        in_specs=in_specs,
        out_specs=out_specs,
        scratch_shapes=[pltpu.VMEM((tm, d), _bf16), pltpu.VMEM((2, tm, HEAD_DIM), _f32)],
        compiler_params=_cparams(1),
    )(*args)


def _rope_tables(l):
    inv = ROPE_THETA ** (-jnp.arange(ROPE_HALF, dtype=_f32) / ROPE_HALF)
    ang = jnp.arange(l).astype(_f32)[:, None] * inv[None, :]
    cos, sin = jnp.cos(ang), jnp.sin(ang)
    pad = HEAD_DIM - ROPE_DIMS
    cos_t = jnp.concatenate([cos, cos, jnp.ones((l, pad), _f32)], axis=1)
    sin_t = jnp.concatenate([-sin, sin, jnp.zeros((l, pad), _f32)], axis=1)
    return jnp.stack([cos_t * QSCALE, cos_t]), jnp.stack([sin_t * QSCALE, sin_t])


def _memkv_kernel(m_ref, w_ref, o_ref):
    o_ref[0] = jnp.dot(m_ref[...].astype(_bf16), w_ref[0], preferred_element_type=_f32).astype(o_ref.dtype)


def _memkv(mem2d, w_mkv):
    rows, d = mem2d.shape
    depth, _, n = w_mkv.shape
    tm = min(rows, PROJ_TM)
    assert rows % tm == 0
    return pl.pallas_call(
        _memkv_kernel,
        out_shape=jax.ShapeDtypeStruct((depth, rows, n), _bf16),
        grid=(depth, rows // tm),
        in_specs=[pl.BlockSpec((tm, d), lambda li, i: (i, 0)),
                  pl.BlockSpec((1, d, n), lambda li, i: (li, 0, 0))],
        out_specs=pl.BlockSpec((1, tm, n), lambda li, i: (li, i, 0)),
        compiler_params=_cparams(2),
    )(mem2d, w_mkv)


def _qk(q, k):
    return lax.dot_general(q, k, (((1,), (1,)), ((), ())), preferred_element_type=_f32)


def _band_masks(mask_ref, nk, half, deltas):
    rel = (lax.broadcasted_iota(jnp.int32, (QBLK, nk), 1) - lax.broadcasted_iota(jnp.int32, (QBLK, nk), 0))
    for v, delta in enumerate(deltas):
        mask_ref[v] = jnp.where(jnp.abs(rel + delta) <= half, 0.0, NEG_INF).astype(_f32)


def _edge_variant(i, n):
    return jnp.where(i == 0, 0, jnp.where(i == n - 1, 2, 1))


def _softmax_pv(scores, values, floors=None):
    def rowmax(s):
        m = s[:, 0:HEAD_DIM]
        for c in range(1, s.shape[1] // HEAD_DIM):
            m = jnp.maximum(m, s[:, c * HEAD_DIM:(c + 1) * HEAD_DIM])
        return jnp.max(m, axis=-1, keepdims=True)

    ms = [rowmax(s) for s in scores]
    if floors is not None:
        ms = [jnp.maximum(m, f) for m, f in zip(ms, floors)]
    ps = [jnp.exp2(s - m).astype(_bf16) for s, m in zip(scores, ms)]
    accs = [jnp.dot(p, v, preferred_element_type=_f32) for p, v in zip(ps, values)]
    return ms, accs


A_UNROLL = 4


def _attn_a_kernel(q_ref, k_ref, v_ref, sink_ref, o_ref, v1_ref, mask_ref, es_ref):
    l = k_ref.shape[1]
    nk = 3 * A_WINDOW
    nblk = l // QBLK
    v1_ref[:, 0:HEAD_DIM] = v_ref[0]
    v1_ref[:, HEAD_DIM:2 * HEAD_DIM] = jnp.ones((l, HEAD_DIM), _bf16)
    rel = (lax.broadcasted_iota(jnp.int32, (QBLK, nk), 1) - lax.broadcasted_iota(jnp.int32, (QBLK, nk), 0))
    for v, delta in enumerate((0, -A_WINDOW, -2 * A_WINDOW)):
        blk = jnp.where(jnp.abs(rel + delta) <= A_WINDOW, 0.0, NEG_INF).astype(_f32)
        for a in range(A_GROUP):
            mask_ref[v, a * QBLK:(a + 1) * QBLK, :] = blk
    for a in range(A_GROUP):
        es_ref[a * QBLK:(a + 1) * QBLK, :] = jnp.broadcast_to(sink_ref[0, a:a + 1, :] * LOG2E, (QBLK, HEAD_DIM))

    def body(t, carry):
        scores, windows = [], []
        for u in range(A_UNROLL):
            i = t * A_UNROLL + u
            q0 = pl.multiple_of(i * QBLK, QBLK)
            ks = pl.multiple_of(jnp.clip(q0 - A_WINDOW, 0, l - nk), QBLK)
            q = jnp.concatenate([q_ref[0, pl.ds(q0, QBLK), a * HEAD_DIM:(a + 1) * HEAD_DIM]
                                 for a in range(A_GROUP)], axis=0)
            scores.append(_qk(q, k_ref[0, pl.ds(ks, nk), :]) + mask_ref[_edge_variant(i, nblk)])
            windows.append((q0, ks))
        cols = range(nk // HEAD_DIM)
        ms = []
        for s in scores:
            m = s[:, 0:HEAD_DIM]
            for c in cols[1:]:
                m = jnp.maximum(m, s[:, c * HEAD_DIM:(c + 1) * HEAD_DIM])
            m = jnp.broadcast_to(jnp.max(m, axis=-1, keepdims=True), m.shape)
            ms.append(jnp.maximum(m, es_ref[...]))
        ps = [jnp.concatenate([jnp.exp2(s[:, c * HEAD_DIM:(c + 1) * HEAD_DIM] - m) for c in cols], axis=1).astype(_bf16)
              for s, m in zip(scores, ms)]
        accs = [jnp.dot(p, v1_ref[pl.ds(ks, nk), :], preferred_element_type=_f32) for p, (_, ks) in zip(ps, windows)]
        for m, acc, (q0, _) in zip(ms, accs, windows):
            z = acc[:, HEAD_DIM:] + jnp.exp2(es_ref[...] - m)
            o = (acc[:, :HEAD_DIM] / z).astype(o_ref.dtype)
            for a in range(A_GROUP):
                o_ref[0, pl.ds(q0, QBLK), a * HEAD_DIM:(a + 1) * HEAD_DIM] = o[a * QBLK:(a + 1) * QBLK, :]
        return carry

    lax.fori_loop(0, nblk // A_UNROLL, body, 0)


def _attn_a(h, sink):
    b, l, _ = h.shape
    gw = A_GROUP * HEAD_DIM
    sink_b = jnp.broadcast_to(sink.astype(_f32).reshape(A_KV_HEADS, A_GROUP, 1), (A_KV_HEADS, A_GROUP, HEAD_DIM))
    sink_b = jnp.pad(sink_b, ((0, 0), (0, 8 - A_GROUP), (0, 0)))
    return pl.pallas_call(
        _attn_a_kernel,
        out_shape=jax.ShapeDtypeStruct((b, l, MIX_W), _bf16),
        grid=(b, A_KV_HEADS),
        in_specs=[pl.BlockSpec((1, l, gw), lambda bi, g: (bi, 0, g)),
                  pl.BlockSpec((1, l, HEAD_DIM), lambda bi, g: (bi, 0, MIX_HEADS + g)),
                  pl.BlockSpec((1, l, HEAD_DIM), lambda bi, g: (bi, 0, MIX_HEADS + A_KV_HEADS + g)),
                  pl.BlockSpec((1, 8, HEAD_DIM), lambda bi, g: (g, 0, 0))],
        out_specs=pl.BlockSpec((1, l, gw), lambda bi, g: (bi, 0, g)),
        scratch_shapes=[pltpu.VMEM((l, 2 * HEAD_DIM), _bf16),
                        pltpu.VMEM((3, A_GROUP * QBLK, 3 * A_WINDOW), _f32),
                        pltpu.VMEM((A_GROUP * QBLK, HEAD_DIM), _f32)],
        compiler_params=_cparams(2),
    )(h, h, h, sink_b)


B_UNROLL = 16


def _attn_b_kernel(q0_ref, k_ref, v_ref, q1_ref, q2_ref, o_ref,
                   kd_ref, v1_ref, tmp_ref, tmp2_ref, og_ref, lse_ref, mask_ref):
    l = o_ref.shape[1]
    nk = 4 * B_HALF_WINDOW
    q_refs = (q0_ref, q1_ref, q2_ref)
    _band_masks(mask_ref, nk, B_HALF_WINDOW, (0, -B_HALF_WINDOW, -2 * B_HALF_WINDOW))

    v1_ref[0, :, 0:HEAD_DIM] = v_ref[0]
    for g in range(3):
        v1_ref[g, :, HEAD_DIM:2 * HEAD_DIM] = jnp.ones((l, HEAD_DIM), _bf16)
    d1 = B_DILATIONS[1]
    assert B_DILATIONS == (1, d1, d1 * d1)
    n1, n2 = l // d1, l // (d1 * d1)
    for src, is_key in ((k_ref, True), (v_ref, False)):
        def put(g, row0, n, rows):
            if is_key:
                kd_ref[g - 1, row0:row0 + n, :] = rows.astype(_bf16)
            else:
                v1_ref[g, row0:row0 + n, 0:HEAD_DIM] = rows.astype(_bf16)

        tmp_ref[...] = src[0].astype(_f32)
        for r in range(d1):
            rows = tmp_ref[pl.ds(r, n1, stride=d1), :]
            tmp2_ref[r * n1:(r + 1) * n1, :] = rows
            put(1, r * n1, n1, rows)
        for r in range(d1 * d1):
            put(2, r * n2, n2, tmp2_ref[pl.ds((r % d1) * n1 + r // d1, n2, stride=d1), :])

    for g, dil in reversed(list(enumerate(B_DILATIONS))):
        n = l // dil
        blocks = n // QBLK

        def body(t, carry, g=g, dil=dil, n=n, blocks=blocks):
            scores, values, dests = [], [], []
            for u in range(B_UNROLL):
                bt = t * B_UNROLL + u
                r = bt // blocks
                i = bt % blocks
                q0 = pl.multiple_of(i * QBLK, QBLK)
                ks = pl.multiple_of(jnp.clip(q0 - B_HALF_WINDOW, 0, n - nk), B_HALF_WINDOW)
                base = pl.multiple_of(r * n + ks, B_HALF_WINDOW)
                if g == 0:
                    q = q0_ref[0, pl.ds(q0, QBLK), :]
                    kw = k_ref[0, pl.ds(ks, nk), :]
                else:
                    q = q_refs[g][0, r, pl.ds(q0, QBLK), :]
                    kw = kd_ref[g - 1, pl.ds(base, nk), :]
                scores.append(_qk(q, kw) + mask_ref[_edge_variant(i, blocks)])
                values.append(v1_ref[g, pl.ds(base, nk), :])
                dests.append((q0, r))
            ms, accs = _softmax_pv(scores, values)
            for m, acc, (q0, r) in zip(ms, accs, dests):
                z = acc[:, HEAD_DIM:]
                o = acc[:, :HEAD_DIM] / z
                lse = m + jnp.log2(z)
                if dil > 1:
                    og_ref[g - 1, pl.ds(q0 * dil + r, QBLK, stride=dil), :] = o
                    lse_ref[g - 1, pl.ds(q0 * dil + r, QBLK, stride=dil), :] = lse
                else:
                    ls = [lse] + [lse_ref[j, pl.ds(q0, QBLK), :] for j in range(2)]
                    top = jnp.maximum(jnp.maximum(ls[0], ls[1]), ls[2])
                    ws = [jnp.exp2(x - top) for x in ls]
                    num = ws[0] * o + ws[1] * og_ref[0, pl.ds(q0, QBLK), :] + ws[2] * og_ref[1, pl.ds(q0, QBLK), :]
                    o_ref[0, pl.ds(q0, QBLK), :] = (num / (ws[0] + ws[1] + ws[2])).astype(o_ref.dtype)
            return carry

        lax.fori_loop(0, dil * blocks // B_UNROLL, body, 0)


def _attn_b(h0, hq1, hq2):
    b, l, _ = h0.shape
    in_specs = [pl.BlockSpec((1, l, HEAD_DIM), lambda bi, h: (bi, 0, h)),
                pl.BlockSpec((1, l, HEAD_DIM), lambda bi, h: (bi, 0, MIX_HEADS + h)),
                pl.BlockSpec((1, l, HEAD_DIM), lambda bi, h: (bi, 0, 2 * MIX_HEADS + h))]
    for dil in B_DILATIONS[1:]:
        in_specs.append(pl.BlockSpec((1, dil, l // dil, HEAD_DIM), lambda bi, h: (bi, 0, 0, h)))
    return pl.pallas_call(
        _attn_b_kernel,
        out_shape=jax.ShapeDtypeStruct((b, l, MIX_W), _bf16),
        grid=(b, MIX_HEADS),
        in_specs=in_specs,
        out_specs=pl.BlockSpec((1, l, HEAD_DIM), lambda bi, h: (bi, 0, h)),
        scratch_shapes=[pltpu.VMEM((2, l, HEAD_DIM), _bf16),
                        pltpu.VMEM((3, l, 2 * HEAD_DIM), _bf16),
                        pltpu.VMEM((l, HEAD_DIM), _f32),
                        pltpu.VMEM((l, HEAD_DIM), _f32),
                        pltpu.VMEM((2, l, HEAD_DIM), _f32),
                        pltpu.VMEM((2, l, HEAD_DIM), _f32),
                        pltpu.VMEM((3, QBLK, 4 * B_HALF_WINDOW), _f32)],
        compiler_params=_cparams(2),
    )(h0, h0, h0, hq1, hq2)


C_KSEG = 4 * SUB
C_NSEG = 4
C_PAD = SUB
C_NKEYS = C_NSEG * C_KSEG
C_UNROLL = 4


def _attn_c_kernel(q_ref, k_ref, v_ref, bias_ref, mask_ref, o_ref, kp_ref, vp_ref, bm_ref):
    l = k_ref.shape[1]
    n_rb = l // GRID_W // NA_ROWS
    n_cb = GRID_W // NA_COLS
    n_r4 = l // GRID_W // SUB_R
    kp_ref[0:C_PAD, :] = jnp.zeros((C_PAD, HEAD_DIM), _bf16)
    kp_ref[C_PAD + l:C_PAD + l + C_PAD, :] = jnp.zeros((C_PAD, HEAD_DIM), _bf16)
    kp_ref[C_PAD:C_PAD + l, :] = k_ref[0]
    vp_ref[0:C_PAD, 0:HEAD_DIM] = jnp.zeros((C_PAD, HEAD_DIM), _bf16)
    vp_ref[C_PAD + l:C_PAD + l + C_PAD, 0:HEAD_DIM] = jnp.zeros((C_PAD, HEAD_DIM), _bf16)
    vp_ref[C_PAD:C_PAD + l, 0:HEAD_DIM] = v_ref[0]
    vp_ref[:, HEAD_DIM:2 * HEAD_DIM] = jnp.ones((l + 2 * C_PAD, HEAD_DIM), _bf16)
    for v in range(9):
        bm_ref[v] = bias_ref[0] + mask_ref[v]

    def body(t, carry):
        scores, values, dests = [], [], []
        for u in range(C_UNROLL * n_cb):
            rb = t * C_UNROLL + u // n_cb
            cb = u % n_cb
            rv = _edge_variant(rb, n_rb)
            cv = 0 if cb == 0 else (2 if cb == n_cb - 1 else 1)
            q_starts = [pl.multiple_of(((2 * rb + a) * SUBS_PER_ROW + 2 * cb) * SUB, 2 * SUB) for a in range(2)]
            q = jnp.concatenate([q_ref[0, pl.ds(qs, 2 * SUB), :] for qs in q_starts], axis=0)
            kparts, vparts = [], []
            for ar in range(C_NSEG):
                r4 = jnp.clip(2 * rb - 1 + ar, 0, n_r4 - 1)
                st = pl.multiple_of(C_PAD + (r4 * SUBS_PER_ROW + 2 * cb - 1) * SUB, SUB)
                kparts.append(kp_ref[pl.ds(st, C_KSEG), :])
                vparts.append(vp_ref[pl.ds(st, C_KSEG), :])
            scores.append(_qk(q, jnp.concatenate(kparts, axis=0)) + bm_ref[rv * 3 + cv])
            values.append(jnp.concatenate(vparts, axis=0))
            dests.append(q_starts)
        ms, accs = _softmax_pv(scores, values)
        for acc, q_starts in zip(accs, dests):
            o = (acc[:, :HEAD_DIM] / acc[:, HEAD_DIM:]).astype(o_ref.dtype)
            for a in range(2):
                o_ref[0, pl.ds(q_starts[a], 2 * SUB), :] = o[a * 2 * SUB:(a + 1) * 2 * SUB, :]
        return carry

    lax.fori_loop(0, n_rb // C_UNROLL, body, 0)


def _c_geometry():
    ql = jnp.arange(QBLK)
    qa, qc, qi, qj = ql // 64, (ql // 32) % 2, (ql // 8) % 4, ql % 8
    q_row = SUB_R * qa + qi
    q_col = SUB_C * qc + qj
    kl = jnp.arange(C_NKEYS)
    ka, kc, ki, kj = kl // C_KSEG, (kl // SUB) % 4, (kl // 8) % 4, kl % 8
    k_row = SUB_R * (ka - 1) + ki
    k_col = SUB_C * (kc - 1) + kj
    return q_row, q_col, k_row, k_col


def _c_bias(rpb):
    q_row, q_col, k_row, k_col = _c_geometry()
    nr, nc = 2 * NA_ROWS - 1, 2 * NA_COLS - 1
    dr = jnp.clip(k_row[None, :] - q_row[:, None] + NA_ROWS - 1, 0, nr - 1)
    dc = jnp.clip(k_col[None, :] - q_col[:, None] + NA_COLS - 1, 0, nc - 1)
    oh_r = (dr[:, :, None] == jnp.arange(nr)).astype(_f32)
    oh_c = (dc[:, :, None] == jnp.arange(nc)).astype(_f32)
    rows = jnp.einsum("hrc,qkr->hqkc", rpb.astype(_f32), oh_r, precision=lax.Precision.HIGHEST)
    return jnp.sum(rows * oh_c[None], axis=-1) * LOG2E


def _c_masks(rows):
    q_row, q_col, k_row, k_col = _c_geometry()
    out = []
    for rb in (0, 1, rows // NA_ROWS - 1):
        qr, kr = NA_ROWS * rb + q_row, NA_ROWS * rb + k_row
        rs = jnp.clip(qr - NA_ROWS // 2, 0, rows - NA_ROWS)
        rvalid = (kr[None, :] >= rs[:, None]) & (kr[None, :] < rs[:, None] + NA_ROWS) & (kr[None, :] >= 0) & (kr[None, :] < rows)
        for cb in (0, 1, GRID_W // NA_COLS - 1):
            qc, kc = NA_COLS * cb + q_col, NA_COLS * cb + k_col
            cs = jnp.clip(qc - NA_COLS // 2, 0, GRID_W - NA_COLS)
            cvalid = (kc[None, :] >= cs[:, None]) & (kc[None, :] < cs[:, None] + NA_COLS) & (kc[None, :] >= 0) & (kc[None, :] < GRID_W)
            out.append(jnp.where(rvalid & cvalid, 0.0, NEG_INF).astype(_f32))
    return jnp.stack(out)


def _attn_c(h, rpb):
    b, l, _ = h.shape
    bias = _c_bias(rpb)
    masks = _c_masks(l // GRID_W)
    return pl.pallas_call(
        _attn_c_kernel,
        out_shape=jax.ShapeDtypeStruct((b, l, MIX_W), _bf16),
        grid=(b, MIX_HEADS),
        in_specs=[pl.BlockSpec((1, l, HEAD_DIM), lambda bi, h: (bi, 0, h)),
                  pl.BlockSpec((1, l, HEAD_DIM), lambda bi, h: (bi, 0, MIX_HEADS + h)),
                  pl.BlockSpec((1, l, HEAD_DIM), lambda bi, h: (bi, 0, 2 * MIX_HEADS + h)),
                  pl.BlockSpec((1, QBLK, C_NKEYS), lambda bi, h: (h, 0, 0)),
                  pl.BlockSpec((9, QBLK, C_NKEYS), lambda bi, h: (0, 0, 0))],
        out_specs=pl.BlockSpec((1, l, HEAD_DIM), lambda bi, h: (bi, 0, h)),
        scratch_shapes=[pltpu.VMEM((l + 2 * C_PAD, HEAD_DIM), _bf16),
                        pltpu.VMEM((l + 2 * C_PAD, 2 * HEAD_DIM), _bf16),
                        pltpu.VMEM((9, QBLK, C_NKEYS), _f32)],
        compiler_params=_cparams(2),
    )(h, h, h, bias, masks)


def _subblock_chunks(tm):
    grp = SUB * SUBS_PER_ROW
    pairs = []
    for g in range(tm // grp):
        for c8 in range(SUBS_PER_ROW):
            for i4 in range(SUB_R):
                pairs.append((g * grp + i4 * GRID_W + c8 * SUB_C, (g * SUBS_PER_ROW + c8) * SUB + i4 * SUB_C))
    return pairs


def _out_kernel(om_ref, glo_ref, ghi_ref, mq_ref, kv_ref, x_ref, w_ref, g_ref, b_ref, o_ref, xs_ref, *, subblock):
    half = D_INNER // 2
    ones = jnp.ones((kv_ref.shape[1], HEAD_DIM), _bf16)
    values = [jnp.concatenate([kv_ref[0, :, MEM_W + h * HEAD_DIM:MEM_W + (h + 1) * HEAD_DIM], ones], axis=1)
              for h in range(MEM_HEADS)]

    n_slices = D_INNER // MEM_W
    part_rows = OUT_SUB // n_slices

    def layer_norm_part(r0, q, z_in):
        lo = q * part_rows
        xin = (xs_ref if subblock else x_ref.at[0])[r0 + lo:r0 + lo + part_rows, :]
        z = ALPHA * xin + z_in[lo:lo + part_rows, :]
        mu = jnp.mean(z, axis=-1, keepdims=True)
        zc = z - mu
        var = jnp.mean(zc * zc, axis=-1, keepdims=True)
        out = zc * lax.rsqrt(var + LN_EPS) * g_ref[...] + b_ref[...]
        if subblock:
            for nat, sub in _subblock_chunks(OUT_SUB):
                if lo <= sub < lo + part_rows:
                    o_ref[0, r0 + nat:r0 + nat + SUB_C, :] = out[sub - lo:sub - lo + SUB_C, :]
        else:
            o_ref[0, r0 + lo:r0 + lo + part_rows, :] = out

    pending = None
    for r0 in range(0, om_ref.shape[1], OUT_SUB):
        rows = slice(r0, r0 + OUT_SUB)
        if subblock:
            for nat, sub in _subblock_chunks(OUT_SUB):
                xs_ref[r0 + sub:r0 + sub + SUB_C, :] = x_ref[0, r0 + nat:r0 + nat + SUB_C, :]

        def silu_gate(c0, c1, rows=rows):
            ref, off = (glo_ref, 0) if c0 < half else (ghi_ref, half)
            hg = 0.5 * ref[0, rows, c0 - off:c1 - off].astype(_f32)
            return hg + hg * jnp.tanh(hg)

        scores = [_qk(mq_ref[0, rows, h * HEAD_DIM:(h + 1) * HEAD_DIM],
                      kv_ref[0, :, h * HEAD_DIM:(h + 1) * HEAD_DIM]) * QSCALE for h in range(MEM_HEADS)]
        _, accs = _softmax_pv(scores, values)
        branch = None
        for q, c0 in enumerate(range(0, D_INNER, MEM_W)):
            if c0 < MIX_W:
                y = om_ref[0, rows, c0:c0 + MEM_W].astype(_f32) * silu_gate(c0, c0 + MEM_W)
            else:
                y = jnp.concatenate([acc[:, :HEAD_DIM] / acc[:, HEAD_DIM:] for acc in accs], axis=1)
                y = y * silu_gate(c0, c0 + MEM_W)
            part = jnp.dot(y.astype(_bf16), w_ref[c0:c0 + MEM_W, :], preferred_element_type=_f32)
            branch = part if branch is None else branch + part
            if pending is not None:
                layer_norm_part(pending[0], q, pending[1])
        pending = (r0, branch)
    for q in range(n_slices):
        layer_norm_part(pending[0], q, pending[1])


def _out(o_mix, h, mq_col, kv, x, w_out, ln_g, ln_b, *, subblock):
    b, l, d = x.shape
    tm = OUT_TM
    mlen = kv.shape[1]
    half = D_INNER // 2
    gate_col = mq_col + MEM_W
    assert mq_col % MEM_W == 0 and gate_col % half == 0
    kern = functools.partial(_out_kernel, subblock=subblock)
    return pl.pallas_call(
        kern,
        out_shape=jax.ShapeDtypeStruct((b, l, d), _f32),
        grid=(b, l // tm),
        in_specs=[pl.BlockSpec((1, tm, MIX_W), lambda bi, i: (bi, i, 0)),
                  pl.BlockSpec((1, tm, half), lambda bi, i: (bi, i, gate_col // half)),
                  pl.BlockSpec((1, tm, half), lambda bi, i: (bi, i, gate_col // half + 1)),
                  pl.BlockSpec((1, tm, MEM_W), lambda bi, i: (bi, i, mq_col // MEM_W)),
                  pl.BlockSpec((1, mlen, 2 * MEM_W), lambda bi, i: (bi, 0, 0)),
                  pl.BlockSpec((1, tm, d), lambda bi, i: (bi, i, 0)),
                  pl.BlockSpec((D_INNER, d), lambda bi, i: (0, 0)),
                  pl.BlockSpec((1, d), lambda bi, i: (0, 0)),
                  pl.BlockSpec((1, d), lambda bi, i: (0, 0))],
        out_specs=pl.BlockSpec((1, tm, d), lambda bi, i: (bi, i, 0)),
        scratch_shapes=[pltpu.VMEM((tm, d), _f32)],
        compiler_params=_cparams(2),
    )(o_mix, h, h, h, kv, x, w_out, ln_g.reshape(1, d), ln_b.reshape(1, d))


def _split_cols(w, sizes):
    out, c = [], 0
    for s in sizes:
        out.append(w[:, c:c + s])
        c += s
    return out


def kernel(x, mem, w_in_a, sink_a, w_in_b, w_in_c, rpb_c, w_mkv, w_out, ln_g, ln_b):
    b, l, d = x.shape
    mlen = mem.shape[1]
    kv_all = _memkv(mem.reshape(b * mlen, d), w_mkv.astype(_bf16)).reshape(DEPTH, b, mlen, 2 * MEM_W)
    tables = _rope_tables(l)

    for i in range(DEPTH):
        kind, j = i % NUM_MIXERS, i // NUM_MIXERS
        tail = MEM_W + D_INNER
        if kind == 0:
            seg = ((1, ((MIX_W, "rope_q"), (MEM_W, "rope_k"), (MEM_W + tail, "plain"))),)
            h, = _proj(x, w_in_a[j].astype(_bf16), tables, seg)
            o_mix = _attn_a(h, sink_a[j])
            mq_col = MIX_W + 2 * MEM_W
        elif kind == 1:
            wq0, wq1, wq2, wrest = _split_cols(w_in_b[j].astype(_bf16), (MIX_W, MIX_W, MIX_W, 2 * MIX_W + tail))
            seg = ((B_DILATIONS[2], ((MIX_W, "rope_q"),)),
                   (B_DILATIONS[1], ((MIX_W, "rope_q"),)),
                   (1, ((MIX_W, "rope_q"), (MIX_W, "rope_k"), (MIX_W + tail, "plain"))))
            hq2, hq1, h = _proj(x, jnp.concatenate([wq2, wq1, wq0, wrest], axis=1), tables, seg)
            o_mix = _attn_b(h, hq1, hq2)
            mq_col = 3 * MIX_W
        else:
            seg = ((1, ((MIX_W, "scale"), (2 * MIX_W + tail, "plain"))),)
            h, = _proj(x, w_in_c[j].astype(_bf16), tables, seg, subblock=True)
            o_mix = _attn_c(h, rpb_c[j])
            mq_col = 3 * MIX_W
        x = _out(o_mix, h, mq_col, kv_all[i], x, w_out[i].astype(_bf16), ln_g[i], ln_b[i], subblock=(kind == 2))
    return x
```

```python
import functools
import math

import jax
import jax.numpy as jnp
from jax import lax
from jax.experimental import pallas as pl
from jax.experimental.pallas import tpu as pltpu

D_MODEL = 1024
DEPTH = 4
NUM_MIXERS = 3
HEAD_DIM = 128
D_INNER = 2 * D_MODEL
MEM_HEADS = 4
MIX_HEADS = D_INNER // HEAD_DIM - MEM_HEADS
A_KV_HEADS = MIX_HEADS // 3
A_GROUP = MIX_HEADS // A_KV_HEADS
A_WINDOW = 128
B_DILATIONS = (1, 4, 16)
B_HALF_WINDOW = 64
NA_ROWS = 8
NA_COLS = 16
GRID_W = 64
ROPE_THETA = 500000.0
ROPE_DIMS = HEAD_DIM // 4
ROPE_HALF = ROPE_DIMS // 2
LN_EPS = 1e-5
ALPHA = (2 * DEPTH) ** 0.25
NEG_INF = -1e30
LOG2E = math.log2(math.e)
QSCALE = HEAD_DIM ** -0.5 * LOG2E

MIX_W = MIX_HEADS * HEAD_DIM
MEM_W = MEM_HEADS * HEAD_DIM

SUB_R = 4
SUB_C = 8
SUB = SUB_R * SUB_C
SUBS_PER_ROW = GRID_W // SUB_C
QBLK = 128

VMEM_LIMIT = 56 * 1024 * 1024
PROJ_TM = 512
PROJ_TN = 512
OUT_TM = 1024
OUT_SUB = 512

_f32 = jnp.float32
_bf16 = jnp.bfloat16


def _cparams(n_grid):
    return pltpu.CompilerParams(dimension_semantics=("arbitrary",) * n_grid,
                                vmem_limit_bytes=VMEM_LIMIT)


def _rope(blk, cos, sin):
    lane = lax.broadcasted_iota(jnp.int32, blk.shape, 1)
    up = pltpu.roll(blk, HEAD_DIM - ROPE_HALF, 1)
    down = pltpu.roll(blk, ROPE_HALF, 1)
    swapped = jnp.where(lane < ROPE_HALF, up, down)
    return blk * cos + swapped * sin


def _proj_kernel(*refs, subblock, segments, has_tables):
    x_ref, w_ref = refs[:2]
    pos = 2
    if has_tables:
        cos_ref, sin_ref = refs[2:4]
        pos = 4
    out_refs = refs[pos:pos + len(segments)]
    xs_ref, tmp_ref = refs[pos + len(segments):]
    tm = xs_ref.shape[0]
    tn = PROJ_TN

    if subblock:
        grp = SUB * SUBS_PER_ROW
        for g in range(tm // grp):
            for c8 in range(SUBS_PER_ROW):
                parts = [x_ref[0, g * grp + i4 * GRID_W + c8 * SUB_C:
                               g * grp + i4 * GRID_W + (c8 + 1) * SUB_C, :] for i4 in range(SUB_R)]
                dst = (g * SUBS_PER_ROW + c8) * SUB
                xs_ref[dst:dst + SUB, :] = jnp.concatenate(parts, axis=0).astype(_bf16)
    else:
        xs_ref[...] = x_ref[0].astype(_bf16)

    def write(o_ref, dil, val, c0):
        c1 = c0 + HEAD_DIM
        if dil == 1:
            o_ref[0, :, c0:c1] = val.astype(o_ref.dtype)
            return
        n = tm // dil
        tmp_ref[0] = val
        if dil == B_DILATIONS[2]:
            d1 = B_DILATIONS[1]
            n1 = tm // d1
            for r in range(d1):
                tmp_ref[1, r * n1:(r + 1) * n1, :] = tmp_ref[0, pl.ds(r, n1, stride=d1), :]
            for r in range(dil):
                rows = tmp_ref[1, pl.ds((r % d1) * n1 + r // d1, n, stride=d1), :]
                o_ref[0, r, :, c0:c1] = rows.astype(o_ref.dtype)
        else:
            for r in range(dil):
                o_ref[0, r, :, c0:c1] = tmp_ref[0, pl.ds(r, n, stride=dil), :].astype(o_ref.dtype)

    wcol = 0
    for o_ref, (dil, parts) in zip(out_refs, segments):
        ocol = 0
        for n_cols, mode in parts:
            for _ in range(n_cols // tn):
                acc = jnp.dot(xs_ref[...], w_ref[:, wcol:wcol + tn], preferred_element_type=_f32)
                for hd in range(tn // HEAD_DIM):
                    val = acc[:, hd * HEAD_DIM:(hd + 1) * HEAD_DIM]
                    if mode == "rope_q":
                        val = _rope(val, cos_ref[0], sin_ref[0])
                    elif mode == "rope_k":
                        val = _rope(val, cos_ref[1], sin_ref[1])
                    elif mode == "scale":
                        val = val * QSCALE
                    write(o_ref, dil, val, ocol + hd * HEAD_DIM)
                wcol += tn
                ocol += tn


def _proj(x, w, tables, segments, *, subblock=False):
    b, l, d = x.shape
    tm, tn = PROJ_TM, PROJ_TN
    tiles = l // tm
    assert l % tm == 0
    widths = [sum(n for n, _ in parts) for _, parts in segments]
    assert sum(widths) == w.shape[1] and all(n % tn == 0 for _, parts in segments for n, _ in parts)
    out_shapes, out_specs = [], []
    for (dil, _), n in zip(segments, widths):
        if dil == 1:
            out_shapes.append(jax.ShapeDtypeStruct((b, l, n), _bf16))
            out_specs.append(pl.BlockSpec((1, tm, n), lambda i: (i // tiles, i % tiles, 0)))
        else:
            out_shapes.append(jax.ShapeDtypeStruct((b, dil, l // dil, n), _bf16))
            out_specs.append(pl.BlockSpec((1, dil, tm // dil, n), lambda i: (i // tiles, 0, i % tiles, 0)))
    in_specs = [pl.BlockSpec((1, tm, d), lambda i: (i // tiles, i % tiles, 0)),
                pl.BlockSpec((d, w.shape[1]), lambda i: (0, 0), pipeline_mode=pl.Buffered(1))]
    args = [x, w]
    has_tables = any(mode.startswith("rope") for _, parts in segments for _, mode in parts)
    if has_tables:
        tab_spec = pl.BlockSpec((2, tm, HEAD_DIM), lambda i: (0, i % tiles, 0))
        in_specs += [tab_spec, tab_spec]
        args += list(tables)
    kern = functools.partial(_proj_kernel, subblock=subblock, segments=segments, has_tables=has_tables)
    return pl.pallas_call(
        kern,
        out_shape=out_shapes,
        grid=(b * tiles,),
        in_specs=in_specs,
        out_specs=out_specs,
        scratch_shapes=[pltpu.VMEM((tm, d), _bf16), pltpu.VMEM((2, tm, HEAD_DIM), _f32)],
        compiler_params=_cparams(1),
    )(*args)


def _rope_tables(l):
    inv = ROPE_THETA ** (-jnp.arange(ROPE_HALF, dtype=_f32) / ROPE_HALF)
    ang = jnp.arange(l).astype(_f32)[:, None] * inv[None, :]
    cos, sin = jnp.cos(ang), jnp.sin(ang)
    pad = HEAD_DIM - ROPE_DIMS
    cos_t = jnp.concatenate([cos, cos, jnp.ones((l, pad), _f32)], axis=1)
    sin_t = jnp.concatenate([-sin, sin, jnp.zeros((l, pad), _f32)], axis=1)
    return jnp.stack([cos_t * QSCALE, cos_t]), jnp.stack([sin_t * QSCALE, sin_t])


def _memkv_kernel(m_ref, w_ref, o_ref):
    o_ref[0] = jnp.dot(m_ref[...].astype(_bf16), w_ref[0], preferred_element_type=_f32).astype(o_ref.dtype)


def _memkv(mem2d, w_mkv):
    rows, d = mem2d.shape
    depth, _, n = w_mkv.shape
    tm = min(rows, PROJ_TM)
    assert rows % tm == 0
    return pl.pallas_call(
        _memkv_kernel,
        out_shape=jax.ShapeDtypeStruct((depth, rows, n), _bf16),
        grid=(depth, rows // tm),
        in_specs=[pl.BlockSpec((tm, d), lambda li, i: (i, 0)),
                  pl.BlockSpec((1, d, n), lambda li, i: (li, 0, 0))],
        out_specs=pl.BlockSpec((1, tm, n), lambda li, i: (li, i, 0)),
        compiler_params=_cparams(2),
    )(mem2d, w_mkv)


def _qk(q, k):
    return lax.dot_general(q, k, (((1,), (1,)), ((), ())), preferred_element_type=_f32)


def _band_masks(mask_ref, nk, half, deltas):
    rel = (lax.broadcasted_iota(jnp.int32, (QBLK, nk), 1) - lax.broadcasted_iota(jnp.int32, (QBLK, nk), 0))
    for v, delta in enumerate(deltas):
        mask_ref[v] = jnp.where(jnp.abs(rel + delta) <= half, 0.0, NEG_INF).astype(_f32)


def _edge_variant(i, n):
    return jnp.where(i == 0, 0, jnp.where(i == n - 1, 2, 1))


def _softmax_pv(scores, values, floors=None):
    def rowmax(s):
        m = s[:, 0:HEAD_DIM]
        for c in range(1, s.shape[1] // HEAD_DIM):
            m = jnp.maximum(m, s[:, c * HEAD_DIM:(c + 1) * HEAD_DIM])
        return jnp.max(m, axis=-1, keepdims=True)

    ms = [rowmax(s) for s in scores]
    if floors is not None:
        ms = [jnp.maximum(m, f) for m, f in zip(ms, floors)]
    ps = [jnp.exp2(s - m).astype(_bf16) for s, m in zip(scores, ms)]
    accs = [jnp.dot(p, v, preferred_element_type=_f32) for p, v in zip(ps, values)]
    return ms, accs


A_UNROLL = 4


def _attn_a_kernel(q_ref, k_ref, v_ref, sink_ref, o_ref, v1_ref, mask_ref, es_ref):
    l = k_ref.shape[1]
    nk = 3 * A_WINDOW
    nblk = l // QBLK
    v1_ref[:, 0:HEAD_DIM] = v_ref[0]
    v1_ref[:, HEAD_DIM:2 * HEAD_DIM] = jnp.ones((l, HEAD_DIM), _bf16)
    rel = (lax.broadcasted_iota(jnp.int32, (QBLK, nk), 1) - lax.broadcasted_iota(jnp.int32, (QBLK, nk), 0))
    for v, delta in enumerate((0, -A_WINDOW, -2 * A_WINDOW)):
        blk = jnp.where(jnp.abs(rel + delta) <= A_WINDOW, 0.0, NEG_INF).astype(_f32)
        for a in range(A_GROUP):
            mask_ref[v, a * QBLK:(a + 1) * QBLK, :] = blk
    for a in range(A_GROUP):
        es_ref[a * QBLK:(a + 1) * QBLK, :] = jnp.broadcast_to(sink_ref[0, a:a + 1, :] * LOG2E, (QBLK, HEAD_DIM))

    def body(t, carry):
        scores, windows = [], []
        for u in range(A_UNROLL):
            i = t * A_UNROLL + u
            q0 = pl.multiple_of(i * QBLK, QBLK)
            ks = pl.multiple_of(jnp.clip(q0 - A_WINDOW, 0, l - nk), QBLK)
            q = jnp.concatenate([q_ref[0, pl.ds(q0, QBLK), a * HEAD_DIM:(a + 1) * HEAD_DIM]
                                 for a in range(A_GROUP)], axis=0)
            scores.append(_qk(q, k_ref[0, pl.ds(ks, nk), :]) + mask_ref[_edge_variant(i, nblk)])
            windows.append((q0, ks))
        cols = range(nk // HEAD_DIM)
        ms = []
        for s in scores:
            m = s[:, 0:HEAD_DIM]
            for c in cols[1:]:
                m = jnp.maximum(m, s[:, c * HEAD_DIM:(c + 1) * HEAD_DIM])
            m = jnp.broadcast_to(jnp.max(m, axis=-1, keepdims=True), m.shape)
            ms.append(jnp.maximum(m, es_ref[...]))
        ps = [jnp.concatenate([jnp.exp2(s[:, c * HEAD_DIM:(c + 1) * HEAD_DIM] - m) for c in cols], axis=1).astype(_bf16)
              for s, m in zip(scores, ms)]
        accs = [jnp.dot(p, v1_ref[pl.ds(ks, nk), :], preferred_element_type=_f32) for p, (_, ks) in zip(ps, windows)]
        for m, acc, (q0, _) in zip(ms, accs, windows):
            z = acc[:, HEAD_DIM:] + jnp.exp2(es_ref[...] - m)
            o = (acc[:, :HEAD_DIM] / z).astype(o_ref.dtype)
            for a in range(A_GROUP):
                o_ref[0, pl.ds(q0, QBLK), a * HEAD_DIM:(a + 1) * HEAD_DIM] = o[a * QBLK:(a + 1) * QBLK, :]
        return carry

    lax.fori_loop(0, nblk // A_UNROLL, body, 0)


def _attn_a(h, sink):
    b, l, _ = h.shape
    gw = A_GROUP * HEAD_DIM
    sink_b = jnp.broadcast_to(sink.astype(_f32).reshape(A_KV_HEADS, A_GROUP, 1), (A_KV_HEADS, A_GROUP, HEAD_DIM))
    sink_b = jnp.pad(sink_b, ((0, 0), (0, 8 - A_GROUP), (0, 0)))
    return pl.pallas_call(
        _attn_a_kernel,
        out_shape=jax.ShapeDtypeStruct((b, l, MIX_W), _bf16),
        grid=(b, A_KV_HEADS),
        in_specs=[pl.BlockSpec((1, l, gw), lambda bi, g: (bi, 0, g)),
                  pl.BlockSpec((1, l, HEAD_DIM), lambda bi, g: (bi, 0, MIX_HEADS + g)),
                  pl.BlockSpec((1, l, HEAD_DIM), lambda bi, g: (bi, 0, MIX_HEADS + A_KV_HEADS + g)),
                  pl.BlockSpec((1, 8, HEAD_DIM), lambda bi, g: (g, 0, 0))],
        out_specs=pl.BlockSpec((1, l, gw), lambda bi, g: (bi, 0, g)),
        scratch_shapes=[pltpu.VMEM((l, 2 * HEAD_DIM), _bf16),
                        pltpu.VMEM((3, A_GROUP * QBLK, 3 * A_WINDOW), _f32),
                        pltpu.VMEM((A_GROUP * QBLK, HEAD_DIM), _f32)],
        compiler_params=_cparams(2),
    )(h, h, h, sink_b)


B_UNROLL = 16


def _attn_b_kernel(q0_ref, k_ref, v_ref, q1_ref, q2_ref, o_ref,
                   kd_ref, v1_ref, tmp_ref, tmp2_ref, og_ref, lse_ref, mask_ref):
    l = o_ref.shape[1]
    nk = 4 * B_HALF_WINDOW
    q_refs = (q0_ref, q1_ref, q2_ref)
    _band_masks(mask_ref, nk, B_HALF_WINDOW, (0, -B_HALF_WINDOW, -2 * B_HALF_WINDOW))

    v1_ref[0, :, 0:HEAD_DIM] = v_ref[0]

    @pl.when((pl.program_id(0) == 0) & (pl.program_id(1) == 0))
    def _():
        for g in range(3):
            v1_ref[g, :, HEAD_DIM:2 * HEAD_DIM] = jnp.ones((l, HEAD_DIM), _bf16)
    d1 = B_DILATIONS[1]
    assert B_DILATIONS == (1, d1, d1 * d1)
    n1, n2 = l // d1, l // (d1 * d1)
    for src, is_key in ((k_ref, True), (v_ref, False)):
        def put(g, row0, n, rows):
            if is_key:
                kd_ref[g - 1, row0:row0 + n, :] = rows.astype(_bf16)
            else:
                v1_ref[g, row0:row0 + n, 0:HEAD_DIM] = rows.astype(_bf16)

        tmp_ref[...] = src[0].astype(_f32)
        for r in range(d1):
            rows = tmp_ref[pl.ds(r, n1, stride=d1), :]
            tmp2_ref[r * n1:(r + 1) * n1, :] = rows
            put(1, r * n1, n1, rows)
        for r in range(d1 * d1):
            put(2, r * n2, n2, tmp2_ref[pl.ds((r % d1) * n1 + r // d1, n2, stride=d1), :])

    for g, dil in reversed(list(enumerate(B_DILATIONS))):
        n = l // dil
        blocks = n // QBLK

        def body(t, carry, g=g, dil=dil, n=n, blocks=blocks):
            scores, values, dests = [], [], []
            for u in range(B_UNROLL):
                bt = t * B_UNROLL + u
                r = bt // blocks
                i = bt % blocks
                q0 = pl.multiple_of(i * QBLK, QBLK)
                ks = pl.multiple_of(jnp.clip(q0 - B_HALF_WINDOW, 0, n - nk), B_HALF_WINDOW)
                base = pl.multiple_of(r * n + ks, B_HALF_WINDOW)
                if g == 0:
                    q = q0_ref[0, pl.ds(q0, QBLK), :]
                    kw = k_ref[0, pl.ds(ks, nk), :]
                else:
                    q = q_refs[g][0, r, pl.ds(q0, QBLK), :]
                    kw = kd_ref[g - 1, pl.ds(base, nk), :]
                scores.append(_qk(q, kw) + mask_ref[_edge_variant(i, blocks)])
                values.append(v1_ref[g, pl.ds(base, nk), :])
                dests.append((q0, r))
            ms, accs = _softmax_pv(scores, values)
            for m, acc, (q0, r) in zip(ms, accs, dests):
                z = acc[:, HEAD_DIM:]
                if dil > 1:
                    og_ref[g - 1, pl.ds(q0 * dil + r, QBLK, stride=dil), :] = acc[:, :HEAD_DIM] / z
                    lse_ref[g - 1, pl.ds(q0 * dil + r, QBLK, stride=dil), :] = m + jnp.log2(z)
                else:
                    l1 = lse_ref[0, pl.ds(q0, QBLK), :]
                    l2 = lse_ref[1, pl.ds(q0, QBLK), :]
                    top = jnp.maximum(jnp.maximum(m, l1), l2)
                    w0, w1, w2 = jnp.exp2(m - top), jnp.exp2(l1 - top), jnp.exp2(l2 - top)
                    num = (w0 * acc[:, :HEAD_DIM] + w1 * og_ref[0, pl.ds(q0, QBLK), :]
                           + w2 * og_ref[1, pl.ds(q0, QBLK), :])
                    o_ref[0, pl.ds(q0, QBLK), :] = (num / (w0 * z + w1 + w2)).astype(o_ref.dtype)
            return carry

        lax.fori_loop(0, dil * blocks // B_UNROLL, body, 0)


def _attn_b(h0, hq1, hq2):
    b, l, _ = h0.shape
    in_specs = [pl.BlockSpec((1, l, HEAD_DIM), lambda bi, h: (bi, 0, h)),
                pl.BlockSpec((1, l, HEAD_DIM), lambda bi, h: (bi, 0, MIX_HEADS + h)),
                pl.BlockSpec((1, l, HEAD_DIM), lambda bi, h: (bi, 0, 2 * MIX_HEADS + h))]
    for dil in B_DILATIONS[1:]:
        in_specs.append(pl.BlockSpec((1, dil, l // dil, HEAD_DIM), lambda bi, h: (bi, 0, 0, h)))
    return pl.pallas_call(
        _attn_b_kernel,
        out_shape=jax.ShapeDtypeStruct((b, l, MIX_W), _bf16),
        grid=(b, MIX_HEADS),
        in_specs=in_specs,
        out_specs=pl.BlockSpec((1, l, HEAD_DIM), lambda bi, h: (bi, 0, h)),
        scratch_shapes=[pltpu.VMEM((2, l, HEAD_DIM), _bf16),
                        pltpu.VMEM((3, l, 2 * HEAD_DIM), _bf16),
                        pltpu.VMEM((l, HEAD_DIM), _f32),
                        pltpu.VMEM((l, HEAD_DIM), _f32),
                        pltpu.VMEM((2, l, HEAD_DIM), _f32),
                        pltpu.VMEM((2, l, HEAD_DIM), _f32),
                        pltpu.VMEM((3, QBLK, 4 * B_HALF_WINDOW), _f32)],
        compiler_params=_cparams(2),
    )(h0, h0, h0, hq1, hq2)


C_KSEG = 4 * SUB
C_NSEG = 4
C_PAD = SUB
C_NKEYS = C_NSEG * C_KSEG
C_UNROLL = 4


def _attn_c_kernel(q_ref, k_ref, v_ref, bias_ref, mask_ref, o_ref, kp_ref, vp_ref, bm_ref):
    l = k_ref.shape[1]
    n_rb = l // GRID_W // NA_ROWS
    n_cb = GRID_W // NA_COLS
    n_r4 = l // GRID_W // SUB_R
    kp_ref[0:C_PAD, :] = jnp.zeros((C_PAD, HEAD_DIM), _bf16)
    kp_ref[C_PAD + l:C_PAD + l + C_PAD, :] = jnp.zeros((C_PAD, HEAD_DIM), _bf16)
    kp_ref[C_PAD:C_PAD + l, :] = k_ref[0]
    vp_ref[0:C_PAD, 0:HEAD_DIM] = jnp.zeros((C_PAD, HEAD_DIM), _bf16)
    vp_ref[C_PAD + l:C_PAD + l + C_PAD, 0:HEAD_DIM] = jnp.zeros((C_PAD, HEAD_DIM), _bf16)
    vp_ref[C_PAD:C_PAD + l, 0:HEAD_DIM] = v_ref[0]
    vp_ref[:, HEAD_DIM:2 * HEAD_DIM] = jnp.ones((l + 2 * C_PAD, HEAD_DIM), _bf16)
    for v in range(9):
        bm_ref[v] = bias_ref[0] + mask_ref[v]

    def body(t, carry):
        scores, values, dests = [], [], []
        for u in range(C_UNROLL * n_cb):
            rb = t * C_UNROLL + u // n_cb
            cb = u % n_cb
            rv = _edge_variant(rb, n_rb)
            cv = 0 if cb == 0 else (2 if cb == n_cb - 1 else 1)
            q_starts = [pl.multiple_of(((2 * rb + a) * SUBS_PER_ROW + 2 * cb) * SUB, 2 * SUB) for a in range(2)]
            q = jnp.concatenate([q_ref[0, pl.ds(qs, 2 * SUB), :] for qs in q_starts], axis=0)
            kparts, vparts = [], []
            for ar in range(C_NSEG):
                r4 = jnp.clip(2 * rb - 1 + ar, 0, n_r4 - 1)
                st = pl.multiple_of(C_PAD + (r4 * SUBS_PER_ROW + 2 * cb - 1) * SUB, SUB)
                kparts.append(kp_ref[pl.ds(st, C_KSEG), :])
                vparts.append(vp_ref[pl.ds(st, C_KSEG), :])
            scores.append(_qk(q, jnp.concatenate(kparts, axis=0)) + bm_ref[rv * 3 + cv])
            values.append(jnp.concatenate(vparts, axis=0))
            dests.append(q_starts)
        ms, accs = _softmax_pv(scores, values)
        for acc, q_starts in zip(accs, dests):
            o = (acc[:, :HEAD_DIM] / acc[:, HEAD_DIM:]).astype(o_ref.dtype)
            for a in range(2):
                o_ref[0, pl.ds(q_starts[a], 2 * SUB), :] = o[a * 2 * SUB:(a + 1) * 2 * SUB, :]
        return carry

    lax.fori_loop(0, n_rb // C_UNROLL, body, 0)


def _c_geometry():
    ql = jnp.arange(QBLK)
    qa, qc, qi, qj = ql // 64, (ql // 32) % 2, (ql // 8) % 4, ql % 8
    q_row = SUB_R * qa + qi
    q_col = SUB_C * qc + qj
    kl = jnp.arange(C_NKEYS)
    ka, kc, ki, kj = kl // C_KSEG, (kl // SUB) % 4, (kl // 8) % 4, kl % 8
    k_row = SUB_R * (ka - 1) + ki
    k_col = SUB_C * (kc - 1) + kj
    return q_row, q_col, k_row, k_col


def _c_bias(rpb):
    nr, nc = 2 * NA_ROWS - 1, 2 * NA_COLS - 1
    q_row, k_row = jnp.arange(NA_ROWS), jnp.arange(C_NSEG * SUB_R) - SUB_R
    q_col, k_col = jnp.arange(NA_COLS), jnp.arange(4 * SUB_C) - SUB_C
    dr = jnp.clip(k_row[None, :] - q_row[:, None] + NA_ROWS - 1, 0, nr - 1)
    dc = jnp.clip(k_col[None, :] - q_col[:, None] + NA_COLS - 1, 0, nc - 1)
    oh_r = (dr[:, :, None] == jnp.arange(nr)).astype(_f32)
    oh_c = (dc[:, :, None] == jnp.arange(nc)).astype(_f32)
    hp = lax.Precision.HIGHEST
    t = jnp.einsum("hrc,qkr->hqkc", rpb.astype(_f32), oh_r, precision=hp)
    t = jnp.einsum("hqkc,xyc->hqkxy", t, oh_c, precision=hp)
    h = rpb.shape[0]
    t = t.reshape(h, 2, SUB_R, C_NSEG, SUB_R, 2, SUB_C, 4, SUB_C).transpose(0, 1, 5, 2, 6, 3, 7, 4, 8)
    return t.reshape(h, QBLK, C_NKEYS) * LOG2E


def _c_masks(rows):
    q_row, q_col, k_row, k_col = _c_geometry()
    out = []
    for rb in (0, 1, rows // NA_ROWS - 1):
        qr, kr = NA_ROWS * rb + q_row, NA_ROWS * rb + k_row
        rs = jnp.clip(qr - NA_ROWS // 2, 0, rows - NA_ROWS)
        rvalid = (kr[None, :] >= rs[:, None]) & (kr[None, :] < rs[:, None] + NA_ROWS) & (kr[None, :] >= 0) & (kr[None, :] < rows)
        for cb in (0, 1, GRID_W // NA_COLS - 1):
            qc, kc = NA_COLS * cb + q_col, NA_COLS * cb + k_col
            cs = jnp.clip(qc - NA_COLS // 2, 0, GRID_W - NA_COLS)
            cvalid = (kc[None, :] >= cs[:, None]) & (kc[None, :] < cs[:, None] + NA_COLS) & (kc[None, :] >= 0) & (kc[None, :] < GRID_W)
            out.append(jnp.where(rvalid & cvalid, 0.0, NEG_INF).astype(_f32))
    return jnp.stack(out)


def _attn_c(h, rpb):
    b, l, _ = h.shape
    bias = _c_bias(rpb)
    masks = _c_masks(l // GRID_W)
    return pl.pallas_call(
        _attn_c_kernel,
        out_shape=jax.ShapeDtypeStruct((b, l, MIX_W), _bf16),
        grid=(b, MIX_HEADS),
        in_specs=[pl.BlockSpec((1, l, HEAD_DIM), lambda bi, h: (bi, 0, h)),
                  pl.BlockSpec((1, l, HEAD_DIM), lambda bi, h: (bi, 0, MIX_HEADS + h)),
                  pl.BlockSpec((1, l, HEAD_DIM), lambda bi, h: (bi, 0, 2 * MIX_HEADS + h)),
                  pl.BlockSpec((1, QBLK, C_NKEYS), lambda bi, h: (h, 0, 0)),
                  pl.BlockSpec((9, QBLK, C_NKEYS), lambda bi, h: (0, 0, 0))],
        out_specs=pl.BlockSpec((1, l, HEAD_DIM), lambda bi, h: (bi, 0, h)),
        scratch_shapes=[pltpu.VMEM((l + 2 * C_PAD, HEAD_DIM), _bf16),
                        pltpu.VMEM((l + 2 * C_PAD, 2 * HEAD_DIM), _bf16),
                        pltpu.VMEM((9, QBLK, C_NKEYS), _f32)],
        compiler_params=_cparams(2),
    )(h, h, h, bias, masks)


def _subblock_chunks(tm):
    grp = SUB * SUBS_PER_ROW
    pairs = []
    for g in range(tm // grp):
        for c8 in range(SUBS_PER_ROW):
            for i4 in range(SUB_R):
                pairs.append((g * grp + i4 * GRID_W + c8 * SUB_C, (g * SUBS_PER_ROW + c8) * SUB + i4 * SUB_C))
    return pairs


def _out_kernel(om_ref, glo_ref, ghi_ref, mq_ref, kv_ref, x_ref, w_ref, g_ref, b_ref, o_ref, xs_ref, *, subblock):
    half = D_INNER // 2
    ones = jnp.ones((kv_ref.shape[1], HEAD_DIM), _bf16)
    values = [jnp.concatenate([kv_ref[0, :, MEM_W + h * HEAD_DIM:MEM_W + (h + 1) * HEAD_DIM], ones], axis=1)
              for h in range(MEM_HEADS)]

    n_slices = D_INNER // MEM_W
    part_rows = OUT_SUB // n_slices

    def layer_norm_part(r0, q, z_in):
        lo = q * part_rows
        xin = (xs_ref if subblock else x_ref.at[0])[r0 + lo:r0 + lo + part_rows, :]
        z = ALPHA * xin + z_in[lo:lo + part_rows, :]
        mu = jnp.mean(z, axis=-1, keepdims=True)
        zc = z - mu
        var = jnp.mean(zc * zc, axis=-1, keepdims=True)
        out = zc * lax.rsqrt(var + LN_EPS) * g_ref[...] + b_ref[...]
        if subblock:
            for nat, sub in _subblock_chunks(OUT_SUB):
                if lo <= sub < lo + part_rows:
                    o_ref[0, r0 + nat:r0 + nat + SUB_C, :] = out[sub - lo:sub - lo + SUB_C, :]
        else:
            o_ref[0, r0 + lo:r0 + lo + part_rows, :] = out

    pending = None
    for r0 in range(0, om_ref.shape[1], OUT_SUB):
        rows = slice(r0, r0 + OUT_SUB)
        if subblock:
            for nat, sub in _subblock_chunks(OUT_SUB):
                xs_ref[r0 + sub:r0 + sub + SUB_C, :] = x_ref[0, r0 + nat:r0 + nat + SUB_C, :]

        def silu_gate(c0, c1, rows=rows):
            ref, off = (glo_ref, 0) if c0 < half else (ghi_ref, half)
            hg = 0.5 * ref[0, rows, c0 - off:c1 - off].astype(_f32)
            return hg + hg * jnp.tanh(hg)

        scores = [_qk(mq_ref[0, rows, h * HEAD_DIM:(h + 1) * HEAD_DIM],
                      kv_ref[0, :, h * HEAD_DIM:(h + 1) * HEAD_DIM]) * QSCALE for h in range(MEM_HEADS)]
        _, accs = _softmax_pv(scores, values)
        branch = None
        for q, c0 in enumerate(range(0, D_INNER, MEM_W)):
            if c0 < MIX_W:
                y = om_ref[0, rows, c0:c0 + MEM_W].astype(_f32) * silu_gate(c0, c0 + MEM_W)
            else:
                y = jnp.concatenate([acc[:, :HEAD_DIM] / acc[:, HEAD_DIM:] for acc in accs], axis=1)
                y = y * silu_gate(c0, c0 + MEM_W)
            part = jnp.dot(y.astype(_bf16), w_ref[c0:c0 + MEM_W, :], preferred_element_type=_f32)
            branch = part if branch is None else branch + part
            if pending is not None:
                layer_norm_part(pending[0], q, pending[1])
        pending = (r0, branch)
    for q in range(n_slices):
        layer_norm_part(pending[0], q, pending[1])


def _out(o_mix, h, mq_col, kv, x, w_out, ln_g, ln_b, *, subblock):
    b, l, d = x.shape
    tm = OUT_TM
    mlen = kv.shape[1]
    half = D_INNER // 2
    gate_col = mq_col + MEM_W
    assert mq_col % MEM_W == 0 and gate_col % half == 0
    kern = functools.partial(_out_kernel, subblock=subblock)
    return pl.pallas_call(
        kern,
        out_shape=jax.ShapeDtypeStruct((b, l, d), _f32),
        grid=(b, l // tm),
        in_specs=[pl.BlockSpec((1, tm, MIX_W), lambda bi, i: (bi, i, 0)),
                  pl.BlockSpec((1, tm, half), lambda bi, i: (bi, i, gate_col // half)),
                  pl.BlockSpec((1, tm, half), lambda bi, i: (bi, i, gate_col // half + 1)),
                  pl.BlockSpec((1, tm, MEM_W), lambda bi, i: (bi, i, mq_col // MEM_W)),
                  pl.BlockSpec((1, mlen, 2 * MEM_W), lambda bi, i: (bi, 0, 0)),
                  pl.BlockSpec((1, tm, d), lambda bi, i: (bi, i, 0)),
                  pl.BlockSpec((D_INNER, d), lambda bi, i: (0, 0)),
                  pl.BlockSpec((1, d), lambda bi, i: (0, 0)),
                  pl.BlockSpec((1, d), lambda bi, i: (0, 0))],
        out_specs=pl.BlockSpec((1, tm, d), lambda bi, i: (bi, i, 0)),
        scratch_shapes=[pltpu.VMEM((tm, d), _f32)],
        compiler_params=_cparams(2),
    )(o_mix, h, h, h, kv, x, w_out, ln_g.reshape(1, d), ln_b.reshape(1, d))


def _split_cols(w, sizes):
    out, c = [], 0
    for s in sizes:
        out.append(w[:, c:c + s])
        c += s
    return out


def kernel(x, mem, w_in_a, sink_a, w_in_b, w_in_c, rpb_c, w_mkv, w_out, ln_g, ln_b):
    b, l, d = x.shape
    mlen = mem.shape[1]
    kv_all = _memkv(mem.reshape(b * mlen, d), w_mkv.astype(_bf16)).reshape(DEPTH, b, mlen, 2 * MEM_W)
    tables = _rope_tables(l)

    for i in range(DEPTH):
        kind, j = i % NUM_MIXERS, i // NUM_MIXERS
        tail = MEM_W + D_INNER
        if kind == 0:
            seg = ((1, ((MIX_W, "rope_q"), (MEM_W, "rope_k"), (MEM_W + tail, "plain"))),)
            h, = _proj(x, w_in_a[j].astype(_bf16), tables, seg)
            o_mix = _attn_a(h, sink_a[j])
            mq_col = MIX_W + 2 * MEM_W
        elif kind == 1:
            wq0, wq1, wq2, wrest = _split_cols(w_in_b[j].astype(_bf16), (MIX_W, MIX_W, MIX_W, 2 * MIX_W + tail))
            seg = ((B_DILATIONS[2], ((MIX_W, "rope_q"),)),
                   (B_DILATIONS[1], ((MIX_W, "rope_q"),)),
                   (1, ((MIX_W, "rope_q"), (MIX_W, "rope_k"), (MIX_W + tail, "plain"))))
            hq2, hq1, h = _proj(x, jnp.concatenate([wq2, wq1, wq0, wrest], axis=1), tables, seg)
            o_mix = _attn_b(h, hq1, hq2)
            mq_col = 3 * MIX_W
        else:
            seg = ((1, ((MIX_W, "scale"), (2 * MIX_W + tail, "plain"))),)
            h, = _proj(x, w_in_c[j].astype(_bf16), tables, seg, subblock=True)
            o_mix = _attn_c(h, rpb_c[j])
            mq_col = 3 * MIX_W
        x = _out(o_mix, h, mq_col, kv_all[i], x, w_out[i].astype(_bf16), ln_g[i], ln_b[i], subblock=(kind == 2))
    return x
```

```python
import functools
import math

import jax
import jax.numpy as jnp
from jax import lax
from jax.experimental import pallas as pl
from jax.experimental.pallas import tpu as pltpu

D_MODEL = 1024
DEPTH = 4
NUM_MIXERS = 3
HEAD_DIM = 128
D_INNER = 2 * D_MODEL
MEM_HEADS = 4
MIX_HEADS = D_INNER // HEAD_DIM - MEM_HEADS
A_KV_HEADS = MIX_HEADS // 3
A_GROUP = MIX_HEADS // A_KV_HEADS
A_WINDOW = 128
B_DILATIONS = (1, 4, 16)
B_HALF_WINDOW = 64
NA_ROWS = 8
NA_COLS = 16
GRID_W = 64
ROPE_THETA = 500000.0
ROPE_DIMS = HEAD_DIM // 4
ROPE_HALF = ROPE_DIMS // 2
LN_EPS = 1e-5
ALPHA = (2 * DEPTH) ** 0.25
NEG_INF = -1e30
LOG2E = math.log2(math.e)
QSCALE = HEAD_DIM ** -0.5 * LOG2E

MIX_W = MIX_HEADS * HEAD_DIM
MEM_W = MEM_HEADS * HEAD_DIM

SUB_R = 4
SUB_C = 8
SUB = SUB_R * SUB_C
SUBS_PER_ROW = GRID_W // SUB_C
QBLK = 128

VMEM_LIMIT = 56 * 1024 * 1024
PROJ_TM = 512
PROJ_TN = 512
OUT_TM = 1024
OUT_SUB = 512

_f32 = jnp.float32
_bf16 = jnp.bfloat16


def _cparams(n_grid):
    return pltpu.CompilerParams(dimension_semantics=("arbitrary",) * n_grid,
                                vmem_limit_bytes=VMEM_LIMIT)


def _rope(blk, cos, sin):
    lane = lax.broadcasted_iota(jnp.int32, blk.shape, 1)
    up = pltpu.roll(blk, HEAD_DIM - ROPE_HALF, 1)
    down = pltpu.roll(blk, ROPE_HALF, 1)
    swapped = jnp.where(lane < ROPE_HALF, up, down)
    return blk * cos + swapped * sin


def _proj_kernel(*refs, subblock, segments, has_tables):
    x_ref, w_ref = refs[:2]
    pos = 2
    if has_tables:
        cos_ref, sin_ref = refs[2:4]
        pos = 4
    out_refs = refs[pos:pos + len(segments)]
    xs_ref, tmp_ref = refs[pos + len(segments):]
    tm = xs_ref.shape[0]
    tn = PROJ_TN

    if subblock:
        grp = SUB * SUBS_PER_ROW
        for g in range(tm // grp):
            for c8 in range(SUBS_PER_ROW):
                parts = [x_ref[0, g * grp + i4 * GRID_W + c8 * SUB_C:
                               g * grp + i4 * GRID_W + (c8 + 1) * SUB_C, :] for i4 in range(SUB_R)]
                dst = (g * SUBS_PER_ROW + c8) * SUB
                xs_ref[dst:dst + SUB, :] = jnp.concatenate(parts, axis=0).astype(_bf16)
    else:
        xs_ref[...] = x_ref[0].astype(_bf16)

    def write(o_ref, dil, val, c0):
        c1 = c0 + HEAD_DIM
        if dil == 1:
            o_ref[0, :, c0:c1] = val.astype(o_ref.dtype)
            return
        n = tm // dil
        tmp_ref[0] = val
        if dil == B_DILATIONS[2]:
            d1 = B_DILATIONS[1]
            n1 = tm // d1
            for r in range(d1):
                tmp_ref[1, r * n1:(r + 1) * n1, :] = tmp_ref[0, pl.ds(r, n1, stride=d1), :]
            for r in range(dil):
                rows = tmp_ref[1, pl.ds((r % d1) * n1 + r // d1, n, stride=d1), :]
                o_ref[0, r, :, c0:c1] = rows.astype(o_ref.dtype)
        else:
            for r in range(dil):
                o_ref[0, r, :, c0:c1] = tmp_ref[0, pl.ds(r, n, stride=dil), :].astype(o_ref.dtype)

    wcol = 0
    for o_ref, (dil, parts) in zip(out_refs, segments):
        ocol = 0
        for n_cols, mode in parts:
            for _ in range(n_cols // tn):
                acc = jnp.dot(xs_ref[...], w_ref[:, wcol:wcol + tn], preferred_element_type=_f32)
                for hd in range(tn // HEAD_DIM):
                    val = acc[:, hd * HEAD_DIM:(hd + 1) * HEAD_DIM]
                    if mode == "rope_q":
                        val = _rope(val, cos_ref[0], sin_ref[0])
                    elif mode == "rope_k":
                        val = _rope(val, cos_ref[1], sin_ref[1])
                    elif mode == "scale":
                        val = val * QSCALE
                    write(o_ref, dil, val, ocol + hd * HEAD_DIM)
                wcol += tn
                ocol += tn


def _proj(x, w, tables, segments, *, subblock=False):
    b, l, d = x.shape
    tm, tn = PROJ_TM, PROJ_TN
    tiles = l // tm
    assert l % tm == 0
    widths = [sum(n for n, _ in parts) for _, parts in segments]
    assert sum(widths) == w.shape[1] and all(n % tn == 0 for _, parts in segments for n, _ in parts)
    out_shapes, out_specs = [], []
    for (dil, _), n in zip(segments, widths):
        if dil == 1:
            out_shapes.append(jax.ShapeDtypeStruct((b, l, n), _bf16))
            out_specs.append(pl.BlockSpec((1, tm, n), lambda i: (i // tiles, i % tiles, 0)))
        else:
            out_shapes.append(jax.ShapeDtypeStruct((b, dil, l // dil, n), _bf16))
            out_specs.append(pl.BlockSpec((1, dil, tm // dil, n), lambda i: (i // tiles, 0, i % tiles, 0)))
    in_specs = [pl.BlockSpec((1, tm, d), lambda i: (i // tiles, i % tiles, 0)),
                pl.BlockSpec((d, w.shape[1]), lambda i: (0, 0), pipeline_mode=pl.Buffered(1))]
    args = [x, w]
    has_tables = any(mode.startswith("rope") for _, parts in segments for _, mode in parts)
    if has_tables:
        tab_spec = pl.BlockSpec((2, tm, HEAD_DIM), lambda i: (0, i % tiles, 0))
        in_specs += [tab_spec, tab_spec]
        args += list(tables)
    kern = functools.partial(_proj_kernel, subblock=subblock, segments=segments, has_tables=has_tables)
    return pl.pallas_call(
        kern,
        out_shape=out_shapes,
        grid=(b * tiles,),
        in_specs=in_specs,
        out_specs=out_specs,
        scratch_shapes=[pltpu.VMEM((tm, d), _bf16), pltpu.VMEM((2, tm, HEAD_DIM), _f32)],
        compiler_params=_cparams(1),
    )(*args)


def _rope_tables(l):
    inv = ROPE_THETA ** (-jnp.arange(ROPE_HALF, dtype=_f32) / ROPE_HALF)
    ang = jnp.arange(l).astype(_f32)[:, None] * inv[None, :]
    cos, sin = jnp.cos(ang), jnp.sin(ang)
    pad = HEAD_DIM - ROPE_DIMS
    cos_t = jnp.concatenate([cos, cos, jnp.ones((l, pad), _f32)], axis=1)
    sin_t = jnp.concatenate([-sin, sin, jnp.zeros((l, pad), _f32)], axis=1)
    return jnp.stack([cos_t * QSCALE, cos_t]), jnp.stack([sin_t * QSCALE, sin_t])


def _memkv_kernel(m_ref, w_ref, o_ref):
    o_ref[0] = jnp.dot(m_ref[...].astype(_bf16), w_ref[0], preferred_element_type=_f32).astype(o_ref.dtype)


def _memkv(mem2d, w_mkv):
    rows, d = mem2d.shape
    depth, _, n = w_mkv.shape
    tm = min(rows, PROJ_TM)
    assert rows % tm == 0
    return pl.pallas_call(
        _memkv_kernel,
        out_shape=jax.ShapeDtypeStruct((depth, rows, n), _bf16),
        grid=(depth, rows // tm),
        in_specs=[pl.BlockSpec((tm, d), lambda li, i: (i, 0)),
                  pl.BlockSpec((1, d, n), lambda li, i: (li, 0, 0))],
        out_specs=pl.BlockSpec((1, tm, n), lambda li, i: (li, i, 0)),
        compiler_params=_cparams(2),
    )(mem2d, w_mkv)


def _qk(q, k):
    return lax.dot_general(q, k, (((1,), (1,)), ((), ())), preferred_element_type=_f32)


def _first_step():
    return (pl.program_id(0) == 0) & (pl.program_id(1) == 0)


def _band_masks(mask_ref, nk, half, deltas):
    rel = (lax.broadcasted_iota(jnp.int32, (QBLK, nk), 1) - lax.broadcasted_iota(jnp.int32, (QBLK, nk), 0))
    for v, delta in enumerate(deltas):
        mask_ref[v] = jnp.where(jnp.abs(rel + delta) <= half, 0.0, NEG_INF).astype(_f32)


def _edge_variant(i, n):
    return jnp.where(i == 0, 0, jnp.where(i == n - 1, 2, 1))


def _softmax_pv(scores, values, floors=None):
    def rowmax(s):
        m = s[:, 0:HEAD_DIM]
        for c in range(1, s.shape[1] // HEAD_DIM):
            m = jnp.maximum(m, s[:, c * HEAD_DIM:(c + 1) * HEAD_DIM])
        return jnp.max(m, axis=-1, keepdims=True)

    ms = [rowmax(s) for s in scores]
    if floors is not None:
        ms = [jnp.maximum(m, f) for m, f in zip(ms, floors)]
    ps = [jnp.exp2(s - m).astype(_bf16) for s, m in zip(scores, ms)]
    accs = [jnp.dot(p, v, preferred_element_type=_f32) for p, v in zip(ps, values)]
    return ms, accs


A_UNROLL = 4


def _attn_a_kernel(q_ref, k_ref, v_ref, sink_ref, o_ref, v1_ref, mask_ref, es_ref):
    l = k_ref.shape[1]
    nk = 3 * A_WINDOW
    nblk = l // QBLK
    v1_ref[:, 0:HEAD_DIM] = v_ref[0]

    @pl.when(_first_step())
    def _():
        v1_ref[:, HEAD_DIM:2 * HEAD_DIM] = jnp.ones((l, HEAD_DIM), _bf16)
        rel = (lax.broadcasted_iota(jnp.int32, (QBLK, nk), 1) - lax.broadcasted_iota(jnp.int32, (QBLK, nk), 0))
        for v, delta in enumerate((0, -A_WINDOW, -2 * A_WINDOW)):
            blk = jnp.where(jnp.abs(rel + delta) <= A_WINDOW, 0.0, NEG_INF).astype(_f32)
            for a in range(A_GROUP):
                mask_ref[v, a * QBLK:(a + 1) * QBLK, :] = blk

    for a in range(A_GROUP):
        es_ref[a * QBLK:(a + 1) * QBLK, :] = jnp.broadcast_to(sink_ref[0, a:a + 1, :] * LOG2E, (QBLK, HEAD_DIM))

    def body(t, carry):
        scores, windows = [], []
        for u in range(A_UNROLL):
            i = t * A_UNROLL + u
            q0 = pl.multiple_of(i * QBLK, QBLK)
            ks = pl.multiple_of(jnp.clip(q0 - A_WINDOW, 0, l - nk), QBLK)
            q = jnp.concatenate([q_ref[0, pl.ds(q0, QBLK), a * HEAD_DIM:(a + 1) * HEAD_DIM]
                                 for a in range(A_GROUP)], axis=0)
            scores.append(_qk(q, k_ref[0, pl.ds(ks, nk), :]) + mask_ref[_edge_variant(i, nblk)])
            windows.append((q0, ks))
        cols = range(nk // HEAD_DIM)
        ms = []
        for s in scores:
            m = s[:, 0:HEAD_DIM]
            for c in cols[1:]:
                m = jnp.maximum(m, s[:, c * HEAD_DIM:(c + 1) * HEAD_DIM])
            m = jnp.broadcast_to(jnp.max(m, axis=-1, keepdims=True), m.shape)
            ms.append(jnp.maximum(m, es_ref[...]))
        ps = [jnp.concatenate([jnp.exp2(s[:, c * HEAD_DIM:(c + 1) * HEAD_DIM] - m) for c in cols], axis=1).astype(_bf16)
              for s, m in zip(scores, ms)]
        accs = [jnp.dot(p, v1_ref[pl.ds(ks, nk), :], preferred_element_type=_f32) for p, (_, ks) in zip(ps, windows)]
        for m, acc, (q0, _) in zip(ms, accs, windows):
            z = acc[:, HEAD_DIM:] + jnp.exp2(es_ref[...] - m)
            o = (acc[:, :HEAD_DIM] / z).astype(o_ref.dtype)
            for a in range(A_GROUP):
                o_ref[0, pl.ds(q0, QBLK), a * HEAD_DIM:(a + 1) * HEAD_DIM] = o[a * QBLK:(a + 1) * QBLK, :]
        return carry

    lax.fori_loop(0, nblk // A_UNROLL, body, 0)


def _attn_a(h, sink):
    b, l, _ = h.shape
    gw = A_GROUP * HEAD_DIM
    sink_b = jnp.broadcast_to(sink.astype(_f32).reshape(A_KV_HEADS, A_GROUP, 1), (A_KV_HEADS, A_GROUP, HEAD_DIM))
    sink_b = jnp.pad(sink_b, ((0, 0), (0, 8 - A_GROUP), (0, 0)))
    return pl.pallas_call(
        _attn_a_kernel,
        out_shape=jax.ShapeDtypeStruct((b, l, MIX_W), _bf16),
        grid=(b, A_KV_HEADS),
        in_specs=[pl.BlockSpec((1, l, gw), lambda bi, g: (bi, 0, g)),
                  pl.BlockSpec((1, l, HEAD_DIM), lambda bi, g: (bi, 0, MIX_HEADS + g)),
                  pl.BlockSpec((1, l, HEAD_DIM), lambda bi, g: (bi, 0, MIX_HEADS + A_KV_HEADS + g)),
                  pl.BlockSpec((1, 8, HEAD_DIM), lambda bi, g: (g, 0, 0))],
        out_specs=pl.BlockSpec((1, l, gw), lambda bi, g: (bi, 0, g)),
        scratch_shapes=[pltpu.VMEM((l, 2 * HEAD_DIM), _bf16),
                        pltpu.VMEM((3, A_GROUP * QBLK, 3 * A_WINDOW), _f32),
                        pltpu.VMEM((A_GROUP * QBLK, HEAD_DIM), _f32)],
        compiler_params=_cparams(2),
    )(h, h, h, sink_b)


B_UNROLL = 16


def _attn_b_kernel(q0_ref, k_ref, v_ref, q1_ref, q2_ref, o_ref,
                   kd_ref, v1_ref, tmp_ref, tmp2_ref, og_ref, lse_ref, mask_ref):
    l = o_ref.shape[1]
    nk = 4 * B_HALF_WINDOW
    q_refs = (q0_ref, q1_ref, q2_ref)
    @pl.when(_first_step())
    def _():
        _band_masks(mask_ref, nk, B_HALF_WINDOW, (0, -B_HALF_WINDOW, -2 * B_HALF_WINDOW))
        for g in range(3):
            v1_ref[g, :, HEAD_DIM:2 * HEAD_DIM] = jnp.ones((l, HEAD_DIM), _bf16)

    v1_ref[0, :, 0:HEAD_DIM] = v_ref[0]
    d1 = B_DILATIONS[1]
    assert B_DILATIONS == (1, d1, d1 * d1)
    n1, n2 = l // d1, l // (d1 * d1)
    for src, is_key in ((k_ref, True), (v_ref, False)):
        def put(g, row0, n, rows):
            if is_key:
                kd_ref[g - 1, row0:row0 + n, :] = rows.astype(_bf16)
            else:
                v1_ref[g, row0:row0 + n, 0:HEAD_DIM] = rows.astype(_bf16)

        tmp_ref[...] = src[0].astype(_f32)
        for r in range(d1):
            rows = tmp_ref[pl.ds(r, n1, stride=d1), :]
            tmp2_ref[r * n1:(r + 1) * n1, :] = rows
            put(1, r * n1, n1, rows)
        for r in range(d1 * d1):
            put(2, r * n2, n2, tmp2_ref[pl.ds((r % d1) * n1 + r // d1, n2, stride=d1), :])

    for g, dil in reversed(list(enumerate(B_DILATIONS))):
        n = l // dil
        blocks = n // QBLK

        def body(t, carry, g=g, dil=dil, n=n, blocks=blocks):
            scores, values, dests = [], [], []
            for u in range(B_UNROLL):
                bt = t * B_UNROLL + u
                r = bt // blocks
                i = bt % blocks
                q0 = pl.multiple_of(i * QBLK, QBLK)
                ks = pl.multiple_of(jnp.clip(q0 - B_HALF_WINDOW, 0, n - nk), B_HALF_WINDOW)
                base = pl.multiple_of(r * n + ks, B_HALF_WINDOW)
                if g == 0:
                    q = q0_ref[0, pl.ds(q0, QBLK), :]
                    kw = k_ref[0, pl.ds(ks, nk), :]
                else:
                    q = q_refs[g][0, r, pl.ds(q0, QBLK), :]
                    kw = kd_ref[g - 1, pl.ds(base, nk), :]
                scores.append(_qk(q, kw) + mask_ref[_edge_variant(i, blocks)])
                values.append(v1_ref[g, pl.ds(base, nk), :])
                dests.append((q0, r))
            ms, accs = _softmax_pv(scores, values)
            for m, acc, (q0, r) in zip(ms, accs, dests):
                z = acc[:, HEAD_DIM:]
                if dil > 1:
                    og_ref[g - 1, pl.ds(q0 * dil + r, QBLK, stride=dil), :] = acc[:, :HEAD_DIM] / z
                    lse_ref[g - 1, pl.ds(q0 * dil + r, QBLK, stride=dil), :] = m + jnp.log2(z)
                else:
                    l1 = lse_ref[0, pl.ds(q0, QBLK), :]
                    l2 = lse_ref[1, pl.ds(q0, QBLK), :]
                    top = jnp.maximum(jnp.maximum(m, l1), l2)
                    w0, w1, w2 = jnp.exp2(m - top), jnp.exp2(l1 - top), jnp.exp2(l2 - top)
                    num = (w0 * acc[:, :HEAD_DIM] + w1 * og_ref[0, pl.ds(q0, QBLK), :]
                           + w2 * og_ref[1, pl.ds(q0, QBLK), :])
                    o_ref[0, pl.ds(q0, QBLK), :] = (num / (w0 * z + w1 + w2)).astype(o_ref.dtype)
            return carry

        lax.fori_loop(0, dil * blocks // B_UNROLL, body, 0)


def _attn_b(h0, hq1, hq2):
    b, l, _ = h0.shape
    in_specs = [pl.BlockSpec((1, l, HEAD_DIM), lambda bi, h: (bi, 0, h)),
                pl.BlockSpec((1, l, HEAD_DIM), lambda bi, h: (bi, 0, MIX_HEADS + h)),
                pl.BlockSpec((1, l, HEAD_DIM), lambda bi, h: (bi, 0, 2 * MIX_HEADS + h))]
    for dil in B_DILATIONS[1:]:
        in_specs.append(pl.BlockSpec((1, dil, l // dil, HEAD_DIM), lambda bi, h: (bi, 0, 0, h)))
    return pl.pallas_call(
        _attn_b_kernel,
        out_shape=jax.ShapeDtypeStruct((b, l, MIX_W), _bf16),
        grid=(b, MIX_HEADS),
        in_specs=in_specs,
        out_specs=pl.BlockSpec((1, l, HEAD_DIM), lambda bi, h: (bi, 0, h)),
        scratch_shapes=[pltpu.VMEM((2, l, HEAD_DIM), _bf16),
                        pltpu.VMEM((3, l, 2 * HEAD_DIM), _bf16),
                        pltpu.VMEM((l, HEAD_DIM), _f32),
                        pltpu.VMEM((l, HEAD_DIM), _f32),
                        pltpu.VMEM((2, l, HEAD_DIM), _f32),
                        pltpu.VMEM((2, l, HEAD_DIM), _f32),
                        pltpu.VMEM((3, QBLK, 4 * B_HALF_WINDOW), _f32)],
        compiler_params=_cparams(2),
    )(h0, h0, h0, hq1, hq2)


C_KSEG = 4 * SUB
C_NSEG = 4
C_PAD = SUB
C_NKEYS = C_NSEG * C_KSEG
C_UNROLL = 4


def _attn_c_kernel(q_ref, k_ref, v_ref, bias_ref, mask_ref, o_ref, kp_ref, vp_ref, bm_ref):
    l = k_ref.shape[1]
    n_rb = l // GRID_W // NA_ROWS
    n_cb = GRID_W // NA_COLS
    n_r4 = l // GRID_W // SUB_R
    @pl.when(_first_step())
    def _():
        kp_ref[0:C_PAD, :] = jnp.zeros((C_PAD, HEAD_DIM), _bf16)
        kp_ref[C_PAD + l:C_PAD + l + C_PAD, :] = jnp.zeros((C_PAD, HEAD_DIM), _bf16)
        vp_ref[0:C_PAD, 0:HEAD_DIM] = jnp.zeros((C_PAD, HEAD_DIM), _bf16)
        vp_ref[C_PAD + l:C_PAD + l + C_PAD, 0:HEAD_DIM] = jnp.zeros((C_PAD, HEAD_DIM), _bf16)
        vp_ref[:, HEAD_DIM:2 * HEAD_DIM] = jnp.ones((l + 2 * C_PAD, HEAD_DIM), _bf16)

    @pl.when(pl.program_id(1) == 0)
    def _():
        for v in range(9):
            bm_ref[v] = bias_ref[0] + mask_ref[v]

    kp_ref[C_PAD:C_PAD + l, :] = k_ref[0]
    vp_ref[C_PAD:C_PAD + l, 0:HEAD_DIM] = v_ref[0]

    def body(t, carry):
        scores, values, dests = [], [], []
        for u in range(C_UNROLL * n_cb):
            rb = t * C_UNROLL + u // n_cb
            cb = u % n_cb
            rv = _edge_variant(rb, n_rb)
            cv = 0 if cb == 0 else (2 if cb == n_cb - 1 else 1)
            q_starts = [pl.multiple_of(((2 * rb + a) * SUBS_PER_ROW + 2 * cb) * SUB, 2 * SUB) for a in range(2)]
            q = jnp.concatenate([q_ref[0, pl.ds(qs, 2 * SUB), :] for qs in q_starts], axis=0)
            kparts, vparts = [], []
            for ar in range(C_NSEG):
                r4 = jnp.clip(2 * rb - 1 + ar, 0, n_r4 - 1)
                st = pl.multiple_of(C_PAD + (r4 * SUBS_PER_ROW + 2 * cb - 1) * SUB, SUB)
                kparts.append(kp_ref[pl.ds(st, C_KSEG), :])
                vparts.append(vp_ref[pl.ds(st, C_KSEG), :])
            scores.append(_qk(q, jnp.concatenate(kparts, axis=0)) + bm_ref[rv * 3 + cv])
            values.append(jnp.concatenate(vparts, axis=0))
            dests.append(q_starts)
        ms, accs = _softmax_pv(scores, values)
        for acc, q_starts in zip(accs, dests):
            o = (acc[:, :HEAD_DIM] / acc[:, HEAD_DIM:]).astype(o_ref.dtype)
            for a in range(2):
                o_ref[0, pl.ds(q_starts[a], 2 * SUB), :] = o[a * 2 * SUB:(a + 1) * 2 * SUB, :]
        return carry

    lax.fori_loop(0, n_rb // C_UNROLL, body, 0)


def _c_geometry():
    ql = jnp.arange(QBLK)
    qa, qc, qi, qj = ql // 64, (ql // 32) % 2, (ql // 8) % 4, ql % 8
    q_row = SUB_R * qa + qi
    q_col = SUB_C * qc + qj
    kl = jnp.arange(C_NKEYS)
    ka, kc, ki, kj = kl // C_KSEG, (kl // SUB) % 4, (kl // 8) % 4, kl % 8
    k_row = SUB_R * (ka - 1) + ki
    k_col = SUB_C * (kc - 1) + kj
    return q_row, q_col, k_row, k_col


def _c_bias(rpb):
    q_row, q_col, k_row, k_col = _c_geometry()
    nr, nc = 2 * NA_ROWS - 1, 2 * NA_COLS - 1
    dr = jnp.clip(k_row[None, :] - q_row[:, None] + NA_ROWS - 1, 0, nr - 1)
    dc = jnp.clip(k_col[None, :] - q_col[:, None] + NA_COLS - 1, 0, nc - 1)
    oh_r = (dr[:, :, None] == jnp.arange(nr)).astype(_f32)
    oh_c = (dc[:, :, None] == jnp.arange(nc)).astype(_f32)
    rows = jnp.einsum("hrc,qkr->hqkc", rpb.astype(_f32), oh_r, precision=lax.Precision.HIGHEST)
    return jnp.sum(rows * oh_c[None], axis=-1) * LOG2E


def _c_masks(rows):
    q_row, q_col, k_row, k_col = _c_geometry()
    out = []
    for rb in (0, 1, rows // NA_ROWS - 1):
        qr, kr = NA_ROWS * rb + q_row, NA_ROWS * rb + k_row
        rs = jnp.clip(qr - NA_ROWS // 2, 0, rows - NA_ROWS)
        rvalid = (kr[None, :] >= rs[:, None]) & (kr[None, :] < rs[:, None] + NA_ROWS) & (kr[None, :] >= 0) & (kr[None, :] < rows)
        for cb in (0, 1, GRID_W // NA_COLS - 1):
            qc, kc = NA_COLS * cb + q_col, NA_COLS * cb + k_col
            cs = jnp.clip(qc - NA_COLS // 2, 0, GRID_W - NA_COLS)
            cvalid = (kc[None, :] >= cs[:, None]) & (kc[None, :] < cs[:, None] + NA_COLS) & (kc[None, :] >= 0) & (kc[None, :] < GRID_W)
            out.append(jnp.where(rvalid & cvalid, 0.0, NEG_INF).astype(_f32))
    return jnp.stack(out)


def _attn_c(h, rpb):
    b, l, _ = h.shape
    bias = _c_bias(rpb)
    masks = _c_masks(l // GRID_W)
    return pl.pallas_call(
        _attn_c_kernel,
        out_shape=jax.ShapeDtypeStruct((b, l, MIX_W), _bf16),
        grid=(MIX_HEADS, b),
        in_specs=[pl.BlockSpec((1, l, HEAD_DIM), lambda h, bi: (bi, 0, h)),
                  pl.BlockSpec((1, l, HEAD_DIM), lambda h, bi: (bi, 0, MIX_HEADS + h)),
                  pl.BlockSpec((1, l, HEAD_DIM), lambda h, bi: (bi, 0, 2 * MIX_HEADS + h)),
                  pl.BlockSpec((1, QBLK, C_NKEYS), lambda h, bi: (h, 0, 0)),
                  pl.BlockSpec((9, QBLK, C_NKEYS), lambda h, bi: (0, 0, 0))],
        out_specs=pl.BlockSpec((1, l, HEAD_DIM), lambda h, bi: (bi, 0, h)),
        scratch_shapes=[pltpu.VMEM((l + 2 * C_PAD, HEAD_DIM), _bf16),
                        pltpu.VMEM((l + 2 * C_PAD, 2 * HEAD_DIM), _bf16),
                        pltpu.VMEM((9, QBLK, C_NKEYS), _f32)],
        compiler_params=_cparams(2),
    )(h, h, h, bias, masks)


def _subblock_chunks(tm):
    grp = SUB * SUBS_PER_ROW
    pairs = []
    for g in range(tm // grp):
        for c8 in range(SUBS_PER_ROW):
            for i4 in range(SUB_R):
                pairs.append((g * grp + i4 * GRID_W + c8 * SUB_C, (g * SUBS_PER_ROW + c8) * SUB + i4 * SUB_C))
    return pairs


def _out_kernel(om_ref, glo_ref, ghi_ref, mq_ref, kv_ref, x_ref, w_ref, g_ref, b_ref, o_ref, xs_ref, *, subblock):
    half = D_INNER // 2
    ones = jnp.ones((kv_ref.shape[1], HEAD_DIM), _bf16)
    values = [jnp.concatenate([kv_ref[0, :, MEM_W + h * HEAD_DIM:MEM_W + (h + 1) * HEAD_DIM], ones], axis=1)
              for h in range(MEM_HEADS)]

    n_slices = D_INNER // MEM_W
    part_rows = OUT_SUB // n_slices

    def layer_norm_part(r0, q, z_in):
        lo = q * part_rows
        xin = (xs_ref if subblock else x_ref.at[0])[r0 + lo:r0 + lo + part_rows, :]
        z = ALPHA * xin + z_in[lo:lo + part_rows, :]
        mu = jnp.mean(z, axis=-1, keepdims=True)
        zc = z - mu
        var = jnp.mean(zc * zc, axis=-1, keepdims=True)
        out = zc * lax.rsqrt(var + LN_EPS) * g_ref[...] + b_ref[...]
        if subblock:
            for nat, sub in _subblock_chunks(OUT_SUB):
                if lo <= sub < lo + part_rows:
                    o_ref[0, r0 + nat:r0 + nat + SUB_C, :] = out[sub - lo:sub - lo + SUB_C, :]
        else:
            o_ref[0, r0 + lo:r0 + lo + part_rows, :] = out

    pending = None
    for r0 in range(0, om_ref.shape[1], OUT_SUB):
        rows = slice(r0, r0 + OUT_SUB)
        if subblock:
            for nat, sub in _subblock_chunks(OUT_SUB):
                xs_ref[r0 + sub:r0 + sub + SUB_C, :] = x_ref[0, r0 + nat:r0 + nat + SUB_C, :]

        def silu_gate(c0, c1, rows=rows):
            ref, off = (glo_ref, 0) if c0 < half else (ghi_ref, half)
            hg = 0.5 * ref[0, rows, c0 - off:c1 - off].astype(_f32)
            return hg + hg * jnp.tanh(hg)

        scores = [_qk(mq_ref[0, rows, h * HEAD_DIM:(h + 1) * HEAD_DIM],
                      kv_ref[0, :, h * HEAD_DIM:(h + 1) * HEAD_DIM]) * QSCALE for h in range(MEM_HEADS)]
        _, accs = _softmax_pv(scores, values)
        branch = None
        for q, c0 in enumerate(range(0, D_INNER, MEM_W)):
            if c0 < MIX_W:
                y = om_ref[0, rows, c0:c0 + MEM_W].astype(_f32) * silu_gate(c0, c0 + MEM_W)
            else:
                y = jnp.concatenate([acc[:, :HEAD_DIM] / acc[:, HEAD_DIM:] for acc in accs], axis=1)
                y = y * silu_gate(c0, c0 + MEM_W)
            part = jnp.dot(y.astype(_bf16), w_ref[c0:c0 + MEM_W, :], preferred_element_type=_f32)
            branch = part if branch is None else branch + part
            if pending is not None:
                layer_norm_part(pending[0], q, pending[1])
        pending = (r0, branch)
    for q in range(n_slices):
        layer_norm_part(pending[0], q, pending[1])


def _out(o_mix, h, mq_col, kv, x, w_out, ln_g, ln_b, *, subblock):
    b, l, d = x.shape
    tm = OUT_TM
    mlen = kv.shape[1]
    half = D_INNER // 2
    gate_col = mq_col + MEM_W
    assert mq_col % MEM_W == 0 and gate_col % half == 0
    kern = functools.partial(_out_kernel, subblock=subblock)
    return pl.pallas_call(
        kern,
        out_shape=jax.ShapeDtypeStruct((b, l, d), _f32),
        grid=(b, l // tm),
        in_specs=[pl.BlockSpec((1, tm, MIX_W), lambda bi, i: (bi, i, 0)),
                  pl.BlockSpec((1, tm, half), lambda bi, i: (bi, i, gate_col // half)),
                  pl.BlockSpec((1, tm, half), lambda bi, i: (bi, i, gate_col // half + 1)),
                  pl.BlockSpec((1, tm, MEM_W), lambda bi, i: (bi, i, mq_col // MEM_W)),
                  pl.BlockSpec((1, mlen, 2 * MEM_W), lambda bi, i: (bi, 0, 0)),
                  pl.BlockSpec((1, tm, d), lambda bi, i: (bi, i, 0)),
                  pl.BlockSpec((D_INNER, d), lambda bi, i: (0, 0)),
                  pl.BlockSpec((1, d), lambda bi, i: (0, 0)),
                  pl.BlockSpec((1, d), lambda bi, i: (0, 0))],
        out_specs=pl.BlockSpec((1, tm, d), lambda bi, i: (bi, i, 0)),
        scratch_shapes=[pltpu.VMEM((tm, d), _f32)],
        compiler_params=_cparams(2),
    )(o_mix, h, h, h, kv, x, w_out, ln_g.reshape(1, d), ln_b.reshape(1, d))


def _split_cols(w, sizes):
    out, c = [], 0
    for s in sizes:
        out.append(w[:, c:c + s])
        c += s
    return out


def kernel(x, mem, w_in_a, sink_a, w_in_b, w_in_c, rpb_c, w_mkv, w_out, ln_g, ln_b):
    b, l, d = x.shape
    mlen = mem.shape[1]
    kv_all = _memkv(mem.reshape(b * mlen, d), w_mkv.astype(_bf16)).reshape(DEPTH, b, mlen, 2 * MEM_W)
    tables = _rope_tables(l)

    for i in range(DEPTH):
        kind, j = i % NUM_MIXERS, i // NUM_MIXERS
        tail = MEM_W + D_INNER
        if kind == 0:
            seg = ((1, ((MIX_W, "rope_q"), (MEM_W, "rope_k"), (MEM_W + tail, "plain"))),)
            h, = _proj(x, w_in_a[j].astype(_bf16), tables, seg)
            o_mix = _attn_a(h, sink_a[j])
            mq_col = MIX_W + 2 * MEM_W
        elif kind == 1:
            wq0, wq1, wq2, wrest = _split_cols(w_in_b[j].astype(_bf16), (MIX_W, MIX_W, MIX_W, 2 * MIX_W + tail))
            seg = ((B_DILATIONS[2], ((MIX_W, "rope_q"),)),
                   (B_DILATIONS[1], ((MIX_W, "rope_q"),)),
                   (1, ((MIX_W, "rope_q"), (MIX_W, "rope_k"), (MIX_W + tail, "plain"))))
            hq2, hq1, h = _proj(x, jnp.concatenate([wq2, wq1, wq0, wrest], axis=1), tables, seg)
            o_mix = _attn_b(h, hq1, hq2)
            mq_col = 3 * MIX_W
        else:
            seg = ((1, ((MIX_W, "scale"), (2 * MIX_W + tail, "plain"))),)
            h, = _proj(x, w_in_c[j].astype(_bf16), tables, seg, subblock=True)
            o_mix = _attn_c(h, rpb_c[j])
            mq_col = 3 * MIX_W
        x = _out(o_mix, h, mq_col, kv_all[i], x, w_out[i].astype(_bf16), ln_g[i], ln_b[i], subblock=(kind == 2))
    return x
```

```python
import functools
import math

import jax
import jax.numpy as jnp
from jax import lax
from jax.experimental import pallas as pl
from jax.experimental.pallas import tpu as pltpu

D_MODEL = 1024
DEPTH = 4
NUM_MIXERS = 3
HEAD_DIM = 128
D_INNER = 2 * D_MODEL
MEM_HEADS = 4
MIX_HEADS = D_INNER // HEAD_DIM - MEM_HEADS
A_KV_HEADS = MIX_HEADS // 3
A_GROUP = MIX_HEADS // A_KV_HEADS
A_WINDOW = 128
B_DILATIONS = (1, 4, 16)
B_HALF_WINDOW = 64
NA_ROWS = 8
NA_COLS = 16
GRID_W = 64
ROPE_THETA = 500000.0
ROPE_DIMS = HEAD_DIM // 4
ROPE_HALF = ROPE_DIMS // 2
LN_EPS = 1e-5
ALPHA = (2 * DEPTH) ** 0.25
NEG_INF = -1e30
LOG2E = math.log2(math.e)
QSCALE = HEAD_DIM ** -0.5 * LOG2E

MIX_W = MIX_HEADS * HEAD_DIM
MEM_W = MEM_HEADS * HEAD_DIM

SUB_R = 4
SUB_C = 8
SUB = SUB_R * SUB_C
SUBS_PER_ROW = GRID_W // SUB_C
QBLK = 128

VMEM_LIMIT = 56 * 1024 * 1024
PROJ_TM = 512
PROJ_TN = 512
OUT_TM = 1024
OUT_SUB = 512

_f32 = jnp.float32
_bf16 = jnp.bfloat16


def _cparams(n_grid):
    return pltpu.CompilerParams(dimension_semantics=("arbitrary",) * n_grid,
                                vmem_limit_bytes=VMEM_LIMIT)


def _rope(blk, cos, sin):
    lane = lax.broadcasted_iota(jnp.int32, blk.shape, 1)
    up = pltpu.roll(blk, HEAD_DIM - ROPE_HALF, 1)
    down = pltpu.roll(blk, ROPE_HALF, 1)
    swapped = jnp.where(lane < ROPE_HALF, up, down)
    return blk * cos + swapped * sin


def _proj_kernel(*refs, subblock, segments, has_tables):
    x_ref, w_ref = refs[:2]
    pos = 2
    if has_tables:
        cos_ref, sin_ref = refs[2:4]
        pos = 4
    out_refs = refs[pos:pos + len(segments)]
    xs_ref, tmp_ref = refs[pos + len(segments):]
    tm = xs_ref.shape[0]
    tn = PROJ_TN

    if subblock:
        grp = SUB * SUBS_PER_ROW
        for g in range(tm // grp):
            for c8 in range(SUBS_PER_ROW):
                parts = [x_ref[0, g * grp + i4 * GRID_W + c8 * SUB_C:
                               g * grp + i4 * GRID_W + (c8 + 1) * SUB_C, :] for i4 in range(SUB_R)]
                dst = (g * SUBS_PER_ROW + c8) * SUB
                xs_ref[dst:dst + SUB, :] = jnp.concatenate(parts, axis=0).astype(_bf16)
    else:
        xs_ref[...] = x_ref[0].astype(_bf16)

    def write(o_ref, dil, val, c0):
        c1 = c0 + HEAD_DIM
        if dil == 1:
            o_ref[0, :, c0:c1] = val.astype(o_ref.dtype)
            return
        n = tm // dil
        tmp_ref[0] = val
        if dil == B_DILATIONS[2]:
            d1 = B_DILATIONS[1]
            n1 = tm // d1
            for r in range(d1):
                tmp_ref[1, r * n1:(r + 1) * n1, :] = tmp_ref[0, pl.ds(r, n1, stride=d1), :]
            for r in range(dil):
                rows = tmp_ref[1, pl.ds((r % d1) * n1 + r // d1, n, stride=d1), :]
                o_ref[0, r, :, c0:c1] = rows.astype(o_ref.dtype)
        else:
            for r in range(dil):
                o_ref[0, r, :, c0:c1] = tmp_ref[0, pl.ds(r, n, stride=dil), :].astype(o_ref.dtype)

    wcol = 0
    for o_ref, (dil, parts) in zip(out_refs, segments):
        ocol = 0
        for n_cols, mode in parts:
            for _ in range(n_cols // tn):
                acc = jnp.dot(xs_ref[...], w_ref[:, wcol:wcol + tn], preferred_element_type=_f32)
                for hd in range(tn // HEAD_DIM):
                    val = acc[:, hd * HEAD_DIM:(hd + 1) * HEAD_DIM]
                    if mode == "rope_q":
                        val = _rope(val, cos_ref[0], sin_ref[0])
                    elif mode == "rope_k":
                        val = _rope(val, cos_ref[1], sin_ref[1])
                    elif mode == "scale":
                        val = val * QSCALE
                    write(o_ref, dil, val, ocol + hd * HEAD_DIM)
                wcol += tn
                ocol += tn


def _proj(x, w, tables, segments, *, subblock=False):
    b, l, d = x.shape
    tm, tn = PROJ_TM, PROJ_TN
    tiles = l // tm
    assert l % tm == 0
    widths = [sum(n for n, _ in parts) for _, parts in segments]
    assert sum(widths) == w.shape[1] and all(n % tn == 0 for _, parts in segments for n, _ in parts)
    out_shapes, out_specs = [], []
    for (dil, _), n in zip(segments, widths):
        if dil == 1:
            out_shapes.append(jax.ShapeDtypeStruct((b, l, n), _bf16))
            out_specs.append(pl.BlockSpec((1, tm, n), lambda i: (i // tiles, i % tiles, 0)))
        else:
            out_shapes.append(jax.ShapeDtypeStruct((b, dil, l // dil, n), _bf16))
            out_specs.append(pl.BlockSpec((1, dil, tm // dil, n), lambda i: (i // tiles, 0, i % tiles, 0)))
    in_specs = [pl.BlockSpec((1, tm, d), lambda i: (i // tiles, i % tiles, 0)),
                pl.BlockSpec((d, w.shape[1]), lambda i: (0, 0), pipeline_mode=pl.Buffered(1))]
    args = [x, w]
    has_tables = any(mode.startswith("rope") for _, parts in segments for _, mode in parts)
    if has_tables:
        tab_spec = pl.BlockSpec((2, tm, HEAD_DIM), lambda i: (0, i % tiles, 0))
        in_specs += [tab_spec, tab_spec]
        args += list(tables)
    kern = functools.partial(_proj_kernel, subblock=subblock, segments=segments, has_tables=has_tables)
    return pl.pallas_call(
        kern,
        out_shape=out_shapes,
        grid=(b * tiles,),
        in_specs=in_specs,
        out_specs=out_specs,
        scratch_shapes=[pltpu.VMEM((tm, d), _bf16), pltpu.VMEM((2, tm, HEAD_DIM), _f32)],
        compiler_params=_cparams(1),
    )(*args)


def _rope_tables(l):
    inv = ROPE_THETA ** (-jnp.arange(ROPE_HALF, dtype=_f32) / ROPE_HALF)
    ang = jnp.arange(l).astype(_f32)[:, None] * inv[None, :]
    cos, sin = jnp.cos(ang), jnp.sin(ang)
    pad = HEAD_DIM - ROPE_DIMS
    cos_t = jnp.concatenate([cos, cos, jnp.ones((l, pad), _f32)], axis=1)
    sin_t = jnp.concatenate([-sin, sin, jnp.zeros((l, pad), _f32)], axis=1)
    return jnp.stack([cos_t * QSCALE, cos_t]), jnp.stack([sin_t * QSCALE, sin_t])


def _memkv_kernel(m_ref, w_ref, o_ref):
    o_ref[0] = jnp.dot(m_ref[...].astype(_bf16), w_ref[0], preferred_element_type=_f32).astype(o_ref.dtype)


def _memkv(mem2d, w_mkv):
    rows, d = mem2d.shape
    depth, _, n = w_mkv.shape
    tm = min(rows, PROJ_TM)
    assert rows % tm == 0
    return pl.pallas_call(
        _memkv_kernel,
        out_shape=jax.ShapeDtypeStruct((depth, rows, n), _bf16),
        grid=(depth, rows // tm),
        in_specs=[pl.BlockSpec((tm, d), lambda li, i: (i, 0)),
                  pl.BlockSpec((1, d, n), lambda li, i: (li, 0, 0))],
        out_specs=pl.BlockSpec((1, tm, n), lambda li, i: (li, i, 0)),
        compiler_params=_cparams(2),
    )(mem2d, w_mkv)


def _qk(q, k):
    return lax.dot_general(q, k, (((1,), (1,)), ((), ())), preferred_element_type=_f32)


def _first_step():
    return (pl.program_id(0) == 0) & (pl.program_id(1) == 0)


def _band_masks(mask_ref, nk, half, deltas):
    rel = (lax.broadcasted_iota(jnp.int32, (QBLK, nk), 1) - lax.broadcasted_iota(jnp.int32, (QBLK, nk), 0))
    for v, delta in enumerate(deltas):
        mask_ref[v] = jnp.where(jnp.abs(rel + delta) <= half, 0.0, NEG_INF).astype(_f32)


def _edge_variant(i, n):
    return jnp.where(i == 0, 0, jnp.where(i == n - 1, 2, 1))


def _softmax_pv(scores, values, floors=None):
    def rowmax(s):
        m = s[:, 0:HEAD_DIM]
        for c in range(1, s.shape[1] // HEAD_DIM):
            m = jnp.maximum(m, s[:, c * HEAD_DIM:(c + 1) * HEAD_DIM])
        return jnp.max(m, axis=-1, keepdims=True)

    ms = [rowmax(s) for s in scores]
    if floors is not None:
        ms = [jnp.maximum(m, f) for m, f in zip(ms, floors)]
    ps = [jnp.exp2(s - m).astype(_bf16) for s, m in zip(scores, ms)]
    accs = [jnp.dot(p, v, preferred_element_type=_f32) for p, v in zip(ps, values)]
    return ms, accs


A_UNROLL = 4


def _attn_a_kernel(q_ref, k_ref, v_ref, sink_ref, o_ref, v1_ref, mask_ref, es_ref):
    l = k_ref.shape[1]
    nk = 3 * A_WINDOW
    nblk = l // QBLK
    v1_ref[:, 0:HEAD_DIM] = v_ref[0]

    @pl.when(_first_step())
    def _():
        v1_ref[:, HEAD_DIM:2 * HEAD_DIM] = jnp.ones((l, HEAD_DIM), _bf16)
        rel = (lax.broadcasted_iota(jnp.int32, (QBLK, nk), 1) - lax.broadcasted_iota(jnp.int32, (QBLK, nk), 0))
        for v, delta in enumerate((0, -A_WINDOW, -2 * A_WINDOW)):
            blk = jnp.where(jnp.abs(rel + delta) <= A_WINDOW, 0.0, NEG_INF).astype(_f32)
            for a in range(A_GROUP):
                mask_ref[v, a * QBLK:(a + 1) * QBLK, :] = blk

    for a in range(A_GROUP):
        es_ref[a * QBLK:(a + 1) * QBLK, :] = jnp.broadcast_to(sink_ref[0, a:a + 1, :] * LOG2E, (QBLK, HEAD_DIM))

    def body(t, carry):
        scores, windows = [], []
        for u in range(A_UNROLL):
            i = t * A_UNROLL + u
            q0 = pl.multiple_of(i * QBLK, QBLK)
            ks = pl.multiple_of(jnp.clip(q0 - A_WINDOW, 0, l - nk), QBLK)
            q = jnp.concatenate([q_ref[0, pl.ds(q0, QBLK), a * HEAD_DIM:(a + 1) * HEAD_DIM]
                                 for a in range(A_GROUP)], axis=0)
            scores.append(_qk(q, k_ref[0, pl.ds(ks, nk), :]) + mask_ref[_edge_variant(i, nblk)])
            windows.append((q0, ks))
        cols = range(nk // HEAD_DIM)
        ms = []
        for s in scores:
            m = s[:, 0:HEAD_DIM]
            for c in cols[1:]:
                m = jnp.maximum(m, s[:, c * HEAD_DIM:(c + 1) * HEAD_DIM])
            m = jnp.broadcast_to(jnp.max(m, axis=-1, keepdims=True), m.shape)
            ms.append(jnp.maximum(m, es_ref[...]))
        ps = [jnp.concatenate([jnp.exp2(s[:, c * HEAD_DIM:(c + 1) * HEAD_DIM] - m) for c in cols], axis=1).astype(_bf16)
              for s, m in zip(scores, ms)]
        accs = [jnp.dot(p, v1_ref[pl.ds(ks, nk), :], preferred_element_type=_f32) for p, (_, ks) in zip(ps, windows)]
        for m, acc, (q0, _) in zip(ms, accs, windows):
            z = acc[:, HEAD_DIM:] + jnp.exp2(es_ref[...] - m)
            o = (acc[:, :HEAD_DIM] / z).astype(o_ref.dtype)
            for a in range(A_GROUP):
                o_ref[0, pl.ds(q0, QBLK), a * HEAD_DIM:(a + 1) * HEAD_DIM] = o[a * QBLK:(a + 1) * QBLK, :]
        return carry

    lax.fori_loop(0, nblk // A_UNROLL, body, 0)


def _attn_a(h, sink):
    b, l, _ = h.shape
    gw = A_GROUP * HEAD_DIM
    sink_b = jnp.broadcast_to(sink.astype(_f32).reshape(A_KV_HEADS, A_GROUP, 1), (A_KV_HEADS, A_GROUP, HEAD_DIM))
    sink_b = jnp.pad(sink_b, ((0, 0), (0, 8 - A_GROUP), (0, 0)))
    return pl.pallas_call(
        _attn_a_kernel,
        out_shape=jax.ShapeDtypeStruct((b, l, MIX_W), _bf16),
        grid=(b, A_KV_HEADS),
        in_specs=[pl.BlockSpec((1, l, gw), lambda bi, g: (bi, 0, g)),
                  pl.BlockSpec((1, l, HEAD_DIM), lambda bi, g: (bi, 0, MIX_HEADS + g)),
                  pl.BlockSpec((1, l, HEAD_DIM), lambda bi, g: (bi, 0, MIX_HEADS + A_KV_HEADS + g)),
                  pl.BlockSpec((1, 8, HEAD_DIM), lambda bi, g: (g, 0, 0))],
        out_specs=pl.BlockSpec((1, l, gw), lambda bi, g: (bi, 0, g)),
        scratch_shapes=[pltpu.VMEM((l, 2 * HEAD_DIM), _bf16),
                        pltpu.VMEM((3, A_GROUP * QBLK, 3 * A_WINDOW), _f32),
                        pltpu.VMEM((A_GROUP * QBLK, HEAD_DIM), _f32)],
        compiler_params=_cparams(2),
    )(h, h, h, sink_b)


B_UNROLL = 16


def _attn_b_kernel(q0_ref, k_ref, v_ref, q1_ref, q2_ref, o_ref,
                   kd_ref, v1_ref, tmp_ref, tmp2_ref, og_ref, lse_ref, mask_ref):
    l = o_ref.shape[1]
    nk = 4 * B_HALF_WINDOW
    q_refs = (q0_ref, q1_ref, q2_ref)
    @pl.when(_first_step())
    def _():
        _band_masks(mask_ref, nk, B_HALF_WINDOW, (0, -B_HALF_WINDOW, -2 * B_HALF_WINDOW))
        for g in range(3):
            v1_ref[g, :, HEAD_DIM:2 * HEAD_DIM] = jnp.ones((l, HEAD_DIM), _bf16)

    v1_ref[0, :, 0:HEAD_DIM] = v_ref[0]
    d1 = B_DILATIONS[1]
    assert B_DILATIONS == (1, d1, d1 * d1)
    n1, n2 = l // d1, l // (d1 * d1)
    for src, is_key in ((k_ref, True), (v_ref, False)):
        def put(g, row0, n, rows):
            if is_key:
                kd_ref[g - 1, row0:row0 + n, :] = rows.astype(_bf16)
            else:
                v1_ref[g, row0:row0 + n, 0:HEAD_DIM] = rows.astype(_bf16)

        tmp_ref[...] = src[0].astype(_f32)
        for r in range(d1):
            rows = tmp_ref[pl.ds(r, n1, stride=d1), :]
            tmp2_ref[r * n1:(r + 1) * n1, :] = rows
            put(1, r * n1, n1, rows)
        for r in range(d1 * d1):
            put(2, r * n2, n2, tmp2_ref[pl.ds((r % d1) * n1 + r // d1, n2, stride=d1), :])

    for g, dil in reversed(list(enumerate(B_DILATIONS))):
        n = l // dil
        blocks = n // QBLK

        def body(t, carry, g=g, dil=dil, n=n, blocks=blocks):
            scores, values, dests = [], [], []
            for u in range(B_UNROLL):
                bt = t * B_UNROLL + u
                r = bt // blocks
                i = bt % blocks
                q0 = pl.multiple_of(i * QBLK, QBLK)
                ks = pl.multiple_of(jnp.clip(q0 - B_HALF_WINDOW, 0, n - nk), B_HALF_WINDOW)
                base = pl.multiple_of(r * n + ks, B_HALF_WINDOW)
                if g == 0:
                    q = q0_ref[0, pl.ds(q0, QBLK), :]
                    kw = k_ref[0, pl.ds(ks, nk), :]
                else:
                    q = q_refs[g][0, r, pl.ds(q0, QBLK), :]
                    kw = kd_ref[g - 1, pl.ds(base, nk), :]
                scores.append(_qk(q, kw) + mask_ref[_edge_variant(i, blocks)])
                values.append(v1_ref[g, pl.ds(base, nk), :])
                dests.append((q0, r))
            ms, accs = _softmax_pv(scores, values)
            for m, acc, (q0, r) in zip(ms, accs, dests):
                z = acc[:, HEAD_DIM:]
                if g == 2:
                    dst = pl.ds((r % d1) * n1 + q0 * d1 + r // d1, QBLK, stride=d1)
                    og_ref[0, dst, :] = acc[:, :HEAD_DIM] / z
                    lse_ref[0, dst, :] = m + jnp.log2(z)
                    continue
                prev = pl.ds(pl.multiple_of(r * n + q0, QBLK), QBLK) if g == 1 else pl.ds(q0, QBLK)
                lp = lse_ref[g - 1 if g else 1, prev, :]
                top = jnp.maximum(m, lp)
                w, wp = jnp.exp2(m - top), jnp.exp2(lp - top)
                num = w * acc[:, :HEAD_DIM] + wp * og_ref[g - 1 if g else 1, prev, :]
                den = w * z + wp
                if g == 1:
                    dst = pl.ds(q0 * dil + r, QBLK, stride=dil)
                    og_ref[1, dst, :] = num / den
                    lse_ref[1, dst, :] = top + jnp.log2(den)
                else:
                    o_ref[0, pl.ds(q0, QBLK), :] = (num / den).astype(o_ref.dtype)
            return carry

        lax.fori_loop(0, dil * blocks // B_UNROLL, body, 0)


def _attn_b(h0, hq1, hq2):
    b, l, _ = h0.shape
    in_specs = [pl.BlockSpec((1, l, HEAD_DIM), lambda bi, h: (bi, 0, h)),
                pl.BlockSpec((1, l, HEAD_DIM), lambda bi, h: (bi, 0, MIX_HEADS + h)),
                pl.BlockSpec((1, l, HEAD_DIM), lambda bi, h: (bi, 0, 2 * MIX_HEADS + h))]
    for dil in B_DILATIONS[1:]:
        in_specs.append(pl.BlockSpec((1, dil, l // dil, HEAD_DIM), lambda bi, h: (bi, 0, 0, h)))
    return pl.pallas_call(
        _attn_b_kernel,
        out_shape=jax.ShapeDtypeStruct((b, l, MIX_W), _bf16),
        grid=(b, MIX_HEADS),
        in_specs=in_specs,
        out_specs=pl.BlockSpec((1, l, HEAD_DIM), lambda bi, h: (bi, 0, h)),
        scratch_shapes=[pltpu.VMEM((2, l, HEAD_DIM), _bf16),
                        pltpu.VMEM((3, l, 2 * HEAD_DIM), _bf16),
                        pltpu.VMEM((l, HEAD_DIM), _f32),
                        pltpu.VMEM((l, HEAD_DIM), _f32),
                        pltpu.VMEM((2, l, HEAD_DIM), _f32),
                        pltpu.VMEM((2, l, HEAD_DIM), _f32),
                        pltpu.VMEM((3, QBLK, 4 * B_HALF_WINDOW), _f32)],
        compiler_params=_cparams(2),
    )(h0, h0, h0, hq1, hq2)


C_KSEG = 4 * SUB
C_NSEG = 4
C_PAD = SUB
C_NKEYS = C_NSEG * C_KSEG
C_UNROLL = 4


def _attn_c_kernel(q_ref, k_ref, v_ref, bias_ref, mask_ref, o_ref, kp_ref, vp_ref, bm_ref):
    l = k_ref.shape[1]
    n_rb = l // GRID_W // NA_ROWS
    n_cb = GRID_W // NA_COLS
    n_r4 = l // GRID_W // SUB_R
    @pl.when(_first_step())
    def _():
        kp_ref[0:C_PAD, :] = jnp.zeros((C_PAD, HEAD_DIM), _bf16)
        kp_ref[C_PAD + l:C_PAD + l + C_PAD, :] = jnp.zeros((C_PAD, HEAD_DIM), _bf16)
        vp_ref[0:C_PAD, 0:HEAD_DIM] = jnp.zeros((C_PAD, HEAD_DIM), _bf16)
        vp_ref[C_PAD + l:C_PAD + l + C_PAD, 0:HEAD_DIM] = jnp.zeros((C_PAD, HEAD_DIM), _bf16)
        vp_ref[:, HEAD_DIM:2 * HEAD_DIM] = jnp.ones((l + 2 * C_PAD, HEAD_DIM), _bf16)

    @pl.when(pl.program_id(1) == 0)
    def _():
        for v in range(9):
            bm_ref[v] = bias_ref[0] + mask_ref[v]

    kp_ref[C_PAD:C_PAD + l, :] = k_ref[0]
    vp_ref[C_PAD:C_PAD + l, 0:HEAD_DIM] = v_ref[0]

    def body(t, carry):
        scores, values, dests = [], [], []
        for u in range(C_UNROLL * n_cb):
            rb = t * C_UNROLL + u // n_cb
            cb = u % n_cb
            rv = _edge_variant(rb, n_rb)
            cv = 0 if cb == 0 else (2 if cb == n_cb - 1 else 1)
            q_starts = [pl.multiple_of(((2 * rb + a) * SUBS_PER_ROW + 2 * cb) * SUB, 2 * SUB) for a in range(2)]
            q = jnp.concatenate([q_ref[0, pl.ds(qs, 2 * SUB), :] for qs in q_starts], axis=0)
            kparts, vparts = [], []
            for ar in range(C_NSEG):
                r4 = jnp.clip(2 * rb - 1 + ar, 0, n_r4 - 1)
                st = pl.multiple_of(C_PAD + (r4 * SUBS_PER_ROW + 2 * cb - 1) * SUB, SUB)
                kparts.append(kp_ref[pl.ds(st, C_KSEG), :])
                vparts.append(vp_ref[pl.ds(st, C_KSEG), :])
            scores.append(_qk(q, jnp.concatenate(kparts, axis=0)) + bm_ref[rv * 3 + cv])
            values.append(jnp.concatenate(vparts, axis=0))
            dests.append(q_starts)
        ms, accs = _softmax_pv(scores, values)
        for acc, q_starts in zip(accs, dests):
            o = (acc[:, :HEAD_DIM] / acc[:, HEAD_DIM:]).astype(o_ref.dtype)
            for a in range(2):
                o_ref[0, pl.ds(q_starts[a], 2 * SUB), :] = o[a * 2 * SUB:(a + 1) * 2 * SUB, :]
        return carry

    lax.fori_loop(0, n_rb // C_UNROLL, body, 0)


def _c_geometry():
    ql = jnp.arange(QBLK)
    qa, qc, qi, qj = ql // 64, (ql // 32) % 2, (ql // 8) % 4, ql % 8
    q_row = SUB_R * qa + qi
    q_col = SUB_C * qc + qj
    kl = jnp.arange(C_NKEYS)
    ka, kc, ki, kj = kl // C_KSEG, (kl // SUB) % 4, (kl // 8) % 4, kl % 8
    k_row = SUB_R * (ka - 1) + ki
    k_col = SUB_C * (kc - 1) + kj
    return q_row, q_col, k_row, k_col


def _c_bias(rpb):
    q_row, q_col, k_row, k_col = _c_geometry()
    nr, nc = 2 * NA_ROWS - 1, 2 * NA_COLS - 1
    dr = jnp.clip(k_row[None, :] - q_row[:, None] + NA_ROWS - 1, 0, nr - 1)
    dc = jnp.clip(k_col[None, :] - q_col[:, None] + NA_COLS - 1, 0, nc - 1)
    oh_r = (dr[:, :, None] == jnp.arange(nr)).astype(_f32)
    oh_c = (dc[:, :, None] == jnp.arange(nc)).astype(_f32)
    rows = jnp.einsum("hrc,qkr->hqkc", rpb.astype(_f32), oh_r, precision=lax.Precision.HIGHEST)
    return jnp.sum(rows * oh_c[None], axis=-1) * LOG2E


def _c_masks(rows):
    q_row, q_col, k_row, k_col = _c_geometry()
    out = []
    for rb in (0, 1, rows // NA_ROWS - 1):
        qr, kr = NA_ROWS * rb + q_row, NA_ROWS * rb + k_row
        rs = jnp.clip(qr - NA_ROWS // 2, 0, rows - NA_ROWS)
        rvalid = (kr[None, :] >= rs[:, None]) & (kr[None, :] < rs[:, None] + NA_ROWS) & (kr[None, :] >= 0) & (kr[None, :] < rows)
        for cb in (0, 1, GRID_W // NA_COLS - 1):
            qc, kc = NA_COLS * cb + q_col, NA_COLS * cb + k_col
            cs = jnp.clip(qc - NA_COLS // 2, 0, GRID_W - NA_COLS)
            cvalid = (kc[None, :] >= cs[:, None]) & (kc[None, :] < cs[:, None] + NA_COLS) & (kc[None, :] >= 0) & (kc[None, :] < GRID_W)
            out.append(jnp.where(rvalid & cvalid, 0.0, NEG_INF).astype(_f32))
    return jnp.stack(out)


def _attn_c(h, rpb):
    b, l, _ = h.shape
    bias = _c_bias(rpb)
    masks = _c_masks(l // GRID_W)
    return pl.pallas_call(
        _attn_c_kernel,
        out_shape=jax.ShapeDtypeStruct((b, l, MIX_W), _bf16),
        grid=(MIX_HEADS, b),
        in_specs=[pl.BlockSpec((1, l, HEAD_DIM), lambda h, bi: (bi, 0, h)),
                  pl.BlockSpec((1, l, HEAD_DIM), lambda h, bi: (bi, 0, MIX_HEADS + h)),
                  pl.BlockSpec((1, l, HEAD_DIM), lambda h, bi: (bi, 0, 2 * MIX_HEADS + h)),
                  pl.BlockSpec((1, QBLK, C_NKEYS), lambda h, bi: (h, 0, 0)),
                  pl.BlockSpec((9, QBLK, C_NKEYS), lambda h, bi: (0, 0, 0))],
        out_specs=pl.BlockSpec((1, l, HEAD_DIM), lambda h, bi: (bi, 0, h)),
        scratch_shapes=[pltpu.VMEM((l + 2 * C_PAD, HEAD_DIM), _bf16),
                        pltpu.VMEM((l + 2 * C_PAD, 2 * HEAD_DIM), _bf16),
                        pltpu.VMEM((9, QBLK, C_NKEYS), _f32)],
        compiler_params=_cparams(2),
    )(h, h, h, bias, masks)


def _subblock_chunks(tm):
    grp = SUB * SUBS_PER_ROW
    pairs = []
    for g in range(tm // grp):
        for c8 in range(SUBS_PER_ROW):
            for i4 in range(SUB_R):
                pairs.append((g * grp + i4 * GRID_W + c8 * SUB_C, (g * SUBS_PER_ROW + c8) * SUB + i4 * SUB_C))
    return pairs


def _out_kernel(om_ref, glo_ref, ghi_ref, mq_ref, kv_ref, x_ref, w_ref, g_ref, b_ref, o_ref, xs_ref, *, subblock):
    half = D_INNER // 2
    ones = jnp.ones((kv_ref.shape[1], HEAD_DIM), _bf16)
    values = [jnp.concatenate([kv_ref[0, :, MEM_W + h * HEAD_DIM:MEM_W + (h + 1) * HEAD_DIM], ones], axis=1)
              for h in range(MEM_HEADS)]

    n_slices = D_INNER // MEM_W
    part_rows = OUT_SUB // n_slices

    def layer_norm_part(r0, q, z_in):
        lo = q * part_rows
        xin = (xs_ref if subblock else x_ref.at[0])[r0 + lo:r0 + lo + part_rows, :]
        z = ALPHA * xin + z_in[lo:lo + part_rows, :]
        mu = jnp.mean(z, axis=-1, keepdims=True)
        zc = z - mu
        var = jnp.mean(zc * zc, axis=-1, keepdims=True)
        out = zc * lax.rsqrt(var + LN_EPS) * g_ref[...] + b_ref[...]
        if subblock:
            for nat, sub in _subblock_chunks(OUT_SUB):
                if lo <= sub < lo + part_rows:
                    o_ref[0, r0 + nat:r0 + nat + SUB_C, :] = out[sub - lo:sub - lo + SUB_C, :]
        else:
            o_ref[0, r0 + lo:r0 + lo + part_rows, :] = out

    pending = None
    for r0 in range(0, om_ref.shape[1], OUT_SUB):
        rows = slice(r0, r0 + OUT_SUB)
        if subblock:
            for nat, sub in _subblock_chunks(OUT_SUB):
                xs_ref[r0 + sub:r0 + sub + SUB_C, :] = x_ref[0, r0 + nat:r0 + nat + SUB_C, :]

        def silu_gate(c0, c1, rows=rows):
            ref, off = (glo_ref, 0) if c0 < half else (ghi_ref, half)
            hg = 0.5 * ref[0, rows, c0 - off:c1 - off].astype(_f32)
            return hg + hg * jnp.tanh(hg)

        scores = [_qk(mq_ref[0, rows, h * HEAD_DIM:(h + 1) * HEAD_DIM],
                      kv_ref[0, :, h * HEAD_DIM:(h + 1) * HEAD_DIM]) * QSCALE for h in range(MEM_HEADS)]
        _, accs = _softmax_pv(scores, values)
        branch = None
        for q, c0 in enumerate(range(0, D_INNER, MEM_W)):
            if c0 < MIX_W:
                y = om_ref[0, rows, c0:c0 + MEM_W].astype(_f32) * silu_gate(c0, c0 + MEM_W)
            else:
                y = jnp.concatenate([acc[:, :HEAD_DIM] / acc[:, HEAD_DIM:] for acc in accs], axis=1)
                y = y * silu_gate(c0, c0 + MEM_W)
            part = jnp.dot(y.astype(_bf16), w_ref[c0:c0 + MEM_W, :], preferred_element_type=_f32)
            branch = part if branch is None else branch + part
            if pending is not None:
                layer_norm_part(pending[0], q, pending[1])
        pending = (r0, branch)
    for q in range(n_slices):
        layer_norm_part(pending[0], q, pending[1])


def _out(o_mix, h, mq_col, kv, x, w_out, ln_g, ln_b, *, subblock):
    b, l, d = x.shape
    tm = OUT_TM
    mlen = kv.shape[1]
    half = D_INNER // 2
    gate_col = mq_col + MEM_W
    assert mq_col % MEM_W == 0 and gate_col % half == 0
    kern = functools.partial(_out_kernel, subblock=subblock)
    return pl.pallas_call(
        kern,
        out_shape=jax.ShapeDtypeStruct((b, l, d), _f32),
        grid=(b, l // tm),
        in_specs=[pl.BlockSpec((1, tm, MIX_W), lambda bi, i: (bi, i, 0)),
                  pl.BlockSpec((1, tm, half), lambda bi, i: (bi, i, gate_col // half)),
                  pl.BlockSpec((1, tm, half), lambda bi, i: (bi, i, gate_col // half + 1)),
                  pl.BlockSpec((1, tm, MEM_W), lambda bi, i: (bi, i, mq_col // MEM_W)),
                  pl.BlockSpec((1, mlen, 2 * MEM_W), lambda bi, i: (bi, 0, 0)),
                  pl.BlockSpec((1, tm, d), lambda bi, i: (bi, i, 0)),
                  pl.BlockSpec((D_INNER, d), lambda bi, i: (0, 0)),
                  pl.BlockSpec((1, d), lambda bi, i: (0, 0)),
                  pl.BlockSpec((1, d), lambda bi, i: (0, 0))],
        out_specs=pl.BlockSpec((1, tm, d), lambda bi, i: (bi, i, 0)),
        scratch_shapes=[pltpu.VMEM((tm, d), _f32)],
        compiler_params=_cparams(2),
    )(o_mix, h, h, h, kv, x, w_out, ln_g.reshape(1, d), ln_b.reshape(1, d))


def _split_cols(w, sizes):
    out, c = [], 0
    for s in sizes:
        out.append(w[:, c:c + s])
        c += s
    return out


def kernel(x, mem, w_in_a, sink_a, w_in_b, w_in_c, rpb_c, w_mkv, w_out, ln_g, ln_b):
    b, l, d = x.shape
    mlen = mem.shape[1]
    kv_all = _memkv(mem.reshape(b * mlen, d), w_mkv.astype(_bf16)).reshape(DEPTH, b, mlen, 2 * MEM_W)
    tables = _rope_tables(l)

    for i in range(DEPTH):
        kind, j = i % NUM_MIXERS, i // NUM_MIXERS
        tail = MEM_W + D_INNER
        if kind == 0:
            seg = ((1, ((MIX_W, "rope_q"), (MEM_W, "rope_k"), (MEM_W + tail, "plain"))),)
            h, = _proj(x, w_in_a[j].astype(_bf16), tables, seg)
            o_mix = _attn_a(h, sink_a[j])
            mq_col = MIX_W + 2 * MEM_W
        elif kind == 1:
            wq0, wq1, wq2, wrest = _split_cols(w_in_b[j].astype(_bf16), (MIX_W, MIX_W, MIX_W, 2 * MIX_W + tail))
            seg = ((B_DILATIONS[2], ((MIX_W, "rope_q"),)),
                   (B_DILATIONS[1], ((MIX_W, "rope_q"),)),
                   (1, ((MIX_W, "rope_q"), (MIX_W, "rope_k"), (MIX_W + tail, "plain"))))
            hq2, hq1, h = _proj(x, jnp.concatenate([wq2, wq1, wq0, wrest], axis=1), tables, seg)
            o_mix = _attn_b(h, hq1, hq2)
            mq_col = 3 * MIX_W
        else:
            seg = ((1, ((MIX_W, "scale"), (2 * MIX_W + tail, "plain"))),)
            h, = _proj(x, w_in_c[j].astype(_bf16), tables, seg, subblock=True)
            o_mix = _attn_c(h, rpb_c[j])
            mq_col = 3 * MIX_W
        x = _out(o_mix, h, mq_col, kv_all[i], x, w_out[i].astype(_bf16), ln_g[i], ln_b[i], subblock=(kind == 2))
    return x
```

```python
import functools
import math

import jax
import jax.numpy as jnp
from jax import lax
from jax.experimental import pallas as pl
from jax.experimental.pallas import tpu as pltpu

D_MODEL = 1024
DEPTH = 4
NUM_MIXERS = 3
HEAD_DIM = 128
D_INNER = 2 * D_MODEL
MEM_HEADS = 4
MIX_HEADS = D_INNER // HEAD_DIM - MEM_HEADS
A_KV_HEADS = MIX_HEADS // 3
A_GROUP = MIX_HEADS // A_KV_HEADS
A_WINDOW = 128
B_DILATIONS = (1, 4, 16)
B_HALF_WINDOW = 64
NA_ROWS = 8
NA_COLS = 16
GRID_W = 64
ROPE_THETA = 500000.0
ROPE_DIMS = HEAD_DIM // 4
ROPE_HALF = ROPE_DIMS // 2
LN_EPS = 1e-5
ALPHA = (2 * DEPTH) ** 0.25
NEG_INF = -1e30
LOG2E = math.log2(math.e)
QSCALE = HEAD_DIM ** -0.5 * LOG2E

MIX_W = MIX_HEADS * HEAD_DIM
MEM_W = MEM_HEADS * HEAD_DIM

SUB_R = 4
SUB_C = 8
SUB = SUB_R * SUB_C
SUBS_PER_ROW = GRID_W // SUB_C
QBLK = 128

VMEM_LIMIT = 56 * 1024 * 1024
PROJ_TM = 512
PROJ_TN = 512
OUT_TM = 1024
OUT_SUB = 512

_f32 = jnp.float32
_bf16 = jnp.bfloat16


def _cparams(n_grid):
    return pltpu.CompilerParams(dimension_semantics=("arbitrary",) * n_grid,
                                vmem_limit_bytes=VMEM_LIMIT)


def _rope(blk, cos, sin):
    lane = lax.broadcasted_iota(jnp.int32, blk.shape, 1)
    up = pltpu.roll(blk, HEAD_DIM - ROPE_HALF, 1)
    down = pltpu.roll(blk, ROPE_HALF, 1)
    swapped = jnp.where(lane < ROPE_HALF, up, down)
    return blk * cos + swapped * sin


def _proj_kernel(*refs, subblock, segments, has_tables):
    x_ref, w_ref = refs[:2]
    pos = 2
    if has_tables:
        cos_ref, sin_ref = refs[2:4]
        pos = 4
    out_refs = refs[pos:pos + len(segments)]
    xs_ref, tmp_ref = refs[pos + len(segments):]
    tm = xs_ref.shape[0]
    tn = PROJ_TN

    if subblock:
        grp = SUB * SUBS_PER_ROW
        for g in range(tm // grp):
            for c8 in range(SUBS_PER_ROW):
                parts = [x_ref[0, g * grp + i4 * GRID_W + c8 * SUB_C:
                               g * grp + i4 * GRID_W + (c8 + 1) * SUB_C, :] for i4 in range(SUB_R)]
                dst = (g * SUBS_PER_ROW + c8) * SUB
                xs_ref[dst:dst + SUB, :] = jnp.concatenate(parts, axis=0).astype(_bf16)
    else:
        xs_ref[...] = x_ref[0].astype(_bf16)

    def write(o_ref, dil, val, c0):
        c1 = c0 + HEAD_DIM
        if dil == 1:
            o_ref[0, :, c0:c1] = val.astype(o_ref.dtype)
            return
        n = tm // dil
        tmp_ref[0] = val
        if dil == B_DILATIONS[2]:
            d1 = B_DILATIONS[1]
            n1 = tm // d1
            for r in range(d1):
                tmp_ref[1, r * n1:(r + 1) * n1, :] = tmp_ref[0, pl.ds(r, n1, stride=d1), :]
            for r in range(dil):
                rows = tmp_ref[1, pl.ds((r % d1) * n1 + r // d1, n, stride=d1), :]
                o_ref[0, r, :, c0:c1] = rows.astype(o_ref.dtype)
        else:
            for r in range(dil):
                o_ref[0, r, :, c0:c1] = tmp_ref[0, pl.ds(r, n, stride=dil), :].astype(o_ref.dtype)

    wcol = 0
    for o_ref, (dil, parts) in zip(out_refs, segments):
        ocol = 0
        for n_cols, mode in parts:
            for _ in range(n_cols // tn):
                acc = jnp.dot(xs_ref[...], w_ref[:, wcol:wcol + tn], preferred_element_type=_f32)
                for hd in range(tn // HEAD_DIM):
                    val = acc[:, hd * HEAD_DIM:(hd + 1) * HEAD_DIM]
                    if mode == "rope_q":
                        val = _rope(val, cos_ref[0], sin_ref[0])
                    elif mode == "rope_k":
                        val = _rope(val, cos_ref[1], sin_ref[1])
                    elif mode == "scale":
                        val = val * QSCALE
                    write(o_ref, dil, val, ocol + hd * HEAD_DIM)
                wcol += tn
                ocol += tn


def _proj(x, w, tables, segments, *, subblock=False):
    b, l, d = x.shape
    tm, tn = PROJ_TM, PROJ_TN
    tiles = l // tm
    assert l % tm == 0
    widths = [sum(n for n, _ in parts) for _, parts in segments]
    assert sum(widths) == w.shape[1] and all(n % tn == 0 for _, parts in segments for n, _ in parts)
    out_shapes, out_specs = [], []
    for (dil, _), n in zip(segments, widths):
        if dil == 1:
            out_shapes.append(jax.ShapeDtypeStruct((b, l, n), _bf16))
            out_specs.append(pl.BlockSpec((1, tm, n), lambda i: (i // tiles, i % tiles, 0)))
        else:
            out_shapes.append(jax.ShapeDtypeStruct((b, dil, l // dil, n), _bf16))
            out_specs.append(pl.BlockSpec((1, dil, tm // dil, n), lambda i: (i // tiles, 0, i % tiles, 0)))
    in_specs = [pl.BlockSpec((1, tm, d), lambda i: (i // tiles, i % tiles, 0)),
                pl.BlockSpec((d, w.shape[1]), lambda i: (0, 0), pipeline_mode=pl.Buffered(1))]
    args = [x, w]
    has_tables = any(mode.startswith("rope") for _, parts in segments for _, mode in parts)
    if has_tables:
        tab_spec = pl.BlockSpec((2, tm, HEAD_DIM), lambda i: (0, i % tiles, 0))
        in_specs += [tab_spec, tab_spec]
        args += list(tables)
    kern = functools.partial(_proj_kernel, subblock=subblock, segments=segments, has_tables=has_tables)
    return pl.pallas_call(
        kern,
        out_shape=out_shapes,
        grid=(b * tiles,),
        in_specs=in_specs,
        out_specs=out_specs,
        scratch_shapes=[pltpu.VMEM((tm, d), _bf16), pltpu.VMEM((2, tm, HEAD_DIM), _f32)],
        compiler_params=_cparams(1),
    )(*args)


def _rope_tables(l):
    inv = ROPE_THETA ** (-jnp.arange(ROPE_HALF, dtype=_f32) / ROPE_HALF)
    ang = jnp.arange(l).astype(_f32)[:, None] * inv[None, :]
    cos, sin = jnp.cos(ang), jnp.sin(ang)
    pad = HEAD_DIM - ROPE_DIMS
    cos_t = jnp.concatenate([cos, cos, jnp.ones((l, pad), _f32)], axis=1)
    sin_t = jnp.concatenate([-sin, sin, jnp.zeros((l, pad), _f32)], axis=1)
    return jnp.stack([cos_t * QSCALE, cos_t]), jnp.stack([sin_t * QSCALE, sin_t])


def _memkv_kernel(m_ref, w_ref, o_ref):
    o_ref[0] = jnp.dot(m_ref[...].astype(_bf16), w_ref[0], preferred_element_type=_f32).astype(o_ref.dtype)


def _memkv(mem2d, w_mkv):
    rows, d = mem2d.shape
    depth, _, n = w_mkv.shape
    tm = min(rows, PROJ_TM)
    assert rows % tm == 0
    return pl.pallas_call(
        _memkv_kernel,
        out_shape=jax.ShapeDtypeStruct((depth, rows, n), _bf16),
        grid=(depth, rows // tm),
        in_specs=[pl.BlockSpec((tm, d), lambda li, i: (i, 0)),
                  pl.BlockSpec((1, d, n), lambda li, i: (li, 0, 0))],
        out_specs=pl.BlockSpec((1, tm, n), lambda li, i: (li, i, 0)),
        compiler_params=_cparams(2),
    )(mem2d, w_mkv)


def _qk(q, k):
    return lax.dot_general(q, k, (((1,), (1,)), ((), ())), preferred_element_type=_f32)


def _first_step():
    return (pl.program_id(0) == 0) & (pl.program_id(1) == 0)


def _band_masks(mask_ref, nk, half, deltas):
    rel = (lax.broadcasted_iota(jnp.int32, (QBLK, nk), 1) - lax.broadcasted_iota(jnp.int32, (QBLK, nk), 0))
    for v, delta in enumerate(deltas):
        mask_ref[v] = jnp.where(jnp.abs(rel + delta) <= half, 0.0, NEG_INF).astype(_f32)


def _edge_variant(i, n):
    return jnp.where(i == 0, 0, jnp.where(i == n - 1, 2, 1))


def _softmax_pv(scores, values, floors=None):
    def rowmax(s):
        m = s[:, 0:HEAD_DIM]
        for c in range(1, s.shape[1] // HEAD_DIM):
            m = jnp.maximum(m, s[:, c * HEAD_DIM:(c + 1) * HEAD_DIM])
        return jnp.max(m, axis=-1, keepdims=True)

    ms = [rowmax(s) for s in scores]
    if floors is not None:
        ms = [jnp.maximum(m, f) for m, f in zip(ms, floors)]
    ps = [jnp.exp2(s - m).astype(_bf16) for s, m in zip(scores, ms)]
    accs = [jnp.dot(p, v, preferred_element_type=_f32) for p, v in zip(ps, values)]
    return ms, accs


A_UNROLL = 32


def _attn_a_kernel(q_ref, k_ref, v_ref, sink_ref, o_ref, v1_ref, mask_ref, es_ref):
    l = k_ref.shape[1]
    nk = 3 * A_WINDOW
    nblk = l // QBLK
    v1_ref[:, 0:HEAD_DIM] = v_ref[0]

    @pl.when(_first_step())
    def _():
        v1_ref[:, HEAD_DIM:2 * HEAD_DIM] = jnp.ones((l, HEAD_DIM), _bf16)
        rel = (lax.broadcasted_iota(jnp.int32, (QBLK, nk), 1) - lax.broadcasted_iota(jnp.int32, (QBLK, nk), 0))
        for v, delta in enumerate((0, -A_WINDOW, -2 * A_WINDOW)):
            blk = jnp.where(jnp.abs(rel + delta) <= A_WINDOW, 0.0, NEG_INF).astype(_f32)
            for a in range(A_GROUP):
                mask_ref[v, a * QBLK:(a + 1) * QBLK, :] = blk

    for a in range(A_GROUP):
        es_ref[a * QBLK:(a + 1) * QBLK, :] = jnp.broadcast_to(sink_ref[0, a:a + 1, :] * LOG2E, (QBLK, HEAD_DIM))

    def body(t, carry):
        scores, windows = [], []
        for u in range(A_UNROLL):
            i = t * A_UNROLL + u
            q0 = pl.multiple_of(i * QBLK, QBLK)
            ks = pl.multiple_of(jnp.clip(q0 - A_WINDOW, 0, l - nk), QBLK)
            q = jnp.concatenate([q_ref[0, pl.ds(q0, QBLK), a * HEAD_DIM:(a + 1) * HEAD_DIM]
                                 for a in range(A_GROUP)], axis=0)
            scores.append(_qk(q, k_ref[0, pl.ds(ks, nk), :]) + mask_ref[_edge_variant(i, nblk)])
            windows.append((q0, ks))
        cols = range(nk // HEAD_DIM)
        ms = []
        for s in scores:
            m = s[:, 0:HEAD_DIM]
            for c in cols[1:]:
                m = jnp.maximum(m, s[:, c * HEAD_DIM:(c + 1) * HEAD_DIM])
            m = jnp.broadcast_to(jnp.max(m, axis=-1, keepdims=True), m.shape)
            ms.append(jnp.maximum(m, es_ref[...]))
        ps = [jnp.concatenate([jnp.exp2(s[:, c * HEAD_DIM:(c + 1) * HEAD_DIM] - m) for c in cols], axis=1).astype(_bf16)
              for s, m in zip(scores, ms)]
        accs = [jnp.dot(p, v1_ref[pl.ds(ks, nk), :], preferred_element_type=_f32) for p, (_, ks) in zip(ps, windows)]
        for m, acc, (q0, _) in zip(ms, accs, windows):
            z = acc[:, HEAD_DIM:] + jnp.exp2(es_ref[...] - m)
            o = (acc[:, :HEAD_DIM] / z).astype(o_ref.dtype)
            for a in range(A_GROUP):
                o_ref[0, pl.ds(q0, QBLK), a * HEAD_DIM:(a + 1) * HEAD_DIM] = o[a * QBLK:(a + 1) * QBLK, :]
        return carry

    lax.fori_loop(0, nblk // A_UNROLL, body, 0)


def _attn_a(h, sink):
    b, l, _ = h.shape
    gw = A_GROUP * HEAD_DIM
    sink_b = jnp.broadcast_to(sink.astype(_f32).reshape(A_KV_HEADS, A_GROUP, 1), (A_KV_HEADS, A_GROUP, HEAD_DIM))
    sink_b = jnp.pad(sink_b, ((0, 0), (0, 8 - A_GROUP), (0, 0)))
    return pl.pallas_call(
        _attn_a_kernel,
        out_shape=jax.ShapeDtypeStruct((b, l, MIX_W), _bf16),
        grid=(b, A_KV_HEADS),
        in_specs=[pl.BlockSpec((1, l, gw), lambda bi, g: (bi, 0, g)),
                  pl.BlockSpec((1, l, HEAD_DIM), lambda bi, g: (bi, 0, MIX_HEADS + g)),
                  pl.BlockSpec((1, l, HEAD_DIM), lambda bi, g: (bi, 0, MIX_HEADS + A_KV_HEADS + g)),
                  pl.BlockSpec((1, 8, HEAD_DIM), lambda bi, g: (g, 0, 0))],
        out_specs=pl.BlockSpec((1, l, gw), lambda bi, g: (bi, 0, g)),
        scratch_shapes=[pltpu.VMEM((l, 2 * HEAD_DIM), _bf16),
                        pltpu.VMEM((3, A_GROUP * QBLK, 3 * A_WINDOW), _f32),
                        pltpu.VMEM((A_GROUP * QBLK, HEAD_DIM), _f32)],
        compiler_params=_cparams(2),
    )(h, h, h, sink_b)


B_UNROLL = 32


def _attn_b_kernel(q0_ref, k_ref, v_ref, q1_ref, q2_ref, o_ref,
                   kd_ref, v1_ref, tmp_ref, tmp2_ref, og_ref, lse_ref, mask_ref):
    l = o_ref.shape[1]
    nk = 4 * B_HALF_WINDOW
    q_refs = (q0_ref, q1_ref, q2_ref)
    @pl.when(_first_step())
    def _():
        _band_masks(mask_ref, nk, B_HALF_WINDOW, (0, -B_HALF_WINDOW, -2 * B_HALF_WINDOW))
        for g in range(3):
            v1_ref[g, :, HEAD_DIM:2 * HEAD_DIM] = jnp.ones((l, HEAD_DIM), _bf16)

    v1_ref[0, :, 0:HEAD_DIM] = v_ref[0]
    d1 = B_DILATIONS[1]
    assert B_DILATIONS == (1, d1, d1 * d1)
    n1, n2 = l // d1, l // (d1 * d1)
    for src, is_key in ((k_ref, True), (v_ref, False)):
        def put(g, row0, n, rows):
            if is_key:
                kd_ref[g - 1, row0:row0 + n, :] = rows.astype(_bf16)
            else:
                v1_ref[g, row0:row0 + n, 0:HEAD_DIM] = rows.astype(_bf16)

        tmp_ref[...] = src[0].astype(_f32)
        for r in range(d1):
            rows = tmp_ref[pl.ds(r, n1, stride=d1), :]
            tmp2_ref[r * n1:(r + 1) * n1, :] = rows
            put(1, r * n1, n1, rows)
        for r in range(d1 * d1):
            put(2, r * n2, n2, tmp2_ref[pl.ds((r % d1) * n1 + r // d1, n2, stride=d1), :])

    for g, dil in reversed(list(enumerate(B_DILATIONS))):
        n = l // dil
        blocks = n // QBLK

        def body(t, carry, g=g, dil=dil, n=n, blocks=blocks):
            scores, values, dests = [], [], []
            for u in range(B_UNROLL):
                bt = t * B_UNROLL + u
                r = bt // blocks
                i = bt % blocks
                q0 = pl.multiple_of(i * QBLK, QBLK)
                ks = pl.multiple_of(jnp.clip(q0 - B_HALF_WINDOW, 0, n - nk), B_HALF_WINDOW)
                base = pl.multiple_of(r * n + ks, B_HALF_WINDOW)
                if g == 0:
                    q = q0_ref[0, pl.ds(q0, QBLK), :]
                    kw = k_ref[0, pl.ds(ks, nk), :]
                else:
                    q = q_refs[g][0, r, pl.ds(q0, QBLK), :]
                    kw = kd_ref[g - 1, pl.ds(base, nk), :]
                scores.append(_qk(q, kw) + mask_ref[_edge_variant(i, blocks)])
                values.append(v1_ref[g, pl.ds(base, nk), :])
                dests.append((q0, r))
            ms, accs = _softmax_pv(scores, values)
            for m, acc, (q0, r) in zip(ms, accs, dests):
                z = acc[:, HEAD_DIM:]
                if g == 2:
                    dst = pl.ds((r % d1) * n1 + q0 * d1 + r // d1, QBLK, stride=d1)
                    og_ref[0, dst, :] = acc[:, :HEAD_DIM] / z
                    lse_ref[0, dst, :] = m + jnp.log2(z)
                    continue
                prev = pl.ds(pl.multiple_of(r * n + q0, QBLK), QBLK) if g == 1 else pl.ds(q0, QBLK)
                lp = lse_ref[g - 1 if g else 1, prev, :]
                top = jnp.maximum(m, lp)
                w, wp = jnp.exp2(m - top), jnp.exp2(lp - top)
                num = w * acc[:, :HEAD_DIM] + wp * og_ref[g - 1 if g else 1, prev, :]
                den = w * z + wp
                if g == 1:
                    dst = pl.ds(q0 * dil + r, QBLK, stride=dil)
                    og_ref[1, dst, :] = num / den
                    lse_ref[1, dst, :] = top + jnp.log2(den)
                else:
                    o_ref[0, pl.ds(q0, QBLK), :] = (num / den).astype(o_ref.dtype)
            return carry

        lax.fori_loop(0, dil * blocks // B_UNROLL, body, 0)


def _attn_b(h0, hq1, hq2):
    b, l, _ = h0.shape
    in_specs = [pl.BlockSpec((1, l, HEAD_DIM), lambda bi, h: (bi, 0, h)),
                pl.BlockSpec((1, l, HEAD_DIM), lambda bi, h: (bi, 0, MIX_HEADS + h)),
                pl.BlockSpec((1, l, HEAD_DIM), lambda bi, h: (bi, 0, 2 * MIX_HEADS + h))]
    for dil in B_DILATIONS[1:]:
        in_specs.append(pl.BlockSpec((1, dil, l // dil, HEAD_DIM), lambda bi, h: (bi, 0, 0, h)))
    return pl.pallas_call(
        _attn_b_kernel,
        out_shape=jax.ShapeDtypeStruct((b, l, MIX_W), _bf16),
        grid=(b, MIX_HEADS),
        in_specs=in_specs,
        out_specs=pl.BlockSpec((1, l, HEAD_DIM), lambda bi, h: (bi, 0, h)),
        scratch_shapes=[pltpu.VMEM((2, l, HEAD_DIM), _bf16),
                        pltpu.VMEM((3, l, 2 * HEAD_DIM), _bf16),
                        pltpu.VMEM((l, HEAD_DIM), _f32),
                        pltpu.VMEM((l, HEAD_DIM), _f32),
                        pltpu.VMEM((2, l, HEAD_DIM), _f32),
                        pltpu.VMEM((2, l, HEAD_DIM), _f32),
                        pltpu.VMEM((3, QBLK, 4 * B_HALF_WINDOW), _f32)],
        compiler_params=_cparams(2),
    )(h0, h0, h0, hq1, hq2)


C_KSEG = 4 * SUB
C_NSEG = 4
C_PAD = SUB
C_NKEYS = C_NSEG * C_KSEG
C_UNROLL = 8


def _attn_c_kernel(q_ref, k_ref, v_ref, bias_ref, mask_ref, o_ref, kp_ref, vp_ref, bm_ref):
    l = k_ref.shape[1]
    n_rb = l // GRID_W // NA_ROWS
    n_cb = GRID_W // NA_COLS
    n_r4 = l // GRID_W // SUB_R
    @pl.when(_first_step())
    def _():
        kp_ref[0:C_PAD, :] = jnp.zeros((C_PAD, HEAD_DIM), _bf16)
        kp_ref[C_PAD + l:C_PAD + l + C_PAD, :] = jnp.zeros((C_PAD, HEAD_DIM), _bf16)
        vp_ref[0:C_PAD, 0:HEAD_DIM] = jnp.zeros((C_PAD, HEAD_DIM), _bf16)
        vp_ref[C_PAD + l:C_PAD + l + C_PAD, 0:HEAD_DIM] = jnp.zeros((C_PAD, HEAD_DIM), _bf16)
        vp_ref[:, HEAD_DIM:2 * HEAD_DIM] = jnp.ones((l + 2 * C_PAD, HEAD_DIM), _bf16)

    @pl.when(pl.program_id(1) == 0)
    def _():
        for v in range(9):
            bm_ref[v] = bias_ref[0] + mask_ref[v]

    kp_ref[C_PAD:C_PAD + l, :] = k_ref[0]
    vp_ref[C_PAD:C_PAD + l, 0:HEAD_DIM] = v_ref[0]

    def body(t, carry):
        scores, values, dests = [], [], []
        for u in range(C_UNROLL * n_cb):
            rb = t * C_UNROLL + u // n_cb
            cb = u % n_cb
            rv = _edge_variant(rb, n_rb)
            cv = 0 if cb == 0 else (2 if cb == n_cb - 1 else 1)
            q_starts = [pl.multiple_of(((2 * rb + a) * SUBS_PER_ROW + 2 * cb) * SUB, 2 * SUB) for a in range(2)]
            q = jnp.concatenate([q_ref[0, pl.ds(qs, 2 * SUB), :] for qs in q_starts], axis=0)
            kparts, vparts = [], []
            for ar in range(C_NSEG):
                r4 = jnp.clip(2 * rb - 1 + ar, 0, n_r4 - 1)
                st = pl.multiple_of(C_PAD + (r4 * SUBS_PER_ROW + 2 * cb - 1) * SUB, SUB)
                kparts.append(kp_ref[pl.ds(st, C_KSEG), :])
                vparts.append(vp_ref[pl.ds(st, C_KSEG), :])
            scores.append(_qk(q, jnp.concatenate(kparts, axis=0)) + bm_ref[rv * 3 + cv])
            values.append(jnp.concatenate(vparts, axis=0))
            dests.append(q_starts)
        ms, accs = _softmax_pv(scores, values)
        for acc, q_starts in zip(accs, dests):
            o = (acc[:, :HEAD_DIM] / acc[:, HEAD_DIM:]).astype(o_ref.dtype)
            for a in range(2):
                o_ref[0, pl.ds(q_starts[a], 2 * SUB), :] = o[a * 2 * SUB:(a + 1) * 2 * SUB, :]
        return carry

    lax.fori_loop(0, n_rb // C_UNROLL, body, 0)


def _c_geometry():
    ql = jnp.arange(QBLK)
    qa, qc, qi, qj = ql // 64, (ql // 32) % 2, (ql // 8) % 4, ql % 8
    q_row = SUB_R * qa + qi
    q_col = SUB_C * qc + qj
    kl = jnp.arange(C_NKEYS)
    ka, kc, ki, kj = kl // C_KSEG, (kl // SUB) % 4, (kl // 8) % 4, kl % 8
    k_row = SUB_R * (ka - 1) + ki
    k_col = SUB_C * (kc - 1) + kj
    return q_row, q_col, k_row, k_col


def _c_bias(rpb):
    q_row, q_col, k_row, k_col = _c_geometry()
    nr, nc = 2 * NA_ROWS - 1, 2 * NA_COLS - 1
    dr = jnp.clip(k_row[None, :] - q_row[:, None] + NA_ROWS - 1, 0, nr - 1)
    dc = jnp.clip(k_col[None, :] - q_col[:, None] + NA_COLS - 1, 0, nc - 1)
    oh_r = (dr[:, :, None] == jnp.arange(nr)).astype(_f32)
    oh_c = (dc[:, :, None] == jnp.arange(nc)).astype(_f32)
    rows = jnp.einsum("hrc,qkr->hqkc", rpb.astype(_f32), oh_r, precision=lax.Precision.HIGHEST)
    return jnp.sum(rows * oh_c[None], axis=-1) * LOG2E


def _c_masks(rows):
    q_row, q_col, k_row, k_col = _c_geometry()
    out = []
    for rb in (0, 1, rows // NA_ROWS - 1):
        qr, kr = NA_ROWS * rb + q_row, NA_ROWS * rb + k_row
        rs = jnp.clip(qr - NA_ROWS // 2, 0, rows - NA_ROWS)
        rvalid = (kr[None, :] >= rs[:, None]) & (kr[None, :] < rs[:, None] + NA_ROWS) & (kr[None, :] >= 0) & (kr[None, :] < rows)
        for cb in (0, 1, GRID_W // NA_COLS - 1):
            qc, kc = NA_COLS * cb + q_col, NA_COLS * cb + k_col
            cs = jnp.clip(qc - NA_COLS // 2, 0, GRID_W - NA_COLS)
            cvalid = (kc[None, :] >= cs[:, None]) & (kc[None, :] < cs[:, None] + NA_COLS) & (kc[None, :] >= 0) & (kc[None, :] < GRID_W)
            out.append(jnp.where(rvalid & cvalid, 0.0, NEG_INF).astype(_f32))
    return jnp.stack(out)


def _attn_c(h, rpb):
    b, l, _ = h.shape
    bias = _c_bias(rpb)
    masks = _c_masks(l // GRID_W)
    return pl.pallas_call(
        _attn_c_kernel,
        out_shape=jax.ShapeDtypeStruct((b, l, MIX_W), _bf16),
        grid=(MIX_HEADS, b),
        in_specs=[pl.BlockSpec((1, l, HEAD_DIM), lambda h, bi: (bi, 0, h)),
                  pl.BlockSpec((1, l, HEAD_DIM), lambda h, bi: (bi, 0, MIX_HEADS + h)),
                  pl.BlockSpec((1, l, HEAD_DIM), lambda h, bi: (bi, 0, 2 * MIX_HEADS + h)),
                  pl.BlockSpec((1, QBLK, C_NKEYS), lambda h, bi: (h, 0, 0)),
                  pl.BlockSpec((9, QBLK, C_NKEYS), lambda h, bi: (0, 0, 0))],
        out_specs=pl.BlockSpec((1, l, HEAD_DIM), lambda h, bi: (bi, 0, h)),
        scratch_shapes=[pltpu.VMEM((l + 2 * C_PAD, HEAD_DIM), _bf16),
                        pltpu.VMEM((l + 2 * C_PAD, 2 * HEAD_DIM), _bf16),
                        pltpu.VMEM((9, QBLK, C_NKEYS), _f32)],
        compiler_params=_cparams(2),
    )(h, h, h, bias, masks)


def _subblock_chunks(tm):
    grp = SUB * SUBS_PER_ROW
    pairs = []
    for g in range(tm // grp):
        for c8 in range(SUBS_PER_ROW):
            for i4 in range(SUB_R):
                pairs.append((g * grp + i4 * GRID_W + c8 * SUB_C, (g * SUBS_PER_ROW + c8) * SUB + i4 * SUB_C))
    return pairs


def _out_kernel(om_ref, glo_ref, ghi_ref, mq_ref, kv_ref, x_ref, w_ref, g_ref, b_ref, o_ref, xs_ref, *, subblock):
    half = D_INNER // 2
    ones = jnp.ones((kv_ref.shape[1], HEAD_DIM), _bf16)
    values = [jnp.concatenate([kv_ref[0, :, MEM_W + h * HEAD_DIM:MEM_W + (h + 1) * HEAD_DIM], ones], axis=1)
              for h in range(MEM_HEADS)]

    n_slices = D_INNER // MEM_W
    part_rows = OUT_SUB // n_slices

    def layer_norm_part(r0, q, z_in):
        lo = q * part_rows
        xin = (xs_ref if subblock else x_ref.at[0])[r0 + lo:r0 + lo + part_rows, :]
        z = ALPHA * xin + z_in[lo:lo + part_rows, :]
        mu = jnp.mean(z, axis=-1, keepdims=True)
        zc = z - mu
        var = jnp.mean(zc * zc, axis=-1, keepdims=True)
        out = zc * lax.rsqrt(var + LN_EPS) * g_ref[...] + b_ref[...]
        if subblock:
            for nat, sub in _subblock_chunks(OUT_SUB):
                if lo <= sub < lo + part_rows:
                    o_ref[0, r0 + nat:r0 + nat + SUB_C, :] = out[sub - lo:sub - lo + SUB_C, :]
        else:
            o_ref[0, r0 + lo:r0 + lo + part_rows, :] = out

    pending = None
    for r0 in range(0, om_ref.shape[1], OUT_SUB):
        rows = slice(r0, r0 + OUT_SUB)
        if subblock:
            for nat, sub in _subblock_chunks(OUT_SUB):
                xs_ref[r0 + sub:r0 + sub + SUB_C, :] = x_ref[0, r0 + nat:r0 + nat + SUB_C, :]

        def silu_gate(c0, c1, rows=rows):
            ref, off = (glo_ref, 0) if c0 < half else (ghi_ref, half)
            hg = 0.5 * ref[0, rows, c0 - off:c1 - off].astype(_f32)
            return hg + hg * jnp.tanh(hg)

        scores = [_qk(mq_ref[0, rows, h * HEAD_DIM:(h + 1) * HEAD_DIM],
                      kv_ref[0, :, h * HEAD_DIM:(h + 1) * HEAD_DIM]) * QSCALE for h in range(MEM_HEADS)]
        _, accs = _softmax_pv(scores, values)
        branch = None
        for q, c0 in enumerate(range(0, D_INNER, MEM_W)):
            if c0 < MIX_W:
                y = om_ref[0, rows, c0:c0 + MEM_W].astype(_f32) * silu_gate(c0, c0 + MEM_W)
            else:
                y = jnp.concatenate([acc[:, :HEAD_DIM] / acc[:, HEAD_DIM:] for acc in accs], axis=1)
                y = y * silu_gate(c0, c0 + MEM_W)
            part = jnp.dot(y.astype(_bf16), w_ref[c0:c0 + MEM_W, :], preferred_element_type=_f32)
            branch = part if branch is None else branch + part
            if pending is not None:
                layer_norm_part(pending[0], q, pending[1])
        pending = (r0, branch)
    for q in range(n_slices):
        layer_norm_part(pending[0], q, pending[1])


def _out(o_mix, h, mq_col, kv, x, w_out, ln_g, ln_b, *, subblock):
    b, l, d = x.shape
    tm = OUT_TM
    mlen = kv.shape[1]
    half = D_INNER // 2
    gate_col = mq_col + MEM_W
    assert mq_col % MEM_W == 0 and gate_col % half == 0
    kern = functools.partial(_out_kernel, subblock=subblock)
    return pl.pallas_call(
        kern,
        out_shape=jax.ShapeDtypeStruct((b, l, d), _f32),
        grid=(b, l // tm),
        in_specs=[pl.BlockSpec((1, tm, MIX_W), lambda bi, i: (bi, i, 0)),
                  pl.BlockSpec((1, tm, half), lambda bi, i: (bi, i, gate_col // half)),
                  pl.BlockSpec((1, tm, half), lambda bi, i: (bi, i, gate_col // half + 1)),
                  pl.BlockSpec((1, tm, MEM_W), lambda bi, i: (bi, i, mq_col // MEM_W)),
                  pl.BlockSpec((1, mlen, 2 * MEM_W), lambda bi, i: (bi, 0, 0)),
                  pl.BlockSpec((1, tm, d), lambda bi, i: (bi, i, 0)),
                  pl.BlockSpec((D_INNER, d), lambda bi, i: (0, 0)),
                  pl.BlockSpec((1, d), lambda bi, i: (0, 0)),
                  pl.BlockSpec((1, d), lambda bi, i: (0, 0))],
        out_specs=pl.BlockSpec((1, tm, d), lambda bi, i: (bi, i, 0)),
        scratch_shapes=[pltpu.VMEM((tm, d), _f32)],
        compiler_params=_cparams(2),
    )(o_mix, h, h, h, kv, x, w_out, ln_g.reshape(1, d), ln_b.reshape(1, d))


def _split_cols(w, sizes):
    out, c = [], 0
    for s in sizes:
        out.append(w[:, c:c + s])
        c += s
    return out


def kernel(x, mem, w_in_a, sink_a, w_in_b, w_in_c, rpb_c, w_mkv, w_out, ln_g, ln_b):
    b, l, d = x.shape
    mlen = mem.shape[1]
    kv_all = _memkv(mem.reshape(b * mlen, d), w_mkv.astype(_bf16)).reshape(DEPTH, b, mlen, 2 * MEM_W)
    tables = _rope_tables(l)

    for i in range(DEPTH):
        kind, j = i % NUM_MIXERS, i // NUM_MIXERS
        tail = MEM_W + D_INNER
        if kind == 0:
            seg = ((1, ((MIX_W, "rope_q"), (MEM_W, "rope_k"), (MEM_W + tail, "plain"))),)
            h, = _proj(x, w_in_a[j].astype(_bf16), tables, seg)
            o_mix = _attn_a(h, sink_a[j])
            mq_col = MIX_W + 2 * MEM_W
        elif kind == 1:
            wq0, wq1, wq2, wrest = _split_cols(w_in_b[j].astype(_bf16), (MIX_W, MIX_W, MIX_W, 2 * MIX_W + tail))
            seg = ((B_DILATIONS[2], ((MIX_W, "rope_q"),)),
                   (B_DILATIONS[1], ((MIX_W, "rope_q"),)),
                   (1, ((MIX_W, "rope_q"), (MIX_W, "rope_k"), (MIX_W + tail, "plain"))))
            hq2, hq1, h = _proj(x, jnp.concatenate([wq2, wq1, wq0, wrest], axis=1), tables, seg)
            o_mix = _attn_b(h, hq1, hq2)
            mq_col = 3 * MIX_W
        else:
            seg = ((1, ((MIX_W, "scale"), (2 * MIX_W + tail, "plain"))),)
            h, = _proj(x, w_in_c[j].astype(_bf16), tables, seg, subblock=True)
            o_mix = _attn_c(h, rpb_c[j])
            mq_col = 3 * MIX_W
        x = _out(o_mix, h, mq_col, kv_all[i], x, w_out[i].astype(_bf16), ln_g[i], ln_b[i], subblock=(kind == 2))
    return x
```

```python
import functools
import math

import jax
import jax.numpy as jnp
from jax import lax
from jax.experimental import pallas as pl
from jax.experimental.pallas import tpu as pltpu

D_MODEL = 1024
DEPTH = 4
NUM_MIXERS = 3
HEAD_DIM = 128
D_INNER = 2 * D_MODEL
MEM_HEADS = 4
MIX_HEADS = D_INNER // HEAD_DIM - MEM_HEADS
A_KV_HEADS = MIX_HEADS // 3
A_GROUP = MIX_HEADS // A_KV_HEADS
A_WINDOW = 128
B_DILATIONS = (1, 4, 16)
B_HALF_WINDOW = 64
NA_ROWS = 8
NA_COLS = 16
GRID_W = 64
ROPE_THETA = 500000.0
ROPE_DIMS = HEAD_DIM // 4
ROPE_HALF = ROPE_DIMS // 2
LN_EPS = 1e-5
ALPHA = (2 * DEPTH) ** 0.25
NEG_INF = -1e30
LOG2E = math.log2(math.e)
QSCALE = HEAD_DIM ** -0.5 * LOG2E

MIX_W = MIX_HEADS * HEAD_DIM
MEM_W = MEM_HEADS * HEAD_DIM

SUB_R = 4
SUB_C = 8
SUB = SUB_R * SUB_C
SUBS_PER_ROW = GRID_W // SUB_C
QBLK = 128

VMEM_LIMIT = 56 * 1024 * 1024
PROJ_TM = 512
PROJ_TN = 512
OUT_TM = 1024
OUT_SUB = 512

_f32 = jnp.float32
_bf16 = jnp.bfloat16


def _cparams(n_grid):
    return pltpu.CompilerParams(dimension_semantics=("arbitrary",) * n_grid,
                                vmem_limit_bytes=VMEM_LIMIT)


def _rope(blk, cos, sin):
    lane = lax.broadcasted_iota(jnp.int32, blk.shape, 1)
    up = pltpu.roll(blk, HEAD_DIM - ROPE_HALF, 1)
    down = pltpu.roll(blk, ROPE_HALF, 1)
    swapped = jnp.where(lane < ROPE_HALF, up, down)
    return blk * cos + swapped * sin


def _proj_kernel(*refs, subblock, segments, has_tables):
    x_ref, w_ref = refs[:2]
    pos = 2
    if has_tables:
        cos_ref, sin_ref = refs[2:4]
        pos = 4
    out_refs = refs[pos:pos + len(segments)]
    xs_ref, tmp_ref = refs[pos + len(segments):]
    tm = xs_ref.shape[0]
    tn = PROJ_TN

    if subblock:
        grp = SUB * SUBS_PER_ROW
        for g in range(tm // grp):
            for c8 in range(SUBS_PER_ROW):
                parts = [x_ref[0, g * grp + i4 * GRID_W + c8 * SUB_C:
                               g * grp + i4 * GRID_W + (c8 + 1) * SUB_C, :] for i4 in range(SUB_R)]
                dst = (g * SUBS_PER_ROW + c8) * SUB
                xs_ref[dst:dst + SUB, :] = jnp.concatenate(parts, axis=0).astype(_bf16)
    else:
        xs_ref[...] = x_ref[0].astype(_bf16)

    def write(o_ref, dil, val, c0):
        c1 = c0 + HEAD_DIM
        if dil == 1:
            o_ref[0, :, c0:c1] = val.astype(o_ref.dtype)
            return
        n = tm // dil
        tmp_ref[0] = val
        if dil == B_DILATIONS[2]:
            d1 = B_DILATIONS[1]
            n1 = tm // d1
            for r in range(d1):
                tmp_ref[1, r * n1:(r + 1) * n1, :] = tmp_ref[0, pl.ds(r, n1, stride=d1), :]
            for r in range(dil):
                rows = tmp_ref[1, pl.ds((r % d1) * n1 + r // d1, n, stride=d1), :]
                o_ref[0, r, :, c0:c1] = rows.astype(o_ref.dtype)
        else:
            for r in range(dil):
                o_ref[0, r, :, c0:c1] = tmp_ref[0, pl.ds(r, n, stride=dil), :].astype(o_ref.dtype)

    wcol = 0
    for o_ref, (dil, parts) in zip(out_refs, segments):
        ocol = 0
        for n_cols, mode in parts:
            for _ in range(n_cols // tn):
                acc = jnp.dot(xs_ref[...], w_ref[:, wcol:wcol + tn], preferred_element_type=_f32)
                for hd in range(tn // HEAD_DIM):
                    val = acc[:, hd * HEAD_DIM:(hd + 1) * HEAD_DIM]
                    if mode == "rope_q":
                        val = _rope(val, cos_ref[0], sin_ref[0])
                    elif mode == "rope_k":
                        val = _rope(val, cos_ref[1], sin_ref[1])
                    elif mode == "scale":
                        val = val * QSCALE
                    write(o_ref, dil, val, ocol + hd * HEAD_DIM)
                wcol += tn
                ocol += tn


def _proj(x, w, tables, segments, *, subblock=False):
    b, l, d = x.shape
    tm, tn = PROJ_TM, PROJ_TN
    tiles = l // tm
    assert l % tm == 0
    widths = [sum(n for n, _ in parts) for _, parts in segments]
    assert sum(widths) == w.shape[1] and all(n % tn == 0 for _, parts in segments for n, _ in parts)
    out_shapes, out_specs = [], []
    for (dil, _), n in zip(segments, widths):
        if dil == 1:
            out_shapes.append(jax.ShapeDtypeStruct((b, l, n), _bf16))
            out_specs.append(pl.BlockSpec((1, tm, n), lambda i: (i // tiles, i % tiles, 0)))
        else:
            out_shapes.append(jax.ShapeDtypeStruct((b, dil, l // dil, n), _bf16))
            out_specs.append(pl.BlockSpec((1, dil, tm // dil, n), lambda i: (i // tiles, 0, i % tiles, 0)))
    in_specs = [pl.BlockSpec((1, tm, d), lambda i: (i // tiles, i % tiles, 0)),
                pl.BlockSpec((d, w.shape[1]), lambda i: (0, 0), pipeline_mode=pl.Buffered(1))]
    args = [x, w]
    has_tables = any(mode.startswith("rope") for _, parts in segments for _, mode in parts)
    if has_tables:
        tab_spec = pl.BlockSpec((2, tm, HEAD_DIM), lambda i: (0, i % tiles, 0))
        in_specs += [tab_spec, tab_spec]
        args += list(tables)
    kern = functools.partial(_proj_kernel, subblock=subblock, segments=segments, has_tables=has_tables)
    return pl.pallas_call(
        kern,
        out_shape=out_shapes,
        grid=(b * tiles,),
        in_specs=in_specs,
        out_specs=out_specs,
        scratch_shapes=[pltpu.VMEM((tm, d), _bf16), pltpu.VMEM((2, tm, HEAD_DIM), _f32)],
        compiler_params=_cparams(1),
    )(*args)


def _rope_tables(l):
    inv = ROPE_THETA ** (-jnp.arange(ROPE_HALF, dtype=_f32) / ROPE_HALF)
    ang = jnp.arange(l).astype(_f32)[:, None] * inv[None, :]
    cos, sin = jnp.cos(ang), jnp.sin(ang)
    pad = HEAD_DIM - ROPE_DIMS
    cos_t = jnp.concatenate([cos, cos, jnp.ones((l, pad), _f32)], axis=1)
    sin_t = jnp.concatenate([-sin, sin, jnp.zeros((l, pad), _f32)], axis=1)
    return jnp.stack([cos_t * QSCALE, cos_t]), jnp.stack([sin_t * QSCALE, sin_t])


def _memkv_kernel(m_ref, w_ref, o_ref):
    o_ref[0] = jnp.dot(m_ref[...].astype(_bf16), w_ref[0], preferred_element_type=_f32).astype(o_ref.dtype)


def _memkv(mem2d, w_mkv):
    rows, d = mem2d.shape
    depth, _, n = w_mkv.shape
    tm = min(rows, PROJ_TM)
    assert rows % tm == 0
    return pl.pallas_call(
        _memkv_kernel,
        out_shape=jax.ShapeDtypeStruct((depth, rows, n), _bf16),
        grid=(depth, rows // tm),
        in_specs=[pl.BlockSpec((tm, d), lambda li, i: (i, 0)),
                  pl.BlockSpec((1, d, n), lambda li, i: (li, 0, 0))],
        out_specs=pl.BlockSpec((1, tm, n), lambda li, i: (li, i, 0)),
        compiler_params=_cparams(2),
    )(mem2d, w_mkv)


def _qk(q, k):
    return lax.dot_general(q, k, (((1,), (1,)), ((), ())), preferred_element_type=_f32)


def _first_step():
    return (pl.program_id(0) == 0) & (pl.program_id(1) == 0)


def _band_masks(mask_ref, nk, half, deltas):
    rel = (lax.broadcasted_iota(jnp.int32, (QBLK, nk), 1) - lax.broadcasted_iota(jnp.int32, (QBLK, nk), 0))
    for v, delta in enumerate(deltas):
        mask_ref[v] = jnp.where(jnp.abs(rel + delta) <= half, 0.0, NEG_INF).astype(_f32)


def _edge_variant(i, n):
    return 0 if i == 0 else (2 if i == n - 1 else 1)


def _softmax_pv(scores, values, floors=None):
    def rowmax(s):
        m = s[:, 0:HEAD_DIM]
        for c in range(1, s.shape[1] // HEAD_DIM):
            m = jnp.maximum(m, s[:, c * HEAD_DIM:(c + 1) * HEAD_DIM])
        return jnp.max(m, axis=-1, keepdims=True)

    ms = [rowmax(s) for s in scores]
    if floors is not None:
        ms = [jnp.maximum(m, f) for m, f in zip(ms, floors)]
    ps = [jnp.exp2(s - m).astype(_bf16) for s, m in zip(scores, ms)]
    accs = [jnp.dot(p, v, preferred_element_type=_f32) for p, v in zip(ps, values)]
    return ms, accs


def _attn_a_kernel(q_ref, k_ref, v_ref, sink_ref, o_ref, mask_ref, es_ref):
    l = k_ref.shape[1]
    nk = 3 * A_WINDOW
    nblk = l // QBLK
    ones = jnp.ones((nk, HEAD_DIM), _bf16)

    @pl.when(_first_step())
    def _():
        rel = (lax.broadcasted_iota(jnp.int32, (QBLK, nk), 1) - lax.broadcasted_iota(jnp.int32, (QBLK, nk), 0))
        for v, delta in enumerate((0, -A_WINDOW, -2 * A_WINDOW)):
            blk = jnp.where(jnp.abs(rel + delta) <= A_WINDOW, 0.0, NEG_INF).astype(_f32)
            for a in range(A_GROUP):
                mask_ref[v, a * QBLK:(a + 1) * QBLK, :] = blk

    for a in range(A_GROUP):
        es_ref[a * QBLK:(a + 1) * QBLK, :] = jnp.broadcast_to(sink_ref[0, a:a + 1, :] * LOG2E, (QBLK, HEAD_DIM))

    scores, windows = [], []
    for i in range(nblk):
        q0 = i * QBLK
        ks = min(max(q0 - A_WINDOW, 0), l - nk)
        q = jnp.concatenate([q_ref[0, q0:q0 + QBLK, a * HEAD_DIM:(a + 1) * HEAD_DIM]
                             for a in range(A_GROUP)], axis=0)
        scores.append(_qk(q, k_ref[0, ks:ks + nk, :]) + mask_ref[_edge_variant(i, nblk)])
        windows.append((q0, ks))
    cols = range(nk // HEAD_DIM)
    ms = []
    for s in scores:
        m = s[:, 0:HEAD_DIM]
        for c in cols[1:]:
            m = jnp.maximum(m, s[:, c * HEAD_DIM:(c + 1) * HEAD_DIM])
        m = jnp.broadcast_to(jnp.max(m, axis=-1, keepdims=True), m.shape)
        ms.append(jnp.maximum(m, es_ref[...]))
    ps = [jnp.concatenate([jnp.exp2(s[:, c * HEAD_DIM:(c + 1) * HEAD_DIM] - m) for c in cols], axis=1).astype(_bf16)
          for s, m in zip(scores, ms)]
    accs = [jnp.dot(p, jnp.concatenate([v_ref[0, ks:ks + nk, :], ones], axis=1), preferred_element_type=_f32)
            for p, (_, ks) in zip(ps, windows)]
    for m, acc, (q0, _) in zip(ms, accs, windows):
        z = acc[:, HEAD_DIM:] + jnp.exp2(es_ref[...] - m)
        o = (acc[:, :HEAD_DIM] / z).astype(o_ref.dtype)
        for a in range(A_GROUP):
            o_ref[0, q0:q0 + QBLK, a * HEAD_DIM:(a + 1) * HEAD_DIM] = o[a * QBLK:(a + 1) * QBLK, :]


def _attn_a(h, sink):
    b, l, _ = h.shape
    gw = A_GROUP * HEAD_DIM
    sink_b = jnp.broadcast_to(sink.astype(_f32).reshape(A_KV_HEADS, A_GROUP, 1), (A_KV_HEADS, A_GROUP, HEAD_DIM))
    sink_b = jnp.pad(sink_b, ((0, 0), (0, 8 - A_GROUP), (0, 0)))
    return pl.pallas_call(
        _attn_a_kernel,
        out_shape=jax.ShapeDtypeStruct((b, l, MIX_W), _bf16),
        grid=(b, A_KV_HEADS),
        in_specs=[pl.BlockSpec((1, l, gw), lambda bi, g: (bi, 0, g)),
                  pl.BlockSpec((1, l, HEAD_DIM), lambda bi, g: (bi, 0, MIX_HEADS + g)),
                  pl.BlockSpec((1, l, HEAD_DIM), lambda bi, g: (bi, 0, MIX_HEADS + A_KV_HEADS + g)),
                  pl.BlockSpec((1, 8, HEAD_DIM), lambda bi, g: (g, 0, 0))],
        out_specs=pl.BlockSpec((1, l, gw), lambda bi, g: (bi, 0, g)),
        scratch_shapes=[pltpu.VMEM((3, A_GROUP * QBLK, 3 * A_WINDOW), _f32),
                        pltpu.VMEM((A_GROUP * QBLK, HEAD_DIM), _f32)],
        compiler_params=_cparams(2),
    )(h, h, h, sink_b)


def _attn_b_kernel(q0_ref, k_ref, v_ref, q1_ref, q2_ref, o_ref,
                   kd_ref, vd_ref, tmp_ref, tmp2_ref, og_ref, lse_ref, mask_ref):
    l = o_ref.shape[1]
    nk = 4 * B_HALF_WINDOW
    q_refs = (q0_ref, q1_ref, q2_ref)
    ones = jnp.ones((nk, HEAD_DIM), _bf16)

    @pl.when(_first_step())
    def _():
        _band_masks(mask_ref, nk, B_HALF_WINDOW, (0, -B_HALF_WINDOW, -2 * B_HALF_WINDOW))

    d1 = B_DILATIONS[1]
    assert B_DILATIONS == (1, d1, d1 * d1)
    n1, n2 = l // d1, l // (d1 * d1)
    for src, dst in ((k_ref, kd_ref), (v_ref, vd_ref)):
        def put(g, row0, n, rows):
            dst[g - 1, row0:row0 + n, :] = rows.astype(_bf16)

        tmp_ref[...] = src[0].astype(_f32)
        for r in range(d1):
            rows = tmp_ref[pl.ds(r, n1, stride=d1), :]
            tmp2_ref[r * n1:(r + 1) * n1, :] = rows
            put(1, r * n1, n1, rows)
        for r in range(d1 * d1):
            put(2, r * n2, n2, tmp2_ref[pl.ds((r % d1) * n1 + r // d1, n2, stride=d1), :])

    for g, dil in reversed(list(enumerate(B_DILATIONS))):
        n = l // dil
        blocks = n // QBLK
        scores, values, dests = [], [], []
        for r in range(dil):
            for i in range(blocks):
                q0 = i * QBLK
                ks = min(max(q0 - B_HALF_WINDOW, 0), n - nk)
                base = r * n + ks
                if g == 0:
                    q = q0_ref[0, q0:q0 + QBLK, :]
                    kw, vw = k_ref[0, ks:ks + nk, :], v_ref[0, ks:ks + nk, :]
                else:
                    q = q_refs[g][0, r, q0:q0 + QBLK, :]
                    kw, vw = kd_ref[g - 1, base:base + nk, :], vd_ref[g - 1, base:base + nk, :]
                scores.append(_qk(q, kw) + mask_ref[_edge_variant(i, blocks)])
                values.append(jnp.concatenate([vw, ones], axis=1))
                dests.append((q0, r))
        ms, accs = _softmax_pv(scores, values)
        for m, acc, (q0, r) in zip(ms, accs, dests):
            z = acc[:, HEAD_DIM:]
            if g == 2:
                dst = pl.ds((r % d1) * n1 + q0 * d1 + r // d1, QBLK, stride=d1)
                og_ref[0, dst, :] = acc[:, :HEAD_DIM] / z
                lse_ref[0, dst, :] = m + jnp.log2(z)
                continue
            prev = pl.ds(r * n + q0, QBLK)
            lp = lse_ref[g - 1 if g else 1, prev, :]
            top = jnp.maximum(m, lp)
            w, wp = jnp.exp2(m - top), jnp.exp2(lp - top)
            num = w * acc[:, :HEAD_DIM] + wp * og_ref[g - 1 if g else 1, prev, :]
            den = w * z + wp
            if g == 1:
                dst = pl.ds(q0 * dil + r, QBLK, stride=dil)
                og_ref[1, dst, :] = num / den
                lse_ref[1, dst, :] = top + jnp.log2(den)
            else:
                o_ref[0, q0:q0 + QBLK, :] = (num / den).astype(o_ref.dtype)


def _attn_b(h0, hq1, hq2):
    b, l, _ = h0.shape
    in_specs = [pl.BlockSpec((1, l, HEAD_DIM), lambda bi, h: (bi, 0, h)),
                pl.BlockSpec((1, l, HEAD_DIM), lambda bi, h: (bi, 0, MIX_HEADS + h)),
                pl.BlockSpec((1, l, HEAD_DIM), lambda bi, h: (bi, 0, 2 * MIX_HEADS + h))]
    for dil in B_DILATIONS[1:]:
        in_specs.append(pl.BlockSpec((1, dil, l // dil, HEAD_DIM), lambda bi, h: (bi, 0, 0, h)))
    return pl.pallas_call(
        _attn_b_kernel,
        out_shape=jax.ShapeDtypeStruct((b, l, MIX_W), _bf16),
        grid=(b, MIX_HEADS),
        in_specs=in_specs,
        out_specs=pl.BlockSpec((1, l, HEAD_DIM), lambda bi, h: (bi, 0, h)),
        scratch_shapes=[pltpu.VMEM((2, l, HEAD_DIM), _bf16),
                        pltpu.VMEM((2, l, HEAD_DIM), _bf16),
                        pltpu.VMEM((l, HEAD_DIM), _f32),
                        pltpu.VMEM((l, HEAD_DIM), _f32),
                        pltpu.VMEM((2, l, HEAD_DIM), _f32),
                        pltpu.VMEM((2, l, HEAD_DIM), _f32),
                        pltpu.VMEM((3, QBLK, 4 * B_HALF_WINDOW), _f32)],
        compiler_params=_cparams(2),
    )(h0, h0, h0, hq1, hq2)


C_KSEG = 4 * SUB
C_NSEG = 4
C_NKEYS = C_NSEG * C_KSEG


def _attn_c_kernel(q_ref, k_ref, v_ref, bias_ref, mask_ref, o_ref, bm_ref):
    l = k_ref.shape[1]
    n_rb = l // GRID_W // NA_ROWS
    n_cb = GRID_W // NA_COLS
    n_r4 = l // GRID_W // SUB_R
    ones = jnp.ones((C_NKEYS, HEAD_DIM), _bf16)

    @pl.when(pl.program_id(1) == 0)
    def _():
        for v in range(9):
            bm_ref[v] = bias_ref[0] + mask_ref[v]

    def segment(ref, st):
        lo, hi = max(st, 0), min(st + C_KSEG, l)
        parts = [jnp.zeros((lo - st, HEAD_DIM), _bf16)] if lo > st else []
        parts.append(ref[0, lo:hi, :])
        if hi < st + C_KSEG:
            parts.append(jnp.zeros((st + C_KSEG - hi, HEAD_DIM), _bf16))
        return parts

    scores, values, dests = [], [], []
    for rb in range(n_rb):
        for cb in range(n_cb):
            rv, cv = _edge_variant(rb, n_rb), _edge_variant(cb, n_cb)
            q_starts = [((2 * rb + a) * SUBS_PER_ROW + 2 * cb) * SUB for a in range(2)]
            q = jnp.concatenate([q_ref[0, qs:qs + 2 * SUB, :] for qs in q_starts], axis=0)
            kparts, vparts = [], []
            for ar in range(C_NSEG):
                r4 = min(max(2 * rb - 1 + ar, 0), n_r4 - 1)
                st = (r4 * SUBS_PER_ROW + 2 * cb - 1) * SUB
                kparts += segment(k_ref, st)
                vparts += segment(v_ref, st)
            scores.append(_qk(q, jnp.concatenate(kparts, axis=0)) + bm_ref[rv * 3 + cv])
            values.append(jnp.concatenate([jnp.concatenate(vparts, axis=0), ones], axis=1))
            dests.append(q_starts)
    ms, accs = _softmax_pv(scores, values)
    for acc, q_starts in zip(accs, dests):
        o = (acc[:, :HEAD_DIM] / acc[:, HEAD_DIM:]).astype(o_ref.dtype)
        for a in range(2):
            o_ref[0, q_starts[a]:q_starts[a] + 2 * SUB, :] = o[a * 2 * SUB:(a + 1) * 2 * SUB, :]


def _c_geometry():
    ql = jnp.arange(QBLK)
    qa, qc, qi, qj = ql // 64, (ql // 32) % 2, (ql // 8) % 4, ql % 8
    q_row = SUB_R * qa + qi
    q_col = SUB_C * qc + qj
    kl = jnp.arange(C_NKEYS)
    ka, kc, ki, kj = kl // C_KSEG, (kl // SUB) % 4, (kl // 8) % 4, kl % 8
    k_row = SUB_R * (ka - 1) + ki
    k_col = SUB_C * (kc - 1) + kj
    return q_row, q_col, k_row, k_col


def _c_bias(rpb):
    q_row, q_col, k_row, k_col = _c_geometry()
    nr, nc = 2 * NA_ROWS - 1, 2 * NA_COLS - 1
    dr = jnp.clip(k_row[None, :] - q_row[:, None] + NA_ROWS - 1, 0, nr - 1)
    dc = jnp.clip(k_col[None, :] - q_col[:, None] + NA_COLS - 1, 0, nc - 1)
    oh_r = (dr[:, :, None] == jnp.arange(nr)).astype(_f32)
    oh_c = (dc[:, :, None] == jnp.arange(nc)).astype(_f32)
    rows = jnp.einsum("hrc,qkr->hqkc", rpb.astype(_f32), oh_r, precision=lax.Precision.HIGHEST)
    return jnp.sum(rows * oh_c[None], axis=-1) * LOG2E


def _c_masks(rows):
    q_row, q_col, k_row, k_col = _c_geometry()
    out = []
    for rb in (0, 1, rows // NA_ROWS - 1):
        qr, kr = NA_ROWS * rb + q_row, NA_ROWS * rb + k_row
        rs = jnp.clip(qr - NA_ROWS // 2, 0, rows - NA_ROWS)
        rvalid = (kr[None, :] >= rs[:, None]) & (kr[None, :] < rs[:, None] + NA_ROWS) & (kr[None, :] >= 0) & (kr[None, :] < rows)
        for cb in (0, 1, GRID_W // NA_COLS - 1):
            qc, kc = NA_COLS * cb + q_col, NA_COLS * cb + k_col
            cs = jnp.clip(qc - NA_COLS // 2, 0, GRID_W - NA_COLS)
            cvalid = (kc[None, :] >= cs[:, None]) & (kc[None, :] < cs[:, None] + NA_COLS) & (kc[None, :] >= 0) & (kc[None, :] < GRID_W)
            out.append(jnp.where(rvalid & cvalid, 0.0, NEG_INF).astype(_f32))
    return jnp.stack(out)


def _attn_c(h, rpb):
    b, l, _ = h.shape
    bias = _c_bias(rpb)
    masks = _c_masks(l // GRID_W)
    return pl.pallas_call(
        _attn_c_kernel,
        out_shape=jax.ShapeDtypeStruct((b, l, MIX_W), _bf16),
        grid=(MIX_HEADS, b),
        in_specs=[pl.BlockSpec((1, l, HEAD_DIM), lambda h, bi: (bi, 0, h)),
                  pl.BlockSpec((1, l, HEAD_DIM), lambda h, bi: (bi, 0, MIX_HEADS + h)),
                  pl.BlockSpec((1, l, HEAD_DIM), lambda h, bi: (bi, 0, 2 * MIX_HEADS + h)),
                  pl.BlockSpec((1, QBLK, C_NKEYS), lambda h, bi: (h, 0, 0)),
                  pl.BlockSpec((9, QBLK, C_NKEYS), lambda h, bi: (0, 0, 0))],
        out_specs=pl.BlockSpec((1, l, HEAD_DIM), lambda h, bi: (bi, 0, h)),
        scratch_shapes=[pltpu.VMEM((9, QBLK, C_NKEYS), _f32)],
        compiler_params=_cparams(2),
    )(h, h, h, bias, masks)


def _subblock_chunks(tm):
    grp = SUB * SUBS_PER_ROW
    pairs = []
    for g in range(tm // grp):
        for c8 in range(SUBS_PER_ROW):
            for i4 in range(SUB_R):
                pairs.append((g * grp + i4 * GRID_W + c8 * SUB_C, (g * SUBS_PER_ROW + c8) * SUB + i4 * SUB_C))
    return pairs


def _out_kernel(om_ref, glo_ref, ghi_ref, mq_ref, kv_ref, x_ref, w_ref, g_ref, b_ref, o_ref, xs_ref, *, subblock):
    half = D_INNER // 2
    ones = jnp.ones((kv_ref.shape[1], HEAD_DIM), _bf16)
    values = [jnp.concatenate([kv_ref[0, :, MEM_W + h * HEAD_DIM:MEM_W + (h + 1) * HEAD_DIM], ones], axis=1)
              for h in range(MEM_HEADS)]

    n_slices = D_INNER // MEM_W
    part_rows = OUT_SUB // n_slices

    def layer_norm_part(r0, q, z_in):
        lo = q * part_rows
        xin = (xs_ref if subblock else x_ref.at[0])[r0 + lo:r0 + lo + part_rows, :]
        z = ALPHA * xin + z_in[lo:lo + part_rows, :]
        mu = jnp.mean(z, axis=-1, keepdims=True)
        zc = z - mu
        var = jnp.mean(zc * zc, axis=-1, keepdims=True)
        out = zc * lax.rsqrt(var + LN_EPS) * g_ref[...] + b_ref[...]
        if subblock:
            for nat, sub in _subblock_chunks(OUT_SUB):
                if lo <= sub < lo + part_rows:
                    o_ref[0, r0 + nat:r0 + nat + SUB_C, :] = out[sub - lo:sub - lo + SUB_C, :]
        else:
            o_ref[0, r0 + lo:r0 + lo + part_rows, :] = out

    pending = None
    for r0 in range(0, om_ref.shape[1], OUT_SUB):
        rows = slice(r0, r0 + OUT_SUB)
        if subblock:
            for nat, sub in _subblock_chunks(OUT_SUB):
                xs_ref[r0 + sub:r0 + sub + SUB_C, :] = x_ref[0, r0 + nat:r0 + nat + SUB_C, :]

        def silu_gate(c0, c1, rows=rows):
            ref, off = (glo_ref, 0) if c0 < half else (ghi_ref, half)
            hg = 0.5 * ref[0, rows, c0 - off:c1 - off].astype(_f32)
            return hg + hg * jnp.tanh(hg)

        scores = [_qk(mq_ref[0, rows, h * HEAD_DIM:(h + 1) * HEAD_DIM],
                      kv_ref[0, :, h * HEAD_DIM:(h + 1) * HEAD_DIM]) * QSCALE for h in range(MEM_HEADS)]
        _, accs = _softmax_pv(scores, values)
        branch = None
        for q, c0 in enumerate(range(0, D_INNER, MEM_W)):
            if c0 < MIX_W:
                y = om_ref[0, rows, c0:c0 + MEM_W].astype(_f32) * silu_gate(c0, c0 + MEM_W)
            else:
                y = jnp.concatenate([acc[:, :HEAD_DIM] / acc[:, HEAD_DIM:] for acc in accs], axis=1)
                y = y * silu_gate(c0, c0 + MEM_W)
            part = jnp.dot(y.astype(_bf16), w_ref[c0:c0 + MEM_W, :], preferred_element_type=_f32)
            branch = part if branch is None else branch + part
            if pending is not None:
                layer_norm_part(pending[0], q, pending[1])
        pending = (r0, branch)
    for q in range(n_slices):
        layer_norm_part(pending[0], q, pending[1])


def _out(o_mix, h, mq_col, kv, x, w_out, ln_g, ln_b, *, subblock):
    b, l, d = x.shape
    tm = OUT_TM
    mlen = kv.shape[1]
    half = D_INNER // 2
    gate_col = mq_col + MEM_W
    assert mq_col % MEM_W == 0 and gate_col % half == 0
    kern = functools.partial(_out_kernel, subblock=subblock)
    return pl.pallas_call(
        kern,
        out_shape=jax.ShapeDtypeStruct((b, l, d), _f32),
        grid=(b, l // tm),
        in_specs=[pl.BlockSpec((1, tm, MIX_W), lambda bi, i: (bi, i, 0)),
                  pl.BlockSpec((1, tm, half), lambda bi, i: (bi, i, gate_col // half)),
                  pl.BlockSpec((1, tm, half), lambda bi, i: (bi, i, gate_col // half + 1)),
                  pl.BlockSpec((1, tm, MEM_W), lambda bi, i: (bi, i, mq_col // MEM_W)),
                  pl.BlockSpec((1, mlen, 2 * MEM_W), lambda bi, i: (bi, 0, 0)),
                  pl.BlockSpec((1, tm, d), lambda bi, i: (bi, i, 0)),
                  pl.BlockSpec((D_INNER, d), lambda bi, i: (0, 0)),
                  pl.BlockSpec((1, d), lambda bi, i: (0, 0)),
                  pl.BlockSpec((1, d), lambda bi, i: (0, 0))],
        out_specs=pl.BlockSpec((1, tm, d), lambda bi, i: (bi, i, 0)),
        scratch_shapes=[pltpu.VMEM((tm, d), _f32)],
        compiler_params=_cparams(2),
    )(o_mix, h, h, h, kv, x, w_out, ln_g.reshape(1, d), ln_b.reshape(1, d))


def _split_cols(w, sizes):
    out, c = [], 0
    for s in sizes:
        out.append(w[:, c:c + s])
        c += s
    return out


def kernel(x, mem, w_in_a, sink_a, w_in_b, w_in_c, rpb_c, w_mkv, w_out, ln_g, ln_b):
    b, l, d = x.shape
    mlen = mem.shape[1]
    kv_all = _memkv(mem.reshape(b * mlen, d), w_mkv.astype(_bf16)).reshape(DEPTH, b, mlen, 2 * MEM_W)
    tables = _rope_tables(l)

    for i in range(DEPTH):
        kind, j = i % NUM_MIXERS, i // NUM_MIXERS
        tail = MEM_W + D_INNER
        if kind == 0:
            seg = ((1, ((MIX_W, "rope_q"), (MEM_W, "rope_k"), (MEM_W + tail, "plain"))),)
            h, = _proj(x, w_in_a[j].astype(_bf16), tables, seg)
            o_mix = _attn_a(h, sink_a[j])
            mq_col = MIX_W + 2 * MEM_W
        elif kind == 1:
            wq0, wq1, wq2, wrest = _split_cols(w_in_b[j].astype(_bf16), (MIX_W, MIX_W, MIX_W, 2 * MIX_W + tail))
            seg = ((B_DILATIONS[2], ((MIX_W, "rope_q"),)),
                   (B_DILATIONS[1], ((MIX_W, "rope_q"),)),
                   (1, ((MIX_W, "rope_q"), (MIX_W, "rope_k"), (MIX_W + tail, "plain"))))
            hq2, hq1, h = _proj(x, jnp.concatenate([wq2, wq1, wq0, wrest], axis=1), tables, seg)
            o_mix = _attn_b(h, hq1, hq2)
            mq_col = 3 * MIX_W
        else:
            seg = ((1, ((MIX_W, "scale"), (2 * MIX_W + tail, "plain"))),)
            h, = _proj(x, w_in_c[j].astype(_bf16), tables, seg, subblock=True)
            o_mix = _attn_c(h, rpb_c[j])
            mq_col = 3 * MIX_W
        x = _out(o_mix, h, mq_col, kv_all[i], x, w_out[i].astype(_bf16), ln_g[i], ln_b[i], subblock=(kind == 2))
    return x
```

```python
import functools
import math

import jax
import jax.numpy as jnp
from jax import lax
from jax.experimental import pallas as pl
from jax.experimental.pallas import tpu as pltpu

D_MODEL = 1024
DEPTH = 4
NUM_MIXERS = 3
HEAD_DIM = 128
D_INNER = 2 * D_MODEL
MEM_HEADS = 4
MIX_HEADS = D_INNER // HEAD_DIM - MEM_HEADS
A_KV_HEADS = MIX_HEADS // 3
A_GROUP = MIX_HEADS // A_KV_HEADS
A_WINDOW = 128
B_DILATIONS = (1, 4, 16)
B_HALF_WINDOW = 64
NA_ROWS = 8
NA_COLS = 16
GRID_W = 64
ROPE_THETA = 500000.0
ROPE_DIMS = HEAD_DIM // 4
ROPE_HALF = ROPE_DIMS // 2
LN_EPS = 1e-5
ALPHA = (2 * DEPTH) ** 0.25
NEG_INF = -1e30
LOG2E = math.log2(math.e)
QSCALE = HEAD_DIM ** -0.5 * LOG2E

MIX_W = MIX_HEADS * HEAD_DIM
MEM_W = MEM_HEADS * HEAD_DIM

SUB_R = 4
SUB_C = 8
SUB = SUB_R * SUB_C
SUBS_PER_ROW = GRID_W // SUB_C
QBLK = 128

V7X_VMEM_BYTES = 64 * 1024 * 1024
VMEM_LIMIT = V7X_VMEM_BYTES * 7 // 8
PROJ_TM_CHOICES = (1024, 512)
PROJ_TN = 512
MEMKV_TM = 512
OUT_TM = 1024
OUT_SUB = 512

_f32 = jnp.float32
_bf16 = jnp.bfloat16


def _cparams(n_grid):
    return pltpu.CompilerParams(dimension_semantics=("arbitrary",) * n_grid,
                                vmem_limit_bytes=VMEM_LIMIT)


def _rope(blk, cos, sin):
    lane = lax.broadcasted_iota(jnp.int32, blk.shape, 1)
    up = pltpu.roll(blk, HEAD_DIM - ROPE_HALF, 1)
    down = pltpu.roll(blk, ROPE_HALF, 1)
    swapped = jnp.where(lane < ROPE_HALF, up, down)
    return blk * cos + swapped * sin


def _proj_kernel(*refs, subblock, segments, has_tables):
    x_ref, w_ref = refs[:2]
    pos = 2
    if has_tables:
        cos_ref, sin_ref = refs[2:4]
        pos = 4
    out_refs = refs[pos:pos + len(segments)]
    xs_ref, tmp_ref = refs[pos + len(segments):]
    tm = xs_ref.shape[0]
    tn = PROJ_TN

    if subblock:
        grp = SUB * SUBS_PER_ROW
        for g in range(tm // grp):
            for c8 in range(SUBS_PER_ROW):
                parts = [x_ref[0, g * grp + i4 * GRID_W + c8 * SUB_C:
                               g * grp + i4 * GRID_W + (c8 + 1) * SUB_C, :] for i4 in range(SUB_R)]
                dst = (g * SUBS_PER_ROW + c8) * SUB
                xs_ref[dst:dst + SUB, :] = jnp.concatenate(parts, axis=0).astype(_bf16)
    else:
        xs_ref[...] = x_ref[0].astype(_bf16)

    def write(o_ref, dil, val, c0):
        c1 = c0 + HEAD_DIM
        if dil == 1:
            o_ref[0, :, c0:c1] = val.astype(o_ref.dtype)
            return
        n = tm // dil
        tmp_ref[0] = val
        if dil == B_DILATIONS[2]:
            d1 = B_DILATIONS[1]
            n1 = tm // d1
            for r in range(d1):
                tmp_ref[1, r * n1:(r + 1) * n1, :] = tmp_ref[0, pl.ds(r, n1, stride=d1), :]
            for r in range(dil):
                rows = tmp_ref[1, pl.ds((r % d1) * n1 + r // d1, n, stride=d1), :]
                o_ref[0, r, :, c0:c1] = rows.astype(o_ref.dtype)
        else:
            for r in range(dil):
                o_ref[0, r, :, c0:c1] = tmp_ref[0, pl.ds(r, n, stride=dil), :].astype(o_ref.dtype)

    wcol = 0
    for o_ref, (dil, parts) in zip(out_refs, segments):
        ocol = 0
        for n_cols, mode in parts:
            for _ in range(n_cols // tn):
                acc = jnp.dot(xs_ref[...], w_ref[:, wcol:wcol + tn], preferred_element_type=_f32)
                for hd in range(tn // HEAD_DIM):
                    val = acc[:, hd * HEAD_DIM:(hd + 1) * HEAD_DIM]
                    if mode == "rope_q":
                        val = _rope(val, cos_ref[0], sin_ref[0])
                    elif mode == "rope_k":
                        val = _rope(val, cos_ref[1], sin_ref[1])
                    elif mode == "scale":
                        val = val * QSCALE
                    write(o_ref, dil, val, ocol + hd * HEAD_DIM)
                wcol += tn
                ocol += tn


def _proj_rows(d, n, has_tables):
    for tm in PROJ_TM_CHOICES:
        need = (d * n * 2 + 2 * tm * n * 2 + 2 * tm * d * 4 + tm * d * 2
                + (2 * 2 * 2 * tm * HEAD_DIM * 4 if has_tables else 0)
                + 2 * tm * HEAD_DIM * 4 + tm * PROJ_TN * 4)
        if need <= VMEM_LIMIT:
            return tm
    raise ValueError("projection weight does not fit in VMEM")


def _proj(x, w, tables, segments, *, subblock=False):
    b, l, d = x.shape
    tn = PROJ_TN
    widths = [sum(n for n, _ in parts) for _, parts in segments]
    assert sum(widths) == w.shape[1] and all(n % tn == 0 for _, parts in segments for n, _ in parts)
    has_tables = any(mode.startswith("rope") for _, parts in segments for _, mode in parts)
    tm = _proj_rows(d, w.shape[1], has_tables)
    tiles = l // tm
    assert l % tm == 0
    out_shapes, out_specs = [], []
    for (dil, _), n in zip(segments, widths):
        if dil == 1:
            out_shapes.append(jax.ShapeDtypeStruct((b, l, n), _bf16))
            out_specs.append(pl.BlockSpec((1, tm, n), lambda i: (i // tiles, i % tiles, 0)))
        else:
            out_shapes.append(jax.ShapeDtypeStruct((b, dil, l // dil, n), _bf16))
            out_specs.append(pl.BlockSpec((1, dil, tm // dil, n), lambda i: (i // tiles, 0, i % tiles, 0)))
    in_specs = [pl.BlockSpec((1, tm, d), lambda i: (i // tiles, i % tiles, 0)),
                pl.BlockSpec((d, w.shape[1]), lambda i: (0, 0), pipeline_mode=pl.Buffered(1))]
    args = [x, w]
    if has_tables:
        tab_spec = pl.BlockSpec((2, tm, HEAD_DIM), lambda i: (0, i % tiles, 0))
        in_specs += [tab_spec, tab_spec]
        args += list(tables)
    kern = functools.partial(_proj_kernel, subblock=subblock, segments=segments, has_tables=has_tables)
    return pl.pallas_call(
        kern,
        out_shape=out_shapes,
        grid=(b * tiles,),
        in_specs=in_specs,
        out_specs=out_specs,
        scratch_shapes=[pltpu.VMEM((tm, d), _bf16), pltpu.VMEM((2, tm, HEAD_DIM), _f32)],
        compiler_params=_cparams(1),
    )(*args)


def _rope_tables(l):
    inv = ROPE_THETA ** (-jnp.arange(ROPE_HALF, dtype=_f32) / ROPE_HALF)
    ang = jnp.arange(l).astype(_f32)[:, None] * inv[None, :]
    cos, sin = jnp.cos(ang), jnp.sin(ang)
    pad = HEAD_DIM - ROPE_DIMS
    cos_t = jnp.concatenate([cos, cos, jnp.ones((l, pad), _f32)], axis=1)
    sin_t = jnp.concatenate([-sin, sin, jnp.zeros((l, pad), _f32)], axis=1)
    return jnp.stack([cos_t * QSCALE, cos_t]), jnp.stack([sin_t * QSCALE, sin_t])


def _memkv_kernel(m_ref, w_ref, o_ref):
    o_ref[0] = jnp.dot(m_ref[...].astype(_bf16), w_ref[0], preferred_element_type=_f32).astype(o_ref.dtype)


def _memkv(mem2d, w_mkv):
    rows, d = mem2d.shape
    depth, _, n = w_mkv.shape
    tm = min(rows, MEMKV_TM)
    assert rows % tm == 0
    return pl.pallas_call(
        _memkv_kernel,
        out_shape=jax.ShapeDtypeStruct((depth, rows, n), _bf16),
        grid=(depth, rows // tm),
        in_specs=[pl.BlockSpec((tm, d), lambda li, i: (i, 0)),
                  pl.BlockSpec((1, d, n), lambda li, i: (li, 0, 0))],
        out_specs=pl.BlockSpec((1, tm, n), lambda li, i: (li, i, 0)),
        compiler_params=_cparams(2),
    )(mem2d, w_mkv)


def _qk(q, k):
    return lax.dot_general(q, k, (((1,), (1,)), ((), ())), preferred_element_type=_f32)


def _first_step():
    return (pl.program_id(0) == 0) & (pl.program_id(1) == 0)


def _band_masks(mask_ref, nk, half, deltas):
    rel = (lax.broadcasted_iota(jnp.int32, (QBLK, nk), 1) - lax.broadcasted_iota(jnp.int32, (QBLK, nk), 0))
    for v, delta in enumerate(deltas):
        mask_ref[v] = jnp.where(jnp.abs(rel + delta) <= half, 0.0, NEG_INF).astype(_f32)


def _edge_variant(i, n):
    return 0 if i == 0 else (2 if i == n - 1 else 1)


def _softmax_pv(scores, values):
    def rowmax(s):
        m = s[:, 0:HEAD_DIM]
        for c in range(1, s.shape[1] // HEAD_DIM):
            m = jnp.maximum(m, s[:, c * HEAD_DIM:(c + 1) * HEAD_DIM])
        return jnp.max(m, axis=-1, keepdims=True)

    ms = [rowmax(s) for s in scores]
    ps = [jnp.exp2(s - m).astype(_bf16) for s, m in zip(scores, ms)]
    accs = [jnp.dot(p, v, preferred_element_type=_f32) for p, v in zip(ps, values)]
    return ms, accs


def _attn_a_kernel(q_ref, k_ref, v_ref, sink_ref, o_ref, mask_ref, sink2_ref):
    l = k_ref.shape[1]
    nk = 3 * A_WINDOW
    nblk = l // QBLK
    ones = jnp.ones((nk, HEAD_DIM), _bf16)

    @pl.when(_first_step())
    def _():
        rel = (lax.broadcasted_iota(jnp.int32, (QBLK, nk), 1) - lax.broadcasted_iota(jnp.int32, (QBLK, nk), 0))
        for v, delta in enumerate((0, -A_WINDOW, -2 * A_WINDOW)):
            blk = jnp.where(jnp.abs(rel + delta) <= A_WINDOW, 0.0, NEG_INF).astype(_f32)
            for a in range(A_GROUP):
                mask_ref[v, a * QBLK:(a + 1) * QBLK, :] = blk

    for a in range(A_GROUP):
        sink2_ref[a * QBLK:(a + 1) * QBLK, :] = jnp.broadcast_to(sink_ref[0, a:a + 1, :] * LOG2E, (QBLK, HEAD_DIM))

    scores, windows = [], []
    for i in range(nblk):
        q0 = i * QBLK
        ks = min(max(q0 - A_WINDOW, 0), l - nk)
        q = jnp.concatenate([q_ref[0, q0:q0 + QBLK, a * HEAD_DIM:(a + 1) * HEAD_DIM]
                             for a in range(A_GROUP)], axis=0)
        scores.append(_qk(q, k_ref[0, ks:ks + nk, :]) + mask_ref[_edge_variant(i, nblk)])
        windows.append((q0, ks))
    cols = range(nk // HEAD_DIM)
    ms = []
    for s in scores:
        m = s[:, 0:HEAD_DIM]
        for c in cols[1:]:
            m = jnp.maximum(m, s[:, c * HEAD_DIM:(c + 1) * HEAD_DIM])
        m = jnp.broadcast_to(jnp.max(m, axis=-1, keepdims=True), m.shape)
        ms.append(jnp.maximum(m, sink2_ref[...]))
    ps = [jnp.concatenate([jnp.exp2(s[:, c * HEAD_DIM:(c + 1) * HEAD_DIM] - m) for c in cols], axis=1).astype(_bf16)
          for s, m in zip(scores, ms)]
    accs = [jnp.dot(p, jnp.concatenate([v_ref[0, ks:ks + nk, :], ones], axis=1), preferred_element_type=_f32)
            for p, (_, ks) in zip(ps, windows)]
    for m, acc, (q0, _) in zip(ms, accs, windows):
        z = acc[:, HEAD_DIM:] + jnp.exp2(sink2_ref[...] - m)
        o = (acc[:, :HEAD_DIM] / z).astype(o_ref.dtype)
        for a in range(A_GROUP):
            o_ref[0, q0:q0 + QBLK, a * HEAD_DIM:(a + 1) * HEAD_DIM] = o[a * QBLK:(a + 1) * QBLK, :]


def _attn_a(h, sink):
    b, l, _ = h.shape
    gw = A_GROUP * HEAD_DIM
    sink_b = jnp.broadcast_to(sink.astype(_f32).reshape(A_KV_HEADS, A_GROUP, 1), (A_KV_HEADS, A_GROUP, HEAD_DIM))
    sink_b = jnp.pad(sink_b, ((0, 0), (0, 8 - A_GROUP), (0, 0)))
    return pl.pallas_call(
        _attn_a_kernel,
        out_shape=jax.ShapeDtypeStruct((b, l, MIX_W), _bf16),
        grid=(b, A_KV_HEADS),
        in_specs=[pl.BlockSpec((1, l, gw), lambda bi, g: (bi, 0, g)),
                  pl.BlockSpec((1, l, HEAD_DIM), lambda bi, g: (bi, 0, MIX_HEADS + g)),
                  pl.BlockSpec((1, l, HEAD_DIM), lambda bi, g: (bi, 0, MIX_HEADS + A_KV_HEADS + g)),
                  pl.BlockSpec((1, 8, HEAD_DIM), lambda bi, g: (g, 0, 0))],
        out_specs=pl.BlockSpec((1, l, gw), lambda bi, g: (bi, 0, g)),
        scratch_shapes=[pltpu.VMEM((3, A_GROUP * QBLK, 3 * A_WINDOW), _f32),
                        pltpu.VMEM((A_GROUP * QBLK, HEAD_DIM), _f32)],
        compiler_params=_cparams(2),
    )(h, h, h, sink_b)


def _attn_b_kernel(q0_ref, k_ref, v_ref, q1_ref, q2_ref, o_ref,
                   kd_ref, vd_ref, tmp_ref, tmp2_ref, og_ref, lse_ref, mask_ref):
    l = o_ref.shape[1]
    nk = 4 * B_HALF_WINDOW
    q_refs = (q0_ref, q1_ref, q2_ref)
    ones = jnp.ones((nk, HEAD_DIM), _bf16)

    @pl.when(_first_step())
    def _():
        _band_masks(mask_ref, nk, B_HALF_WINDOW, (0, -B_HALF_WINDOW, -2 * B_HALF_WINDOW))

    d1 = B_DILATIONS[1]
    assert B_DILATIONS == (1, d1, d1 * d1)
    n1, n2 = l // d1, l // (d1 * d1)
    for src, dst in ((k_ref, kd_ref), (v_ref, vd_ref)):
        def put(g, row0, n, rows):
            dst[g - 1, row0:row0 + n, :] = rows.astype(_bf16)

        tmp_ref[...] = src[0].astype(_f32)
        for r in range(d1):
            rows = tmp_ref[pl.ds(r, n1, stride=d1), :]
            tmp2_ref[r * n1:(r + 1) * n1, :] = rows
            put(1, r * n1, n1, rows)
        for r in range(d1 * d1):
            put(2, r * n2, n2, tmp2_ref[pl.ds((r % d1) * n1 + r // d1, n2, stride=d1), :])

    for g, dil in reversed(list(enumerate(B_DILATIONS))):
        n = l // dil
        blocks = n // QBLK
        scores, values, dests = [], [], []
        for r in range(dil):
            for i in range(blocks):
                q0 = i * QBLK
                ks = min(max(q0 - B_HALF_WINDOW, 0), n - nk)
                base = r * n + ks
                if g == 0:
                    q = q0_ref[0, q0:q0 + QBLK, :]
                    kw, vw = k_ref[0, ks:ks + nk, :], v_ref[0, ks:ks + nk, :]
                else:
                    q = q_refs[g][0, r, q0:q0 + QBLK, :]
                    kw, vw = kd_ref[g - 1, base:base + nk, :], vd_ref[g - 1, base:base + nk, :]
                scores.append(_qk(q, kw) + mask_ref[_edge_variant(i, blocks)])
                values.append(jnp.concatenate([vw, ones], axis=1))
                dests.append((q0, r))
        ms, accs = _softmax_pv(scores, values)
        for m, acc, (q0, r) in zip(ms, accs, dests):
            z = acc[:, HEAD_DIM:]
            if g == 2:
                dst = pl.ds((r % d1) * n1 + q0 * d1 + r // d1, QBLK, stride=d1)
                og_ref[0, dst, :] = acc[:, :HEAD_DIM] / z
                lse_ref[0, dst, :] = m + jnp.log2(z)
                continue
            prev = pl.ds(r * n + q0, QBLK)
            lp = lse_ref[g - 1 if g else 1, prev, :]
            top = jnp.maximum(m, lp)
            w, wp = jnp.exp2(m - top), jnp.exp2(lp - top)
            num = w * acc[:, :HEAD_DIM] + wp * og_ref[g - 1 if g else 1, prev, :]
            den = w * z + wp
            if g == 1:
                dst = pl.ds(q0 * dil + r, QBLK, stride=dil)
                og_ref[1, dst, :] = num / den
                lse_ref[1, dst, :] = top + jnp.log2(den)
            else:
                o_ref[0, q0:q0 + QBLK, :] = (num / den).astype(o_ref.dtype)


def _attn_b(h0, hq1, hq2):
    b, l, _ = h0.shape
    in_specs = [pl.BlockSpec((1, l, HEAD_DIM), lambda bi, h: (bi, 0, h)),
                pl.BlockSpec((1, l, HEAD_DIM), lambda bi, h: (bi, 0, MIX_HEADS + h)),
                pl.BlockSpec((1, l, HEAD_DIM), lambda bi, h: (bi, 0, 2 * MIX_HEADS + h))]
    for dil in B_DILATIONS[1:]:
        in_specs.append(pl.BlockSpec((1, dil, l // dil, HEAD_DIM), lambda bi, h: (bi, 0, 0, h)))
    return pl.pallas_call(
        _attn_b_kernel,
        out_shape=jax.ShapeDtypeStruct((b, l, MIX_W), _bf16),
        grid=(b, MIX_HEADS),
        in_specs=in_specs,
        out_specs=pl.BlockSpec((1, l, HEAD_DIM), lambda bi, h: (bi, 0, h)),
        scratch_shapes=[pltpu.VMEM((2, l, HEAD_DIM), _bf16),
                        pltpu.VMEM((2, l, HEAD_DIM), _bf16),
                        pltpu.VMEM((l, HEAD_DIM), _f32),
                        pltpu.VMEM((l, HEAD_DIM), _f32),
                        pltpu.VMEM((2, l, HEAD_DIM), _f32),
                        pltpu.VMEM((2, l, HEAD_DIM), _f32),
                        pltpu.VMEM((3, QBLK, 4 * B_HALF_WINDOW), _f32)],
        compiler_params=_cparams(2),
    )(h0, h0, h0, hq1, hq2)


C_KSEG = 4 * SUB
C_NSEG = 4
C_NKEYS = C_NSEG * C_KSEG


def _attn_c_kernel(q_ref, k_ref, v_ref, bias_ref, mask_ref, o_ref, bm_ref):
    l = k_ref.shape[1]
    n_rb = l // GRID_W // NA_ROWS
    n_cb = GRID_W // NA_COLS
    n_r4 = l // GRID_W // SUB_R
    ones = jnp.ones((C_NKEYS, HEAD_DIM), _bf16)

    @pl.when(pl.program_id(1) == 0)
    def _():
        for v in range(9):
            bm_ref[v] = bias_ref[0] + mask_ref[v]

    def segment(ref, st):
        lo, hi = max(st, 0), min(st + C_KSEG, l)
        parts = [jnp.zeros((lo - st, HEAD_DIM), _bf16)] if lo > st else []
        parts.append(ref[0, lo:hi, :])
        if hi < st + C_KSEG:
            parts.append(jnp.zeros((st + C_KSEG - hi, HEAD_DIM), _bf16))
        return parts

    scores, values, dests = [], [], []
    for rb in range(n_rb):
        for cb in range(n_cb):
            rv, cv = _edge_variant(rb, n_rb), _edge_variant(cb, n_cb)
            q_starts = [((2 * rb + a) * SUBS_PER_ROW + 2 * cb) * SUB for a in range(2)]
            q = jnp.concatenate([q_ref[0, qs:qs + 2 * SUB, :] for qs in q_starts], axis=0)
            kparts, vparts = [], []
            for ar in range(C_NSEG):
                r4 = min(max(2 * rb - 1 + ar, 0), n_r4 - 1)
                st = (r4 * SUBS_PER_ROW + 2 * cb - 1) * SUB
                kparts += segment(k_ref, st)
                vparts += segment(v_ref, st)
            scores.append(_qk(q, jnp.concatenate(kparts, axis=0)) + bm_ref[rv * 3 + cv])
            values.append(jnp.concatenate([jnp.concatenate(vparts, axis=0), ones], axis=1))
            dests.append(q_starts)
    ms, accs = _softmax_pv(scores, values)
    for acc, q_starts in zip(accs, dests):
        o = (acc[:, :HEAD_DIM] / acc[:, HEAD_DIM:]).astype(o_ref.dtype)
        for a in range(2):
            o_ref[0, q_starts[a]:q_starts[a] + 2 * SUB, :] = o[a * 2 * SUB:(a + 1) * 2 * SUB, :]


def _c_geometry():
    ql = jnp.arange(QBLK)
    qa, qc, qi, qj = ql // 64, (ql // 32) % 2, (ql // 8) % 4, ql % 8
    q_row = SUB_R * qa + qi
    q_col = SUB_C * qc + qj
    kl = jnp.arange(C_NKEYS)
    ka, kc, ki, kj = kl // C_KSEG, (kl // SUB) % 4, (kl // 8) % 4, kl % 8
    k_row = SUB_R * (ka - 1) + ki
    k_col = SUB_C * (kc - 1) + kj
    return q_row, q_col, k_row, k_col


def _c_bias(rpb):
    q_row, q_col, k_row, k_col = _c_geometry()
    nr, nc = 2 * NA_ROWS - 1, 2 * NA_COLS - 1
    dr = jnp.clip(k_row[None, :] - q_row[:, None] + NA_ROWS - 1, 0, nr - 1)
    dc = jnp.clip(k_col[None, :] - q_col[:, None] + NA_COLS - 1, 0, nc - 1)
    oh_r = (dr[:, :, None] == jnp.arange(nr)).astype(_f32)
    oh_c = (dc[:, :, None] == jnp.arange(nc)).astype(_f32)
    rows = jnp.einsum("hrc,qkr->hqkc", rpb.astype(_f32), oh_r, precision=lax.Precision.HIGHEST)
    return jnp.sum(rows * oh_c[None], axis=-1) * LOG2E


def _c_masks(rows):
    q_row, q_col, k_row, k_col = _c_geometry()
    out = []
    for rb in (0, 1, rows // NA_ROWS - 1):
        qr, kr = NA_ROWS * rb + q_row, NA_ROWS * rb + k_row
        rs = jnp.clip(qr - NA_ROWS // 2, 0, rows - NA_ROWS)
        rvalid = (kr[None, :] >= rs[:, None]) & (kr[None, :] < rs[:, None] + NA_ROWS) & (kr[None, :] >= 0) & (kr[None, :] < rows)
        for cb in (0, 1, GRID_W // NA_COLS - 1):
            qc, kc = NA_COLS * cb + q_col, NA_COLS * cb + k_col
            cs = jnp.clip(qc - NA_COLS // 2, 0, GRID_W - NA_COLS)
            cvalid = (kc[None, :] >= cs[:, None]) & (kc[None, :] < cs[:, None] + NA_COLS) & (kc[None, :] >= 0) & (kc[None, :] < GRID_W)
            out.append(jnp.where(rvalid & cvalid, 0.0, NEG_INF).astype(_f32))
    return jnp.stack(out)


def _attn_c(h, rpb):
    b, l, _ = h.shape
    bias = _c_bias(rpb)
    masks = _c_masks(l // GRID_W)
    return pl.pallas_call(
        _attn_c_kernel,
        out_shape=jax.ShapeDtypeStruct((b, l, MIX_W), _bf16),
        grid=(MIX_HEADS, b),
        in_specs=[pl.BlockSpec((1, l, HEAD_DIM), lambda h, bi: (bi, 0, h)),
                  pl.BlockSpec((1, l, HEAD_DIM), lambda h, bi: (bi, 0, MIX_HEADS + h)),
                  pl.BlockSpec((1, l, HEAD_DIM), lambda h, bi: (bi, 0, 2 * MIX_HEADS + h)),
                  pl.BlockSpec((1, QBLK, C_NKEYS), lambda h, bi: (h, 0, 0)),
                  pl.BlockSpec((9, QBLK, C_NKEYS), lambda h, bi: (0, 0, 0))],
        out_specs=pl.BlockSpec((1, l, HEAD_DIM), lambda h, bi: (bi, 0, h)),
        scratch_shapes=[pltpu.VMEM((9, QBLK, C_NKEYS), _f32)],
        compiler_params=_cparams(2),
    )(h, h, h, bias, masks)


def _subblock_chunks(tm):
    grp = SUB * SUBS_PER_ROW
    pairs = []
    for g in range(tm // grp):
        for c8 in range(SUBS_PER_ROW):
            for i4 in range(SUB_R):
                pairs.append((g * grp + i4 * GRID_W + c8 * SUB_C, (g * SUBS_PER_ROW + c8) * SUB + i4 * SUB_C))
    return pairs


def _out_kernel(om_ref, glo_ref, ghi_ref, mq_ref, kv_ref, x_ref, w_ref, g_ref, b_ref, o_ref, xs_ref, *, subblock):
    half = D_INNER // 2
    ones = jnp.ones((kv_ref.shape[1], HEAD_DIM), _bf16)
    values = [jnp.concatenate([kv_ref[0, :, MEM_W + h * HEAD_DIM:MEM_W + (h + 1) * HEAD_DIM], ones], axis=1)
              for h in range(MEM_HEADS)]

    n_slices = D_INNER // MEM_W
    part_rows = OUT_SUB // n_slices

    def layer_norm_part(r0, q, z_in):
        lo = q * part_rows
        xin = (xs_ref if subblock else x_ref.at[0])[r0 + lo:r0 + lo + part_rows, :]
        z = ALPHA * xin + z_in[lo:lo + part_rows, :]
        mu = jnp.mean(z, axis=-1, keepdims=True)
        zc = z - mu
        var = jnp.mean(zc * zc, axis=-1, keepdims=True)
        out = zc * lax.rsqrt(var + LN_EPS) * g_ref[...] + b_ref[...]
        if subblock:
            for nat, sub in _subblock_chunks(OUT_SUB):
                if lo <= sub < lo + part_rows:
                    o_ref[0, r0 + nat:r0 + nat + SUB_C, :] = out[sub - lo:sub - lo + SUB_C, :]
        else:
            o_ref[0, r0 + lo:r0 + lo + part_rows, :] = out

    pending = None
    for r0 in range(0, om_ref.shape[1], OUT_SUB):
        rows = slice(r0, r0 + OUT_SUB)
        if subblock:
            for nat, sub in _subblock_chunks(OUT_SUB):
                xs_ref[r0 + sub:r0 + sub + SUB_C, :] = x_ref[0, r0 + nat:r0 + nat + SUB_C, :]

        def silu_gate(c0, c1, rows=rows):
            ref, off = (glo_ref, 0) if c0 < half else (ghi_ref, half)
            hg = 0.5 * ref[0, rows, c0 - off:c1 - off].astype(_f32)
            return hg + hg * jnp.tanh(hg)

        scores = [_qk(mq_ref[0, rows, h * HEAD_DIM:(h + 1) * HEAD_DIM],
                      kv_ref[0, :, h * HEAD_DIM:(h + 1) * HEAD_DIM]) * QSCALE for h in range(MEM_HEADS)]
        _, accs = _softmax_pv(scores, values)
        branch = None
        for q, c0 in enumerate(range(0, D_INNER, MEM_W)):
            if c0 < MIX_W:
                y = om_ref[0, rows, c0:c0 + MEM_W].astype(_f32) * silu_gate(c0, c0 + MEM_W)
            else:
                y = jnp.concatenate([acc[:, :HEAD_DIM] / acc[:, HEAD_DIM:] for acc in accs], axis=1)
                y = y * silu_gate(c0, c0 + MEM_W)
            part = jnp.dot(y.astype(_bf16), w_ref[c0:c0 + MEM_W, :], preferred_element_type=_f32)
            branch = part if branch is None else branch + part
            if pending is not None:
                layer_norm_part(pending[0], q, pending[1])
        pending = (r0, branch)
    for q in range(n_slices):
        layer_norm_part(pending[0], q, pending[1])


def _out(o_mix, h, mq_col, kv, x, w_out, ln_g, ln_b, *, subblock):
    b, l, d = x.shape
    tm = OUT_TM
    mlen = kv.shape[1]
    half = D_INNER // 2
    gate_col = mq_col + MEM_W
    assert mq_col % MEM_W == 0 and gate_col % half == 0
    kern = functools.partial(_out_kernel, subblock=subblock)
    return pl.pallas_call(
        kern,
        out_shape=jax.ShapeDtypeStruct((b, l, d), _f32),
        grid=(b, l // tm),
        in_specs=[pl.BlockSpec((1, tm, MIX_W), lambda bi, i: (bi, i, 0)),
                  pl.BlockSpec((1, tm, half), lambda bi, i: (bi, i, gate_col // half)),
                  pl.BlockSpec((1, tm, half), lambda bi, i: (bi, i, gate_col // half + 1)),
                  pl.BlockSpec((1, tm, MEM_W), lambda bi, i: (bi, i, mq_col // MEM_W)),
                  pl.BlockSpec((1, mlen, 2 * MEM_W), lambda bi, i: (bi, 0, 0)),
                  pl.BlockSpec((1, tm, d), lambda bi, i: (bi, i, 0)),
                  pl.BlockSpec((D_INNER, d), lambda bi, i: (0, 0)),
                  pl.BlockSpec((1, d), lambda bi, i: (0, 0)),
                  pl.BlockSpec((1, d), lambda bi, i: (0, 0))],
        out_specs=pl.BlockSpec((1, tm, d), lambda bi, i: (bi, i, 0)),
        scratch_shapes=[pltpu.VMEM((tm, d), _f32)],
        compiler_params=_cparams(2),
    )(o_mix, h, h, h, kv, x, w_out, ln_g.reshape(1, d), ln_b.reshape(1, d))


def _split_cols(w, sizes):
    out, c = [], 0
    for s in sizes:
        out.append(w[:, c:c + s])
        c += s
    return out


def kernel(x, mem, w_in_a, sink_a, w_in_b, w_in_c, rpb_c, w_mkv, w_out, ln_g, ln_b):
    b, l, d = x.shape
    mlen = mem.shape[1]
    kv_all = _memkv(mem.reshape(b * mlen, d), w_mkv.astype(_bf16)).reshape(DEPTH, b, mlen, 2 * MEM_W)
    tables = _rope_tables(l)

    for i in range(DEPTH):
        kind, j = i % NUM_MIXERS, i // NUM_MIXERS
        tail = MEM_W + D_INNER
        if kind == 0:
            seg = ((1, ((MIX_W, "rope_q"), (MEM_W, "rope_k"), (MEM_W + tail, "plain"))),)
            h, = _proj(x, w_in_a[j].astype(_bf16), tables, seg)
            o_mix = _attn_a(h, sink_a[j])
            mq_col = MIX_W + 2 * MEM_W
        elif kind == 1:
            wq0, wq1, wq2, wrest = _split_cols(w_in_b[j].astype(_bf16), (MIX_W, MIX_W, MIX_W, 2 * MIX_W + tail))
            seg = ((B_DILATIONS[2], ((MIX_W, "rope_q"),)),
                   (B_DILATIONS[1], ((MIX_W, "rope_q"),)),
                   (1, ((MIX_W, "rope_q"), (MIX_W, "rope_k"), (MIX_W + tail, "plain"))))
            hq2, hq1, h = _proj(x, jnp.concatenate([wq2, wq1, wq0, wrest], axis=1), tables, seg)
            o_mix = _attn_b(h, hq1, hq2)
            mq_col = 3 * MIX_W
        else:
            seg = ((1, ((MIX_W, "scale"), (2 * MIX_W + tail, "plain"))),)
            h, = _proj(x, w_in_c[j].astype(_bf16), tables, seg, subblock=True)
            o_mix = _attn_c(h, rpb_c[j])
            mq_col = 3 * MIX_W
        x = _out(o_mix, h, mq_col, kv_all[i], x, w_out[i].astype(_bf16), ln_g[i], ln_b[i], subblock=(kind == 2))
    return x
```

```python
import functools
import math

import jax
import jax.numpy as jnp
from jax import lax
from jax.experimental import pallas as pl
from jax.experimental.pallas import tpu as pltpu

D_MODEL = 1024
DEPTH = 4
NUM_MIXERS = 3
HEAD_DIM = 128
D_INNER = 2 * D_MODEL
MEM_HEADS = 4
MIX_HEADS = D_INNER // HEAD_DIM - MEM_HEADS
A_KV_HEADS = MIX_HEADS // 3
A_GROUP = MIX_HEADS // A_KV_HEADS
A_WINDOW = 128
B_DILATIONS = (1, 4, 16)
B_HALF_WINDOW = 64
NA_ROWS = 8
NA_COLS = 16
GRID_W = 64
ROPE_THETA = 500000.0
ROPE_DIMS = HEAD_DIM // 4
ROPE_HALF = ROPE_DIMS // 2
LN_EPS = 1e-5
ALPHA = (2 * DEPTH) ** 0.25
NEG_INF = -1e30
LOG2E = math.log2(math.e)
QSCALE = HEAD_DIM ** -0.5 * LOG2E

MIX_W = MIX_HEADS * HEAD_DIM
MEM_W = MEM_HEADS * HEAD_DIM

SUB_R = 4
SUB_C = 8
SUB = SUB_R * SUB_C
SUBS_PER_ROW = GRID_W // SUB_C
QBLK = 128

V7X_VMEM_BYTES = 64 * 1024 * 1024
VMEM_LIMIT = V7X_VMEM_BYTES * 7 // 8
PROJ_TM_CHOICES = (1024, 512)
PROJ_TN = 512
MEMKV_TM = 512
OUT_TM = 1024
OUT_SUB = 512

_f32 = jnp.float32
_bf16 = jnp.bfloat16


def _cparams(n_grid):
    return pltpu.CompilerParams(dimension_semantics=("arbitrary",) * n_grid,
                                vmem_limit_bytes=VMEM_LIMIT)


def _rope(blk, cos, sin):
    lane = lax.broadcasted_iota(jnp.int32, blk.shape, 1)
    up = pltpu.roll(blk, HEAD_DIM - ROPE_HALF, 1)
    down = pltpu.roll(blk, ROPE_HALF, 1)
    swapped = jnp.where(lane < ROPE_HALF, up, down)
    return blk * cos + swapped * sin


def _proj_kernel(*refs, subblock, segments, has_tables):
    x_ref, w_ref = refs[:2]
    pos = 2
    if has_tables:
        cos_ref, sin_ref = refs[2:4]
        pos = 4
    out_refs = refs[pos:pos + len(segments)]
    xs_ref, tmp_ref = refs[pos + len(segments):]
    tm = xs_ref.shape[0]
    tn = PROJ_TN

    if subblock:
        grp = SUB * SUBS_PER_ROW
        for g in range(tm // grp):
            for c8 in range(SUBS_PER_ROW):
                parts = [x_ref[0, g * grp + i4 * GRID_W + c8 * SUB_C:
                               g * grp + i4 * GRID_W + (c8 + 1) * SUB_C, :] for i4 in range(SUB_R)]
                dst = (g * SUBS_PER_ROW + c8) * SUB
                xs_ref[dst:dst + SUB, :] = jnp.concatenate(parts, axis=0).astype(_bf16)
    else:
        xs_ref[...] = x_ref[0].astype(_bf16)

    def write(o_ref, dil, val, c0):
        c1 = c0 + HEAD_DIM
        if dil == 1:
            o_ref[0, :, c0:c1] = val.astype(o_ref.dtype)
            return
        n = tm // dil
        tmp_ref[0] = val
        if dil == B_DILATIONS[2]:
            d1 = B_DILATIONS[1]
            n1 = tm // d1
            for r in range(d1):
                tmp_ref[1, r * n1:(r + 1) * n1, :] = tmp_ref[0, pl.ds(r, n1, stride=d1), :]
            for r in range(dil):
                rows = tmp_ref[1, pl.ds((r % d1) * n1 + r // d1, n, stride=d1), :]
                o_ref[0, r, :, c0:c1] = rows.astype(o_ref.dtype)
        else:
            for r in range(dil):
                o_ref[0, r, :, c0:c1] = tmp_ref[0, pl.ds(r, n, stride=dil), :].astype(o_ref.dtype)

    wcol = 0
    for o_ref, (dil, parts) in zip(out_refs, segments):
        ocol = 0
        for n_cols, mode in parts:
            for _ in range(n_cols // tn):
                acc = jnp.dot(xs_ref[...], w_ref[:, wcol:wcol + tn], preferred_element_type=_f32)
                for hd in range(tn // HEAD_DIM):
                    val = acc[:, hd * HEAD_DIM:(hd + 1) * HEAD_DIM]
                    if mode == "rope_q":
                        val = _rope(val, cos_ref[0], sin_ref[0])
                    elif mode == "rope_k":
                        val = _rope(val, cos_ref[1], sin_ref[1])
                    elif mode == "scale":
                        val = val * QSCALE
                    write(o_ref, dil, val, ocol + hd * HEAD_DIM)
                wcol += tn
                ocol += tn


def _proj_rows(d, n, has_tables):
    for tm in PROJ_TM_CHOICES:
        need = (d * n * 2 + 2 * tm * n * 2 + 2 * tm * d * 4 + tm * d * 2
                + (2 * 2 * 2 * tm * HEAD_DIM * 4 if has_tables else 0)
                + 2 * tm * HEAD_DIM * 4 + tm * PROJ_TN * 4)
        if need <= VMEM_LIMIT:
            return tm
    raise ValueError("projection weight does not fit in VMEM")


def _proj(x, w, tables, segments, *, subblock=False):
    b, l, d = x.shape
    tn = PROJ_TN
    widths = [sum(n for n, _ in parts) for _, parts in segments]
    assert sum(widths) == w.shape[1] and all(n % tn == 0 for _, parts in segments for n, _ in parts)
    has_tables = any(mode.startswith("rope") for _, parts in segments for _, mode in parts)
    tm = _proj_rows(d, w.shape[1], has_tables)
    tiles = l // tm
    assert l % tm == 0
    out_shapes, out_specs = [], []
    for (dil, _), n in zip(segments, widths):
        if dil == 1:
            out_shapes.append(jax.ShapeDtypeStruct((b, l, n), _bf16))
            out_specs.append(pl.BlockSpec((1, tm, n), lambda i: (i // tiles, i % tiles, 0)))
        else:
            out_shapes.append(jax.ShapeDtypeStruct((b, dil, l // dil, n), _bf16))
            out_specs.append(pl.BlockSpec((1, dil, tm // dil, n), lambda i: (i // tiles, 0, i % tiles, 0)))
    in_specs = [pl.BlockSpec((1, tm, d), lambda i: (i // tiles, i % tiles, 0)),
                pl.BlockSpec((d, w.shape[1]), lambda i: (0, 0), pipeline_mode=pl.Buffered(1))]
    args = [x, w]
    if has_tables:
        tab_spec = pl.BlockSpec((2, tm, HEAD_DIM), lambda i: (0, i % tiles, 0))
        in_specs += [tab_spec, tab_spec]
        args += list(tables)
    kern = functools.partial(_proj_kernel, subblock=subblock, segments=segments, has_tables=has_tables)
    return pl.pallas_call(
        kern,
        out_shape=out_shapes,
        grid=(b * tiles,),
        in_specs=in_specs,
        out_specs=out_specs,
        scratch_shapes=[pltpu.VMEM((tm, d), _bf16), pltpu.VMEM((2, tm, HEAD_DIM), _f32)],
        compiler_params=_cparams(1),
    )(*args)


def _rope_tables(l):
    inv = ROPE_THETA ** (-jnp.arange(ROPE_HALF, dtype=_f32) / ROPE_HALF)
    ang = jnp.arange(l).astype(_f32)[:, None] * inv[None, :]
    cos, sin = jnp.cos(ang), jnp.sin(ang)
    pad = HEAD_DIM - ROPE_DIMS
    cos_t = jnp.concatenate([cos, cos, jnp.ones((l, pad), _f32)], axis=1)
    sin_t = jnp.concatenate([-sin, sin, jnp.zeros((l, pad), _f32)], axis=1)
    return jnp.stack([cos_t * QSCALE, cos_t]), jnp.stack([sin_t * QSCALE, sin_t])


def _memkv_kernel(m_ref, w_ref, o_ref):
    o_ref[0] = jnp.dot(m_ref[...].astype(_bf16), w_ref[0], preferred_element_type=_f32).astype(o_ref.dtype)


def _memkv(mem2d, w_mkv):
    rows, d = mem2d.shape
    depth, _, n = w_mkv.shape
    tm = min(rows, MEMKV_TM)
    assert rows % tm == 0
    return pl.pallas_call(
        _memkv_kernel,
        out_shape=jax.ShapeDtypeStruct((depth, rows, n), _bf16),
        grid=(depth, rows // tm),
        in_specs=[pl.BlockSpec((tm, d), lambda li, i: (i, 0)),
                  pl.BlockSpec((1, d, n), lambda li, i: (li, 0, 0))],
        out_specs=pl.BlockSpec((1, tm, n), lambda li, i: (li, i, 0)),
        compiler_params=_cparams(2),
    )(mem2d, w_mkv)


def _qk(q, k):
    return lax.dot_general(q, k, (((1,), (1,)), ((), ())), preferred_element_type=_f32)


def _first_step():
    return (pl.program_id(0) == 0) & (pl.program_id(1) == 0)


def _band_masks(mask_ref, nk, half, deltas):
    rel = (lax.broadcasted_iota(jnp.int32, (QBLK, nk), 1) - lax.broadcasted_iota(jnp.int32, (QBLK, nk), 0))
    for v, delta in enumerate(deltas):
        mask_ref[v] = jnp.where(jnp.abs(rel + delta) <= half, 0.0, NEG_INF).astype(_f32)


def _edge_variant(i, n):
    return 0 if i == 0 else (2 if i == n - 1 else 1)


def _softmax_pv(scores, values, packed_exp):
    def rowmax(s):
        m = s[:, 0:HEAD_DIM]
        for c in range(1, s.shape[1] // HEAD_DIM):
            m = jnp.maximum(m, s[:, c * HEAD_DIM:(c + 1) * HEAD_DIM])
        return jnp.max(m, axis=-1, keepdims=True)

    ms = [rowmax(s) for s in scores]
    if packed_exp:
        ps = [jnp.exp2((s - m).astype(_bf16)) for s, m in zip(scores, ms)]
    else:
        ps = [jnp.exp2(s - m).astype(_bf16) for s, m in zip(scores, ms)]
    accs = [jnp.dot(p, v, preferred_element_type=_f32) for p, v in zip(ps, values)]
    return ms, accs


def _attn_a_kernel(q_ref, k_ref, v_ref, sink_ref, o_ref, mask_ref, sink2_ref):
    l = k_ref.shape[1]
    nk = 3 * A_WINDOW
    nblk = l // QBLK
    ones = jnp.ones((nk, HEAD_DIM), _bf16)

    @pl.when(_first_step())
    def _():
        rel = (lax.broadcasted_iota(jnp.int32, (QBLK, nk), 1) - lax.broadcasted_iota(jnp.int32, (QBLK, nk), 0))
        for v, delta in enumerate((0, -A_WINDOW, -2 * A_WINDOW)):
            blk = jnp.where(jnp.abs(rel + delta) <= A_WINDOW, 0.0, NEG_INF).astype(_f32)
            for a in range(A_GROUP):
                mask_ref[v, a * QBLK:(a + 1) * QBLK, :] = blk

    for a in range(A_GROUP):
        sink2_ref[a * QBLK:(a + 1) * QBLK, :] = jnp.broadcast_to(sink_ref[0, a:a + 1, :] * LOG2E, (QBLK, HEAD_DIM))

    scores, windows = [], []
    for i in range(nblk):
        q0 = i * QBLK
        ks = min(max(q0 - A_WINDOW, 0), l - nk)
        q = jnp.concatenate([q_ref[0, q0:q0 + QBLK, a * HEAD_DIM:(a + 1) * HEAD_DIM]
                             for a in range(A_GROUP)], axis=0)
        scores.append(_qk(q, k_ref[0, ks:ks + nk, :]) + mask_ref[_edge_variant(i, nblk)])
        windows.append((q0, ks))
    cols = range(nk // HEAD_DIM)
    ms = []
    for s in scores:
        m = s[:, 0:HEAD_DIM]
        for c in cols[1:]:
            m = jnp.maximum(m, s[:, c * HEAD_DIM:(c + 1) * HEAD_DIM])
        m = jnp.broadcast_to(jnp.max(m, axis=-1, keepdims=True), m.shape)
        ms.append(jnp.maximum(m, sink2_ref[...]))
    ps = [jnp.concatenate([jnp.exp2(s[:, c * HEAD_DIM:(c + 1) * HEAD_DIM] - m) for c in cols], axis=1).astype(_bf16)
          for s, m in zip(scores, ms)]
    accs = [jnp.dot(p, jnp.concatenate([v_ref[0, ks:ks + nk, :], ones], axis=1), preferred_element_type=_f32)
            for p, (_, ks) in zip(ps, windows)]
    for m, acc, (q0, _) in zip(ms, accs, windows):
        z = acc[:, HEAD_DIM:] + jnp.exp2(sink2_ref[...] - m)
        o = (acc[:, :HEAD_DIM] / z).astype(o_ref.dtype)
        for a in range(A_GROUP):
            o_ref[0, q0:q0 + QBLK, a * HEAD_DIM:(a + 1) * HEAD_DIM] = o[a * QBLK:(a + 1) * QBLK, :]


def _attn_a(h, sink):
    b, l, _ = h.shape
    gw = A_GROUP * HEAD_DIM
    sink_b = jnp.broadcast_to(sink.astype(_f32).reshape(A_KV_HEADS, A_GROUP, 1), (A_KV_HEADS, A_GROUP, HEAD_DIM))
    sink_b = jnp.pad(sink_b, ((0, 0), (0, 8 - A_GROUP), (0, 0)))
    return pl.pallas_call(
        _attn_a_kernel,
        out_shape=jax.ShapeDtypeStruct((b, l, MIX_W), _bf16),
        grid=(b, A_KV_HEADS),
        in_specs=[pl.BlockSpec((1, l, gw), lambda bi, g: (bi, 0, g)),
                  pl.BlockSpec((1, l, HEAD_DIM), lambda bi, g: (bi, 0, MIX_HEADS + g)),
                  pl.BlockSpec((1, l, HEAD_DIM), lambda bi, g: (bi, 0, MIX_HEADS + A_KV_HEADS + g)),
                  pl.BlockSpec((1, 8, HEAD_DIM), lambda bi, g: (g, 0, 0))],
        out_specs=pl.BlockSpec((1, l, gw), lambda bi, g: (bi, 0, g)),
        scratch_shapes=[pltpu.VMEM((3, A_GROUP * QBLK, 3 * A_WINDOW), _f32),
                        pltpu.VMEM((A_GROUP * QBLK, HEAD_DIM), _f32)],
        compiler_params=_cparams(2),
    )(h, h, h, sink_b)


def _attn_b_kernel(q0_ref, k_ref, v_ref, q1_ref, q2_ref, o_ref,
                   kd_ref, vd_ref, tmp_ref, tmp2_ref, og_ref, lse_ref, mask_ref):
    l = o_ref.shape[1]
    nk = 4 * B_HALF_WINDOW
    q_refs = (q0_ref, q1_ref, q2_ref)
    ones = jnp.ones((nk, HEAD_DIM), _bf16)

    @pl.when(_first_step())
    def _():
        _band_masks(mask_ref, nk, B_HALF_WINDOW, (0, -B_HALF_WINDOW, -2 * B_HALF_WINDOW))

    d1 = B_DILATIONS[1]
    assert B_DILATIONS == (1, d1, d1 * d1)
    n1, n2 = l // d1, l // (d1 * d1)
    for src, dst in ((k_ref, kd_ref), (v_ref, vd_ref)):
        def put(g, row0, n, rows):
            dst[g - 1, row0:row0 + n, :] = rows.astype(_bf16)

        tmp_ref[...] = src[0].astype(_f32)
        for r in range(d1):
            rows = tmp_ref[pl.ds(r, n1, stride=d1), :]
            tmp2_ref[r * n1:(r + 1) * n1, :] = rows
            put(1, r * n1, n1, rows)
        for r in range(d1 * d1):
            put(2, r * n2, n2, tmp2_ref[pl.ds((r % d1) * n1 + r // d1, n2, stride=d1), :])

    for g, dil in reversed(list(enumerate(B_DILATIONS))):
        n = l // dil
        blocks = n // QBLK
        scores, values, dests = [], [], []
        for r in range(dil):
            for i in range(blocks):
                q0 = i * QBLK
                ks = min(max(q0 - B_HALF_WINDOW, 0), n - nk)
                base = r * n + ks
                if g == 0:
                    q = q0_ref[0, q0:q0 + QBLK, :]
                    kw, vw = k_ref[0, ks:ks + nk, :], v_ref[0, ks:ks + nk, :]
                else:
                    q = q_refs[g][0, r, q0:q0 + QBLK, :]
                    kw, vw = kd_ref[g - 1, base:base + nk, :], vd_ref[g - 1, base:base + nk, :]
                scores.append(_qk(q, kw) + mask_ref[_edge_variant(i, blocks)])
                values.append(jnp.concatenate([vw, ones], axis=1))
                dests.append((q0, r))
        ms, accs = _softmax_pv(scores, values, packed_exp=True)
        for m, acc, (q0, r) in zip(ms, accs, dests):
            z = acc[:, HEAD_DIM:]
            if g == 2:
                dst = pl.ds((r % d1) * n1 + q0 * d1 + r // d1, QBLK, stride=d1)
                og_ref[0, dst, :] = acc[:, :HEAD_DIM] / z
                lse_ref[0, dst, :] = m + jnp.log2(z)
                continue
            prev = pl.ds(r * n + q0, QBLK)
            lp = lse_ref[g - 1 if g else 1, prev, :]
            top = jnp.maximum(m, lp)
            w, wp = jnp.exp2(m - top), jnp.exp2(lp - top)
            num = w * acc[:, :HEAD_DIM] + wp * og_ref[g - 1 if g else 1, prev, :]
            den = w * z + wp
            if g == 1:
                dst = pl.ds(q0 * dil + r, QBLK, stride=dil)
                og_ref[1, dst, :] = num / den
                lse_ref[1, dst, :] = top + jnp.log2(den)
            else:
                o_ref[0, q0:q0 + QBLK, :] = (num / den).astype(o_ref.dtype)


def _attn_b(h0, hq1, hq2):
    b, l, _ = h0.shape
    in_specs = [pl.BlockSpec((1, l, HEAD_DIM), lambda bi, h: (bi, 0, h)),
                pl.BlockSpec((1, l, HEAD_DIM), lambda bi, h: (bi, 0, MIX_HEADS + h)),
                pl.BlockSpec((1, l, HEAD_DIM), lambda bi, h: (bi, 0, 2 * MIX_HEADS + h))]
    for dil in B_DILATIONS[1:]:
        in_specs.append(pl.BlockSpec((1, dil, l // dil, HEAD_DIM), lambda bi, h: (bi, 0, 0, h)))
    return pl.pallas_call(
        _attn_b_kernel,
        out_shape=jax.ShapeDtypeStruct((b, l, MIX_W), _bf16),
        grid=(b, MIX_HEADS),
        in_specs=in_specs,
        out_specs=pl.BlockSpec((1, l, HEAD_DIM), lambda bi, h: (bi, 0, h)),
        scratch_shapes=[pltpu.VMEM((2, l, HEAD_DIM), _bf16),
                        pltpu.VMEM((2, l, HEAD_DIM), _bf16),
                        pltpu.VMEM((l, HEAD_DIM), _f32),
                        pltpu.VMEM((l, HEAD_DIM), _f32),
                        pltpu.VMEM((2, l, HEAD_DIM), _f32),
                        pltpu.VMEM((2, l, HEAD_DIM), _f32),
                        pltpu.VMEM((3, QBLK, 4 * B_HALF_WINDOW), _f32)],
        compiler_params=_cparams(2),
    )(h0, h0, h0, hq1, hq2)


C_KSEG = 4 * SUB
C_NSEG = 4
C_NKEYS = C_NSEG * C_KSEG


def _attn_c_kernel(q_ref, k_ref, v_ref, bias_ref, mask_ref, o_ref, bm_ref):
    l = k_ref.shape[1]
    n_rb = l // GRID_W // NA_ROWS
    n_cb = GRID_W // NA_COLS
    n_r4 = l // GRID_W // SUB_R
    ones = jnp.ones((C_NKEYS, HEAD_DIM), _bf16)

    @pl.when(pl.program_id(1) == 0)
    def _():
        for v in range(9):
            bm_ref[v] = bias_ref[0] + mask_ref[v]

    def segment(ref, st):
        lo, hi = max(st, 0), min(st + C_KSEG, l)
        parts = [jnp.zeros((lo - st, HEAD_DIM), _bf16)] if lo > st else []
        parts.append(ref[0, lo:hi, :])
        if hi < st + C_KSEG:
            parts.append(jnp.zeros((st + C_KSEG - hi, HEAD_DIM), _bf16))
        return parts

    scores, values, dests = [], [], []
    for rb in range(n_rb):
        for cb in range(n_cb):
            rv, cv = _edge_variant(rb, n_rb), _edge_variant(cb, n_cb)
            q_starts = [((2 * rb + a) * SUBS_PER_ROW + 2 * cb) * SUB for a in range(2)]
            q = jnp.concatenate([q_ref[0, qs:qs + 2 * SUB, :] for qs in q_starts], axis=0)
            kparts, vparts = [], []
            for ar in range(C_NSEG):
                r4 = min(max(2 * rb - 1 + ar, 0), n_r4 - 1)
                st = (r4 * SUBS_PER_ROW + 2 * cb - 1) * SUB
                kparts += segment(k_ref, st)
                vparts += segment(v_ref, st)
            scores.append(_qk(q, jnp.concatenate(kparts, axis=0)) + bm_ref[rv * 3 + cv])
            values.append(jnp.concatenate([jnp.concatenate(vparts, axis=0), ones], axis=1))
            dests.append(q_starts)
    ms, accs = _softmax_pv(scores, values, packed_exp=False)
    for acc, q_starts in zip(accs, dests):
        o = (acc[:, :HEAD_DIM] / acc[:, HEAD_DIM:]).astype(o_ref.dtype)
        for a in range(2):
            o_ref[0, q_starts[a]:q_starts[a] + 2 * SUB, :] = o[a * 2 * SUB:(a + 1) * 2 * SUB, :]


def _c_geometry():
    ql = jnp.arange(QBLK)
    qa, qc, qi, qj = ql // 64, (ql // 32) % 2, (ql // 8) % 4, ql % 8
    q_row = SUB_R * qa + qi
    q_col = SUB_C * qc + qj
    kl = jnp.arange(C_NKEYS)
    ka, kc, ki, kj = kl // C_KSEG, (kl // SUB) % 4, (kl // 8) % 4, kl % 8
    k_row = SUB_R * (ka - 1) + ki
    k_col = SUB_C * (kc - 1) + kj
    return q_row, q_col, k_row, k_col


def _c_bias(rpb):
    q_row, q_col, k_row, k_col = _c_geometry()
    nr, nc = 2 * NA_ROWS - 1, 2 * NA_COLS - 1
    dr = jnp.clip(k_row[None, :] - q_row[:, None] + NA_ROWS - 1, 0, nr - 1)
    dc = jnp.clip(k_col[None, :] - q_col[:, None] + NA_COLS - 1, 0, nc - 1)
    oh_r = (dr[:, :, None] == jnp.arange(nr)).astype(_f32)
    oh_c = (dc[:, :, None] == jnp.arange(nc)).astype(_f32)
    rows = jnp.einsum("hrc,qkr->hqkc", rpb.astype(_f32), oh_r, precision=lax.Precision.HIGHEST)
    return jnp.sum(rows * oh_c[None], axis=-1) * LOG2E


def _c_masks(rows):
    q_row, q_col, k_row, k_col = _c_geometry()
    out = []
    for rb in (0, 1, rows // NA_ROWS - 1):
        qr, kr = NA_ROWS * rb + q_row, NA_ROWS * rb + k_row
        rs = jnp.clip(qr - NA_ROWS // 2, 0, rows - NA_ROWS)
        rvalid = (kr[None, :] >= rs[:, None]) & (kr[None, :] < rs[:, None] + NA_ROWS) & (kr[None, :] >= 0) & (kr[None, :] < rows)
        for cb in (0, 1, GRID_W // NA_COLS - 1):
            qc, kc = NA_COLS * cb + q_col, NA_COLS * cb + k_col
            cs = jnp.clip(qc - NA_COLS // 2, 0, GRID_W - NA_COLS)
            cvalid = (kc[None, :] >= cs[:, None]) & (kc[None, :] < cs[:, None] + NA_COLS) & (kc[None, :] >= 0) & (kc[None, :] < GRID_W)
            out.append(jnp.where(rvalid & cvalid, 0.0, NEG_INF).astype(_f32))
    return jnp.stack(out)


def _attn_c(h, rpb):
    b, l, _ = h.shape
    bias = _c_bias(rpb)
    masks = _c_masks(l // GRID_W)
    return pl.pallas_call(
        _attn_c_kernel,
        out_shape=jax.ShapeDtypeStruct((b, l, MIX_W), _bf16),
        grid=(MIX_HEADS, b),
        in_specs=[pl.BlockSpec((1, l, HEAD_DIM), lambda h, bi: (bi, 0, h)),
                  pl.BlockSpec((1, l, HEAD_DIM), lambda h, bi: (bi, 0, MIX_HEADS + h)),
                  pl.BlockSpec((1, l, HEAD_DIM), lambda h, bi: (bi, 0, 2 * MIX_HEADS + h)),
                  pl.BlockSpec((1, QBLK, C_NKEYS), lambda h, bi: (h, 0, 0)),
                  pl.BlockSpec((9, QBLK, C_NKEYS), lambda h, bi: (0, 0, 0))],
        out_specs=pl.BlockSpec((1, l, HEAD_DIM), lambda h, bi: (bi, 0, h)),
        scratch_shapes=[pltpu.VMEM((9, QBLK, C_NKEYS), _f32)],
        compiler_params=_cparams(2),
    )(h, h, h, bias, masks)


def _subblock_chunks(tm):
    grp = SUB * SUBS_PER_ROW
    pairs = []
    for g in range(tm // grp):
        for c8 in range(SUBS_PER_ROW):
            for i4 in range(SUB_R):
                pairs.append((g * grp + i4 * GRID_W + c8 * SUB_C, (g * SUBS_PER_ROW + c8) * SUB + i4 * SUB_C))
    return pairs


def _out_kernel(om_ref, glo_ref, ghi_ref, mq_ref, kv_ref, x_ref, w_ref, g_ref, b_ref, o_ref, xs_ref, *, subblock):
    half = D_INNER // 2
    ones = jnp.ones((kv_ref.shape[1], HEAD_DIM), _bf16)
    values = [jnp.concatenate([kv_ref[0, :, MEM_W + h * HEAD_DIM:MEM_W + (h + 1) * HEAD_DIM], ones], axis=1)
              for h in range(MEM_HEADS)]

    n_slices = D_INNER // MEM_W
    part_rows = OUT_SUB // n_slices

    def layer_norm_part(r0, q, z_in):
        lo = q * part_rows
        xin = (xs_ref if subblock else x_ref.at[0])[r0 + lo:r0 + lo + part_rows, :]
        z = ALPHA * xin + z_in[lo:lo + part_rows, :]
        mu = jnp.mean(z, axis=-1, keepdims=True)
        zc = z - mu
        var = jnp.mean(zc * zc, axis=-1, keepdims=True)
        out = zc * lax.rsqrt(var + LN_EPS) * g_ref[...] + b_ref[...]
        if subblock:
            for nat, sub in _subblock_chunks(OUT_SUB):
                if lo <= sub < lo + part_rows:
                    o_ref[0, r0 + nat:r0 + nat + SUB_C, :] = out[sub - lo:sub - lo + SUB_C, :]
        else:
            o_ref[0, r0 + lo:r0 + lo + part_rows, :] = out

    pending = None
    for r0 in range(0, om_ref.shape[1], OUT_SUB):
        rows = slice(r0, r0 + OUT_SUB)
        if subblock:
            for nat, sub in _subblock_chunks(OUT_SUB):
                xs_ref[r0 + sub:r0 + sub + SUB_C, :] = x_ref[0, r0 + nat:r0 + nat + SUB_C, :]

        def silu_gate(c0, c1, rows=rows):
            ref, off = (glo_ref, 0) if c0 < half else (ghi_ref, half)
            hg = 0.5 * ref[0, rows, c0 - off:c1 - off].astype(_f32)
            return hg + hg * jnp.tanh(hg)

        scores = [_qk(mq_ref[0, rows, h * HEAD_DIM:(h + 1) * HEAD_DIM],
                      kv_ref[0, :, h * HEAD_DIM:(h + 1) * HEAD_DIM]) * QSCALE for h in range(MEM_HEADS)]
        _, accs = _softmax_pv(scores, values, packed_exp=True)
        branch = None
        for q, c0 in enumerate(range(0, D_INNER, MEM_W)):
            if c0 < MIX_W:
                y = om_ref[0, rows, c0:c0 + MEM_W].astype(_f32) * silu_gate(c0, c0 + MEM_W)
            else:
                y = jnp.concatenate([acc[:, :HEAD_DIM] / acc[:, HEAD_DIM:] for acc in accs], axis=1)
                y = y * silu_gate(c0, c0 + MEM_W)
            part = jnp.dot(y.astype(_bf16), w_ref[c0:c0 + MEM_W, :], preferred_element_type=_f32)
            branch = part if branch is None else branch + part
            if pending is not None:
                layer_norm_part(pending[0], q, pending[1])
        pending = (r0, branch)
    for q in range(n_slices):
        layer_norm_part(pending[0], q, pending[1])


def _out(o_mix, h, mq_col, kv, x, w_out, ln_g, ln_b, *, subblock):
    b, l, d = x.shape
    tm = OUT_TM
    mlen = kv.shape[1]
    half = D_INNER // 2
    gate_col = mq_col + MEM_W
    assert mq_col % MEM_W == 0 and gate_col % half == 0
    kern = functools.partial(_out_kernel, subblock=subblock)
    return pl.pallas_call(
        kern,
        out_shape=jax.ShapeDtypeStruct((b, l, d), _f32),
        grid=(b, l // tm),
        in_specs=[pl.BlockSpec((1, tm, MIX_W), lambda bi, i: (bi, i, 0)),
                  pl.BlockSpec((1, tm, half), lambda bi, i: (bi, i, gate_col // half)),
                  pl.BlockSpec((1, tm, half), lambda bi, i: (bi, i, gate_col // half + 1)),
                  pl.BlockSpec((1, tm, MEM_W), lambda bi, i: (bi, i, mq_col // MEM_W)),
                  pl.BlockSpec((1, mlen, 2 * MEM_W), lambda bi, i: (bi, 0, 0)),
                  pl.BlockSpec((1, tm, d), lambda bi, i: (bi, i, 0)),
                  pl.BlockSpec((D_INNER, d), lambda bi, i: (0, 0)),
                  pl.BlockSpec((1, d), lambda bi, i: (0, 0)),
                  pl.BlockSpec((1, d), lambda bi, i: (0, 0))],
        out_specs=pl.BlockSpec((1, tm, d), lambda bi, i: (bi, i, 0)),
        scratch_shapes=[pltpu.VMEM((tm, d), _f32)],
        compiler_params=_cparams(2),
    )(o_mix, h, h, h, kv, x, w_out, ln_g.reshape(1, d), ln_b.reshape(1, d))


def _split_cols(w, sizes):
    out, c = [], 0
    for s in sizes:
        out.append(w[:, c:c + s])
        c += s
    return out


def kernel(x, mem, w_in_a, sink_a, w_in_b, w_in_c, rpb_c, w_mkv, w_out, ln_g, ln_b):
    b, l, d = x.shape
    mlen = mem.shape[1]
    kv_all = _memkv(mem.reshape(b * mlen, d), w_mkv.astype(_bf16)).reshape(DEPTH, b, mlen, 2 * MEM_W)
    tables = _rope_tables(l)

    for i in range(DEPTH):
        kind, j = i % NUM_MIXERS, i // NUM_MIXERS
        tail = MEM_W + D_INNER
        if kind == 0:
            seg = ((1, ((MIX_W, "rope_q"), (MEM_W, "rope_k"), (MEM_W + tail, "plain"))),)
            h, = _proj(x, w_in_a[j].astype(_bf16), tables, seg)
            o_mix = _attn_a(h, sink_a[j])
            mq_col = MIX_W + 2 * MEM_W
        elif kind == 1:
            wq0, wq1, wq2, wrest = _split_cols(w_in_b[j].astype(_bf16), (MIX_W, MIX_W, MIX_W, 2 * MIX_W + tail))
            seg = ((B_DILATIONS[2], ((MIX_W, "rope_q"),)),
                   (B_DILATIONS[1], ((MIX_W, "rope_q"),)),
                   (1, ((MIX_W, "rope_q"), (MIX_W, "rope_k"), (MIX_W + tail, "plain"))))
            hq2, hq1, h = _proj(x, jnp.concatenate([wq2, wq1, wq0, wrest], axis=1), tables, seg)
            o_mix = _attn_b(h, hq1, hq2)
            mq_col = 3 * MIX_W
        else:
            seg = ((1, ((MIX_W, "scale"), (2 * MIX_W + tail, "plain"))),)
            h, = _proj(x, w_in_c[j].astype(_bf16), tables, seg, subblock=True)
            o_mix = _attn_c(h, rpb_c[j])
            mq_col = 3 * MIX_W
        x = _out(o_mix, h, mq_col, kv_all[i], x, w_out[i].astype(_bf16), ln_g[i], ln_b[i], subblock=(kind == 2))
    return x
```

```python
import functools
import math

import jax
import jax.numpy as jnp
from jax import lax
from jax.experimental import pallas as pl
from jax.experimental.pallas import tpu as pltpu

D_MODEL = 1024
DEPTH = 4
NUM_MIXERS = 3
HEAD_DIM = 128
D_INNER = 2 * D_MODEL
MEM_HEADS = 4
MIX_HEADS = D_INNER // HEAD_DIM - MEM_HEADS
A_KV_HEADS = MIX_HEADS // 3
A_GROUP = MIX_HEADS // A_KV_HEADS
A_WINDOW = 128
B_DILATIONS = (1, 4, 16)
B_HALF_WINDOW = 64
NA_ROWS = 8
NA_COLS = 16
GRID_W = 64
ROPE_THETA = 500000.0
ROPE_DIMS = HEAD_DIM // 4
ROPE_HALF = ROPE_DIMS // 2
LN_EPS = 1e-5
ALPHA = (2 * DEPTH) ** 0.25
NEG_INF = -1e30
LOG2E = math.log2(math.e)
QSCALE = HEAD_DIM ** -0.5 * LOG2E

MIX_W = MIX_HEADS * HEAD_DIM
MEM_W = MEM_HEADS * HEAD_DIM

SUB_R = 4
SUB_C = 8
SUB = SUB_R * SUB_C
SUBS_PER_ROW = GRID_W // SUB_C
QBLK = 128

V7X_VMEM_BYTES = 64 * 1024 * 1024
VMEM_LIMIT = V7X_VMEM_BYTES * 7 // 8
PROJ_TM_CHOICES = (1024, 512)
PROJ_TN = 512
MEMKV_TM = 512
OUT_TM = 1024
OUT_SUB = 512

_f32 = jnp.float32
_bf16 = jnp.bfloat16


def _cparams(n_grid):
    return pltpu.CompilerParams(dimension_semantics=("arbitrary",) * n_grid,
                                vmem_limit_bytes=VMEM_LIMIT)


def _rope(blk, cos, sin):
    lane = lax.broadcasted_iota(jnp.int32, blk.shape, 1)
    up = pltpu.roll(blk, HEAD_DIM - ROPE_HALF, 1)
    down = pltpu.roll(blk, ROPE_HALF, 1)
    swapped = jnp.where(lane < ROPE_HALF, up, down)
    return blk * cos + swapped * sin


def _proj_kernel(*refs, subblock, segments, has_tables):
    x_ref, w_ref = refs[:2]
    pos = 2
    if has_tables:
        cos_ref, sin_ref = refs[2:4]
        pos = 4
    out_refs = refs[pos:pos + len(segments)]
    xs_ref, tmp_ref = refs[pos + len(segments):]
    tm = xs_ref.shape[0]
    tn = PROJ_TN

    if subblock:
        grp = SUB * SUBS_PER_ROW
        for g in range(tm // grp):
            for c8 in range(SUBS_PER_ROW):
                parts = [x_ref[0, g * grp + i4 * GRID_W + c8 * SUB_C:
                               g * grp + i4 * GRID_W + (c8 + 1) * SUB_C, :] for i4 in range(SUB_R)]
                dst = (g * SUBS_PER_ROW + c8) * SUB
                xs_ref[dst:dst + SUB, :] = jnp.concatenate(parts, axis=0).astype(_bf16)
    else:
        xs_ref[...] = x_ref[0].astype(_bf16)

    def write(o_ref, dil, val, c0):
        c1 = c0 + HEAD_DIM
        if dil == 1:
            o_ref[0, :, c0:c1] = val.astype(o_ref.dtype)
            return
        n = tm // dil
        tmp_ref[0] = val
        if dil == B_DILATIONS[2]:
            d1 = B_DILATIONS[1]
            n1 = tm // d1
            for r in range(d1):
                tmp_ref[1, r * n1:(r + 1) * n1, :] = tmp_ref[0, pl.ds(r, n1, stride=d1), :]
            for r in range(dil):
                rows = tmp_ref[1, pl.ds((r % d1) * n1 + r // d1, n, stride=d1), :]
                o_ref[0, r, :, c0:c1] = rows.astype(o_ref.dtype)
        else:
            for r in range(dil):
                o_ref[0, r, :, c0:c1] = tmp_ref[0, pl.ds(r, n, stride=dil), :].astype(o_ref.dtype)

    wcol = 0
    for o_ref, (dil, parts) in zip(out_refs, segments):
        ocol = 0
        for n_cols, mode in parts:
            for _ in range(n_cols // tn):
                acc = jnp.dot(xs_ref[...], w_ref[:, wcol:wcol + tn], preferred_element_type=_f32)
                for hd in range(tn // HEAD_DIM):
                    val = acc[:, hd * HEAD_DIM:(hd + 1) * HEAD_DIM]
                    if mode == "rope_q":
                        val = _rope(val, cos_ref[0], sin_ref[0])
                    elif mode == "rope_k":
                        val = _rope(val, cos_ref[1], sin_ref[1])
                    elif mode == "scale":
                        val = val * QSCALE
                    write(o_ref, dil, val, ocol + hd * HEAD_DIM)
                wcol += tn
                ocol += tn


def _proj_rows(d, n, has_tables):
    for tm in PROJ_TM_CHOICES:
        need = (d * n * 2 + 2 * tm * n * 2 + 2 * tm * d * 4 + tm * d * 2
                + (2 * 2 * 2 * tm * HEAD_DIM * 4 if has_tables else 0)
                + 2 * tm * HEAD_DIM * 4 + tm * PROJ_TN * 4)
        if need <= VMEM_LIMIT:
            return tm
    raise ValueError("projection weight does not fit in VMEM")


def _proj(x, w, tables, segments, *, subblock=False):
    b, l, d = x.shape
    tn = PROJ_TN
    widths = [sum(n for n, _ in parts) for _, parts in segments]
    assert sum(widths) == w.shape[1] and all(n % tn == 0 for _, parts in segments for n, _ in parts)
    has_tables = any(mode.startswith("rope") for _, parts in segments for _, mode in parts)
    tm = _proj_rows(d, w.shape[1], has_tables)
    tiles = l // tm
    assert l % tm == 0
    out_shapes, out_specs = [], []
    for (dil, _), n in zip(segments, widths):
        if dil == 1:
            out_shapes.append(jax.ShapeDtypeStruct((b, l, n), _bf16))
            out_specs.append(pl.BlockSpec((1, tm, n), lambda i: (i // tiles, i % tiles, 0)))
        else:
            out_shapes.append(jax.ShapeDtypeStruct((b, dil, l // dil, n), _bf16))
            out_specs.append(pl.BlockSpec((1, dil, tm // dil, n), lambda i: (i // tiles, 0, i % tiles, 0)))
    in_specs = [pl.BlockSpec((1, tm, d), lambda i: (i // tiles, i % tiles, 0)),
                pl.BlockSpec((d, w.shape[1]), lambda i: (0, 0), pipeline_mode=pl.Buffered(1))]
    args = [x, w]
    if has_tables:
        tab_spec = pl.BlockSpec((2, tm, HEAD_DIM), lambda i: (0, i % tiles, 0))
        in_specs += [tab_spec, tab_spec]
        args += list(tables)
    kern = functools.partial(_proj_kernel, subblock=subblock, segments=segments, has_tables=has_tables)
    return pl.pallas_call(
        kern,
        out_shape=out_shapes,
        grid=(b * tiles,),
        in_specs=in_specs,
        out_specs=out_specs,
        scratch_shapes=[pltpu.VMEM((tm, d), _bf16), pltpu.VMEM((2, tm, HEAD_DIM), _f32)],
        compiler_params=_cparams(1),
    )(*args)


def _rope_tables(l):
    inv = ROPE_THETA ** (-jnp.arange(ROPE_HALF, dtype=_f32) / ROPE_HALF)
    ang = jnp.arange(l).astype(_f32)[:, None] * inv[None, :]
    cos, sin = jnp.cos(ang), jnp.sin(ang)
    pad = HEAD_DIM - ROPE_DIMS
    cos_t = jnp.concatenate([cos, cos, jnp.ones((l, pad), _f32)], axis=1)
    sin_t = jnp.concatenate([-sin, sin, jnp.zeros((l, pad), _f32)], axis=1)
    return jnp.stack([cos_t * QSCALE, cos_t]), jnp.stack([sin_t * QSCALE, sin_t])


def _memkv_kernel(m_ref, w_ref, o_ref):
    o_ref[0] = jnp.dot(m_ref[...].astype(_bf16), w_ref[0], preferred_element_type=_f32).astype(o_ref.dtype)


def _memkv(mem2d, w_mkv):
    rows, d = mem2d.shape
    depth, _, n = w_mkv.shape
    tm = min(rows, MEMKV_TM)
    assert rows % tm == 0
    return pl.pallas_call(
        _memkv_kernel,
        out_shape=jax.ShapeDtypeStruct((depth, rows, n), _bf16),
        grid=(depth, rows // tm),
        in_specs=[pl.BlockSpec((tm, d), lambda li, i: (i, 0)),
                  pl.BlockSpec((1, d, n), lambda li, i: (li, 0, 0))],
        out_specs=pl.BlockSpec((1, tm, n), lambda li, i: (li, i, 0)),
        compiler_params=_cparams(2),
    )(mem2d, w_mkv)


def _qk(q, k):
    return lax.dot_general(q, k, (((1,), (1,)), ((), ())), preferred_element_type=_f32)


def _first_step():
    return (pl.program_id(0) == 0) & (pl.program_id(1) == 0)


def _band_masks(mask_ref, nk, half, deltas):
    rel = (lax.broadcasted_iota(jnp.int32, (QBLK, nk), 1) - lax.broadcasted_iota(jnp.int32, (QBLK, nk), 0))
    for v, delta in enumerate(deltas):
        mask_ref[v] = jnp.where(jnp.abs(rel + delta) <= half, 0.0, NEG_INF).astype(_f32)


def _edge_variant(i, n):
    return 0 if i == 0 else (2 if i == n - 1 else 1)


def _softmax_pv(scores, values, packed_exp):
    def rowmax(s):
        m = s[:, 0:HEAD_DIM]
        for c in range(1, s.shape[1] // HEAD_DIM):
            m = jnp.maximum(m, s[:, c * HEAD_DIM:(c + 1) * HEAD_DIM])
        return jnp.max(m, axis=-1, keepdims=True)

    ms = [rowmax(s) for s in scores]
    if packed_exp:
        ps = [jnp.exp2((s - m).astype(_bf16)) for s, m in zip(scores, ms)]
    else:
        ps = [jnp.exp2(s - m).astype(_bf16) for s, m in zip(scores, ms)]
    accs = [jnp.dot(p, v, preferred_element_type=_f32) for p, v in zip(ps, values)]
    return ms, accs


def _attn_a_kernel(q_ref, k_ref, v_ref, sink_ref, o_ref, mask_ref, sink2_ref):
    l = k_ref.shape[1]
    nk = 3 * A_WINDOW
    nblk = l // QBLK
    ones = jnp.ones((nk, HEAD_DIM), _bf16)

    @pl.when(_first_step())
    def _():
        rel = (lax.broadcasted_iota(jnp.int32, (QBLK, nk), 1) - lax.broadcasted_iota(jnp.int32, (QBLK, nk), 0))
        for v, delta in enumerate((0, -A_WINDOW, -2 * A_WINDOW)):
            blk = jnp.where(jnp.abs(rel + delta) <= A_WINDOW, 0.0, NEG_INF).astype(_f32)
            for a in range(A_GROUP):
                mask_ref[v, a * QBLK:(a + 1) * QBLK, :] = blk

    for a in range(A_GROUP):
        sink2_ref[a * QBLK:(a + 1) * QBLK, :] = jnp.broadcast_to(sink_ref[0, a:a + 1, :] * LOG2E, (QBLK, HEAD_DIM))

    scores, windows = [], []
    for i in range(nblk):
        q0 = i * QBLK
        ks = min(max(q0 - A_WINDOW, 0), l - nk)
        q = jnp.concatenate([q_ref[0, q0:q0 + QBLK, a * HEAD_DIM:(a + 1) * HEAD_DIM]
                             for a in range(A_GROUP)], axis=0)
        scores.append(_qk(q, k_ref[0, ks:ks + nk, :]) + mask_ref[_edge_variant(i, nblk)])
        windows.append((q0, ks))
    cols = range(nk // HEAD_DIM)
    ms = []
    for s in scores:
        m = s[:, 0:HEAD_DIM]
        for c in cols[1:]:
            m = jnp.maximum(m, s[:, c * HEAD_DIM:(c + 1) * HEAD_DIM])
        m = jnp.broadcast_to(jnp.max(m, axis=-1, keepdims=True), m.shape)
        ms.append(jnp.maximum(m, sink2_ref[...]))
    ps = [jnp.concatenate([jnp.exp2(s[:, c * HEAD_DIM:(c + 1) * HEAD_DIM] - m) for c in cols], axis=1).astype(_bf16)
          for s, m in zip(scores, ms)]
    accs = [jnp.dot(p, jnp.concatenate([v_ref[0, ks:ks + nk, :], ones], axis=1), preferred_element_type=_f32)
            for p, (_, ks) in zip(ps, windows)]
    for m, acc, (q0, _) in zip(ms, accs, windows):
        z = acc[:, HEAD_DIM:] + jnp.exp2(sink2_ref[...] - m)
        o = (acc[:, :HEAD_DIM] / z).astype(o_ref.dtype)
        for a in range(A_GROUP):
            o_ref[0, q0:q0 + QBLK, a * HEAD_DIM:(a + 1) * HEAD_DIM] = o[a * QBLK:(a + 1) * QBLK, :]


def _attn_a(h, sink):
    b, l, _ = h.shape
    gw = A_GROUP * HEAD_DIM
    sink_b = jnp.broadcast_to(sink.astype(_f32).reshape(A_KV_HEADS, A_GROUP, 1), (A_KV_HEADS, A_GROUP, HEAD_DIM))
    sink_b = jnp.pad(sink_b, ((0, 0), (0, 8 - A_GROUP), (0, 0)))
    return pl.pallas_call(
        _attn_a_kernel,
        out_shape=jax.ShapeDtypeStruct((b, l, MIX_W), _bf16),
        grid=(b, A_KV_HEADS),
        in_specs=[pl.BlockSpec((1, l, gw), lambda bi, g: (bi, 0, g)),
                  pl.BlockSpec((1, l, HEAD_DIM), lambda bi, g: (bi, 0, MIX_HEADS + g)),
                  pl.BlockSpec((1, l, HEAD_DIM), lambda bi, g: (bi, 0, MIX_HEADS + A_KV_HEADS + g)),
                  pl.BlockSpec((1, 8, HEAD_DIM), lambda bi, g: (g, 0, 0))],
        out_specs=pl.BlockSpec((1, l, gw), lambda bi, g: (bi, 0, g)),
        scratch_shapes=[pltpu.VMEM((3, A_GROUP * QBLK, 3 * A_WINDOW), _f32),
                        pltpu.VMEM((A_GROUP * QBLK, HEAD_DIM), _f32)],
        compiler_params=_cparams(2),
    )(h, h, h, sink_b)


def _attn_b_kernel(q0_ref, k_ref, v_ref, q1_ref, q2_ref, o_ref,
                   kd_ref, vd_ref, tmp_ref, tmp2_ref, og_ref, lse_ref, mask_ref):
    l = o_ref.shape[1]
    nk = 4 * B_HALF_WINDOW
    q_refs = (q0_ref, q1_ref, q2_ref)
    ones = jnp.ones((nk, HEAD_DIM), _bf16)

    @pl.when(_first_step())
    def _():
        _band_masks(mask_ref, nk, B_HALF_WINDOW, (0, -B_HALF_WINDOW, -2 * B_HALF_WINDOW))

    d1 = B_DILATIONS[1]
    assert B_DILATIONS == (1, d1, d1 * d1)
    n1, n2 = l // d1, l // (d1 * d1)
    for src, dst in ((k_ref, kd_ref), (v_ref, vd_ref)):
        def put(g, row0, n, rows):
            dst[g - 1, row0:row0 + n, :] = rows.astype(_bf16)

        tmp_ref[...] = src[0].astype(_f32)
        for r in range(d1):
            rows = tmp_ref[pl.ds(r, n1, stride=d1), :]
            tmp2_ref[r * n1:(r + 1) * n1, :] = rows
            put(1, r * n1, n1, rows)
        for r in range(d1 * d1):
            put(2, r * n2, n2, tmp2_ref[pl.ds((r % d1) * n1 + r // d1, n2, stride=d1), :])

    for g, dil in reversed(list(enumerate(B_DILATIONS))):
        n = l // dil
        blocks = n // QBLK
        scores, values, dests = [], [], []
        for r in range(dil):
            for i in range(blocks):
                q0 = i * QBLK
                ks = min(max(q0 - B_HALF_WINDOW, 0), n - nk)
                base = r * n + ks
                if g == 0:
                    q = q0_ref[0, q0:q0 + QBLK, :]
                    kw, vw = k_ref[0, ks:ks + nk, :], v_ref[0, ks:ks + nk, :]
                else:
                    q = q_refs[g][0, r, q0:q0 + QBLK, :]
                    kw, vw = kd_ref[g - 1, base:base + nk, :], vd_ref[g - 1, base:base + nk, :]
                scores.append(_qk(q, kw) + mask_ref[_edge_variant(i, blocks)])
                values.append(jnp.concatenate([vw, ones], axis=1))
                dests.append((q0, r))
        ms, accs = _softmax_pv(scores, values, packed_exp=True)
        for m, acc, (q0, r) in zip(ms, accs, dests):
            z = acc[:, HEAD_DIM:]
            if g == 2:
                dst = pl.ds((r % d1) * n1 + q0 * d1 + r // d1, QBLK, stride=d1)
                og_ref[0, dst, :] = acc[:, :HEAD_DIM] / z
                lse_ref[0, dst, :] = m + jnp.log2(z)
                continue
            prev = pl.ds(r * n + q0, QBLK)
            lp = lse_ref[g - 1 if g else 1, prev, :]
            top = jnp.maximum(m, lp)
            w, wp = jnp.exp2(m - top), jnp.exp2(lp - top)
            num = w * acc[:, :HEAD_DIM] + wp * og_ref[g - 1 if g else 1, prev, :]
            den = w * z + wp
            if g == 1:
                dst = pl.ds(q0 * dil + r, QBLK, stride=dil)
                og_ref[1, dst, :] = num / den
                lse_ref[1, dst, :] = top + jnp.log2(den)
            else:
                o_ref[0, q0:q0 + QBLK, :] = (num / den).astype(o_ref.dtype)


def _attn_b(h0, hq1, hq2):
    b, l, _ = h0.shape
    in_specs = [pl.BlockSpec((1, l, HEAD_DIM), lambda bi, h: (bi, 0, h)),
                pl.BlockSpec((1, l, HEAD_DIM), lambda bi, h: (bi, 0, MIX_HEADS + h)),
                pl.BlockSpec((1, l, HEAD_DIM), lambda bi, h: (bi, 0, 2 * MIX_HEADS + h))]
    for dil in B_DILATIONS[1:]:
        in_specs.append(pl.BlockSpec((1, dil, l // dil, HEAD_DIM), lambda bi, h: (bi, 0, 0, h)))
    return pl.pallas_call(
        _attn_b_kernel,
        out_shape=jax.ShapeDtypeStruct((b, l, MIX_W), _bf16),
        grid=(b, MIX_HEADS),
        in_specs=in_specs,
        out_specs=pl.BlockSpec((1, l, HEAD_DIM), lambda bi, h: (bi, 0, h)),
        scratch_shapes=[pltpu.VMEM((2, l, HEAD_DIM), _bf16),
                        pltpu.VMEM((2, l, HEAD_DIM), _bf16),
                        pltpu.VMEM((l, HEAD_DIM), _f32),
                        pltpu.VMEM((l, HEAD_DIM), _f32),
                        pltpu.VMEM((2, l, HEAD_DIM), _f32),
                        pltpu.VMEM((2, l, HEAD_DIM), _f32),
                        pltpu.VMEM((3, QBLK, 4 * B_HALF_WINDOW), _f32)],
        compiler_params=_cparams(2),
    )(h0, h0, h0, hq1, hq2)


C_KSEG = 4 * SUB
C_NSEG = 4
C_NKEYS = C_NSEG * C_KSEG
C_HEADS_PER_STEP = 2


def _attn_c_kernel(q_ref, k_ref, v_ref, bias_ref, mask_ref, o_ref, bm_ref):
    l = k_ref.shape[1]
    n_rb = l // GRID_W // NA_ROWS
    n_cb = GRID_W // NA_COLS
    n_r4 = l // GRID_W // SUB_R
    ones = jnp.ones((C_NKEYS, HEAD_DIM), _bf16)

    @pl.when(pl.program_id(1) == 0)
    def _():
        for hh in range(C_HEADS_PER_STEP):
            for v in range(9):
                bm_ref[hh * 9 + v] = bias_ref[hh] + mask_ref[v]

    def segment(ref, st, lanes):
        lo, hi = max(st, 0), min(st + C_KSEG, l)
        parts = [jnp.zeros((lo - st, HEAD_DIM), _bf16)] if lo > st else []
        parts.append(ref[0, lo:hi, lanes])
        if hi < st + C_KSEG:
            parts.append(jnp.zeros((st + C_KSEG - hi, HEAD_DIM), _bf16))
        return parts

    scores, values, dests = [], [], []
    for hh in range(C_HEADS_PER_STEP):
        lanes = slice(hh * HEAD_DIM, (hh + 1) * HEAD_DIM)
        for rb in range(n_rb):
            for cb in range(n_cb):
                rv, cv = _edge_variant(rb, n_rb), _edge_variant(cb, n_cb)
                q_starts = [((2 * rb + a) * SUBS_PER_ROW + 2 * cb) * SUB for a in range(2)]
                q = jnp.concatenate([q_ref[0, qs:qs + 2 * SUB, lanes] for qs in q_starts], axis=0)
                kparts, vparts = [], []
                for ar in range(C_NSEG):
                    r4 = min(max(2 * rb - 1 + ar, 0), n_r4 - 1)
                    st = (r4 * SUBS_PER_ROW + 2 * cb - 1) * SUB
                    kparts += segment(k_ref, st, lanes)
                    vparts += segment(v_ref, st, lanes)
                scores.append(_qk(q, jnp.concatenate(kparts, axis=0)) + bm_ref[hh * 9 + rv * 3 + cv])
                values.append(jnp.concatenate([jnp.concatenate(vparts, axis=0), ones], axis=1))
                dests.append((lanes, q_starts))
    ms, accs = _softmax_pv(scores, values, packed_exp=False)
    for acc, (lanes, q_starts) in zip(accs, dests):
        o = (acc[:, :HEAD_DIM] / acc[:, HEAD_DIM:]).astype(o_ref.dtype)
        for a in range(2):
            o_ref[0, q_starts[a]:q_starts[a] + 2 * SUB, lanes] = o[a * 2 * SUB:(a + 1) * 2 * SUB, :]


def _c_geometry():
    ql = jnp.arange(QBLK)
    qa, qc, qi, qj = ql // 64, (ql // 32) % 2, (ql // 8) % 4, ql % 8
    q_row = SUB_R * qa + qi
    q_col = SUB_C * qc + qj
    kl = jnp.arange(C_NKEYS)
    ka, kc, ki, kj = kl // C_KSEG, (kl // SUB) % 4, (kl // 8) % 4, kl % 8
    k_row = SUB_R * (ka - 1) + ki
    k_col = SUB_C * (kc - 1) + kj
    return q_row, q_col, k_row, k_col


def _c_bias(rpb):
    q_row, q_col, k_row, k_col = _c_geometry()
    nr, nc = 2 * NA_ROWS - 1, 2 * NA_COLS - 1
    dr = jnp.clip(k_row[None, :] - q_row[:, None] + NA_ROWS - 1, 0, nr - 1)
    dc = jnp.clip(k_col[None, :] - q_col[:, None] + NA_COLS - 1, 0, nc - 1)
    oh_r = (dr[:, :, None] == jnp.arange(nr)).astype(_f32)
    oh_c = (dc[:, :, None] == jnp.arange(nc)).astype(_f32)
    rows = jnp.einsum("hrc,qkr->hqkc", rpb.astype(_f32), oh_r, precision=lax.Precision.HIGHEST)
    return jnp.sum(rows * oh_c[None], axis=-1) * LOG2E


def _c_masks(rows):
    q_row, q_col, k_row, k_col = _c_geometry()
    out = []
    for rb in (0, 1, rows // NA_ROWS - 1):
        qr, kr = NA_ROWS * rb + q_row, NA_ROWS * rb + k_row
        rs = jnp.clip(qr - NA_ROWS // 2, 0, rows - NA_ROWS)
        rvalid = (kr[None, :] >= rs[:, None]) & (kr[None, :] < rs[:, None] + NA_ROWS) & (kr[None, :] >= 0) & (kr[None, :] < rows)
        for cb in (0, 1, GRID_W // NA_COLS - 1):
            qc, kc = NA_COLS * cb + q_col, NA_COLS * cb + k_col
            cs = jnp.clip(qc - NA_COLS // 2, 0, GRID_W - NA_COLS)
            cvalid = (kc[None, :] >= cs[:, None]) & (kc[None, :] < cs[:, None] + NA_COLS) & (kc[None, :] >= 0) & (kc[None, :] < GRID_W)
            out.append(jnp.where(rvalid & cvalid, 0.0, NEG_INF).astype(_f32))
    return jnp.stack(out)


def _attn_c(h, rpb):
    b, l, _ = h.shape
    bias = _c_bias(rpb)
    masks = _c_masks(l // GRID_W)
    groups = MIX_HEADS // C_HEADS_PER_STEP
    gw = C_HEADS_PER_STEP * HEAD_DIM
    return pl.pallas_call(
        _attn_c_kernel,
        out_shape=jax.ShapeDtypeStruct((b, l, MIX_W), _bf16),
        grid=(groups, b),
        in_specs=[pl.BlockSpec((1, l, gw), lambda h, bi: (bi, 0, h)),
                  pl.BlockSpec((1, l, gw), lambda h, bi: (bi, 0, groups + h)),
                  pl.BlockSpec((1, l, gw), lambda h, bi: (bi, 0, 2 * groups + h)),
                  pl.BlockSpec((C_HEADS_PER_STEP, QBLK, C_NKEYS), lambda h, bi: (h, 0, 0)),
                  pl.BlockSpec((9, QBLK, C_NKEYS), lambda h, bi: (0, 0, 0))],
        out_specs=pl.BlockSpec((1, l, gw), lambda h, bi: (bi, 0, h)),
        scratch_shapes=[pltpu.VMEM((C_HEADS_PER_STEP * 9, QBLK, C_NKEYS), _f32)],
        compiler_params=_cparams(2),
    )(h, h, h, bias, masks)


def _subblock_chunks(tm):
    grp = SUB * SUBS_PER_ROW
    pairs = []
    for g in range(tm // grp):
        for c8 in range(SUBS_PER_ROW):
            for i4 in range(SUB_R):
                pairs.append((g * grp + i4 * GRID_W + c8 * SUB_C, (g * SUBS_PER_ROW + c8) * SUB + i4 * SUB_C))
    return pairs


def _out_kernel(om_ref, glo_ref, ghi_ref, mq_ref, kv_ref, x_ref, w_ref, g_ref, b_ref, o_ref, xs_ref, *, subblock):
    half = D_INNER // 2
    ones = jnp.ones((kv_ref.shape[1], HEAD_DIM), _bf16)
    values = [jnp.concatenate([kv_ref[0, :, MEM_W + h * HEAD_DIM:MEM_W + (h + 1) * HEAD_DIM], ones], axis=1)
              for h in range(MEM_HEADS)]

    n_slices = D_INNER // MEM_W
    part_rows = OUT_SUB // n_slices

    def layer_norm_part(r0, q, z_in):
        lo = q * part_rows
        xin = (xs_ref if subblock else x_ref.at[0])[r0 + lo:r0 + lo + part_rows, :]
        z = ALPHA * xin + z_in[lo:lo + part_rows, :]
        mu = jnp.mean(z, axis=-1, keepdims=True)
        zc = z - mu
        var = jnp.mean(zc * zc, axis=-1, keepdims=True)
        out = zc * lax.rsqrt(var + LN_EPS) * g_ref[...] + b_ref[...]
        if subblock:
            for nat, sub in _subblock_chunks(OUT_SUB):
                if lo <= sub < lo + part_rows:
                    o_ref[0, r0 + nat:r0 + nat + SUB_C, :] = out[sub - lo:sub - lo + SUB_C, :]
        else:
            o_ref[0, r0 + lo:r0 + lo + part_rows, :] = out

    pending = None
    for r0 in range(0, om_ref.shape[1], OUT_SUB):
        rows = slice(r0, r0 + OUT_SUB)
        if subblock:
            for nat, sub in _subblock_chunks(OUT_SUB):
                xs_ref[r0 + sub:r0 + sub + SUB_C, :] = x_ref[0, r0 + nat:r0 + nat + SUB_C, :]

        def silu_gate(c0, c1, rows=rows):
            ref, off = (glo_ref, 0) if c0 < half else (ghi_ref, half)
            hg = 0.5 * ref[0, rows, c0 - off:c1 - off].astype(_f32)
            return hg + hg * jnp.tanh(hg)

        scores = [_qk(mq_ref[0, rows, h * HEAD_DIM:(h + 1) * HEAD_DIM],
                      kv_ref[0, :, h * HEAD_DIM:(h + 1) * HEAD_DIM]) * QSCALE for h in range(MEM_HEADS)]
        _, accs = _softmax_pv(scores, values, packed_exp=True)
        branch = None
        for q, c0 in enumerate(range(0, D_INNER, MEM_W)):
            if c0 < MIX_W:
                y = om_ref[0, rows, c0:c0 + MEM_W].astype(_f32) * silu_gate(c0, c0 + MEM_W)
            else:
                y = jnp.concatenate([acc[:, :HEAD_DIM] / acc[:, HEAD_DIM:] for acc in accs], axis=1)
                y = y * silu_gate(c0, c0 + MEM_W)
            part = jnp.dot(y.astype(_bf16), w_ref[c0:c0 + MEM_W, :], preferred_element_type=_f32)
            branch = part if branch is None else branch + part
            if pending is not None:
                layer_norm_part(pending[0], q, pending[1])
        pending = (r0, branch)
    for q in range(n_slices):
        layer_norm_part(pending[0], q, pending[1])


def _out(o_mix, h, mq_col, kv, x, w_out, ln_g, ln_b, *, subblock):
    b, l, d = x.shape
    tm = OUT_TM
    mlen = kv.shape[1]
    half = D_INNER // 2
    gate_col = mq_col + MEM_W
    assert mq_col % MEM_W == 0 and gate_col % half == 0
    kern = functools.partial(_out_kernel, subblock=subblock)
    return pl.pallas_call(
        kern,
        out_shape=jax.ShapeDtypeStruct((b, l, d), _f32),
        grid=(b, l // tm),
        in_specs=[pl.BlockSpec((1, tm, MIX_W), lambda bi, i: (bi, i, 0)),
                  pl.BlockSpec((1, tm, half), lambda bi, i: (bi, i, gate_col // half)),
                  pl.BlockSpec((1, tm, half), lambda bi, i: (bi, i, gate_col // half + 1)),
                  pl.BlockSpec((1, tm, MEM_W), lambda bi, i: (bi, i, mq_col // MEM_W)),
                  pl.BlockSpec((1, mlen, 2 * MEM_W), lambda bi, i: (bi, 0, 0)),
                  pl.BlockSpec((1, tm, d), lambda bi, i: (bi, i, 0)),
                  pl.BlockSpec((D_INNER, d), lambda bi, i: (0, 0)),
                  pl.BlockSpec((1, d), lambda bi, i: (0, 0)),
                  pl.BlockSpec((1, d), lambda bi, i: (0, 0))],
        out_specs=pl.BlockSpec((1, tm, d), lambda bi, i: (bi, i, 0)),
        scratch_shapes=[pltpu.VMEM((tm, d), _f32)],
        compiler_params=_cparams(2),
    )(o_mix, h, h, h, kv, x, w_out, ln_g.reshape(1, d), ln_b.reshape(1, d))


def _split_cols(w, sizes):
    out, c = [], 0
    for s in sizes:
        out.append(w[:, c:c + s])
        c += s
    return out


def kernel(x, mem, w_in_a, sink_a, w_in_b, w_in_c, rpb_c, w_mkv, w_out, ln_g, ln_b):
    b, l, d = x.shape
    mlen = mem.shape[1]
    kv_all = _memkv(mem.reshape(b * mlen, d), w_mkv.astype(_bf16)).reshape(DEPTH, b, mlen, 2 * MEM_W)
    tables = _rope_tables(l)

    for i in range(DEPTH):
        kind, j = i % NUM_MIXERS, i // NUM_MIXERS
        tail = MEM_W + D_INNER
        if kind == 0:
            seg = ((1, ((MIX_W, "rope_q"), (MEM_W, "rope_k"), (MEM_W + tail, "plain"))),)
            h, = _proj(x, w_in_a[j].astype(_bf16), tables, seg)
            o_mix = _attn_a(h, sink_a[j])
            mq_col = MIX_W + 2 * MEM_W
        elif kind == 1:
            wq0, wq1, wq2, wrest = _split_cols(w_in_b[j].astype(_bf16), (MIX_W, MIX_W, MIX_W, 2 * MIX_W + tail))
            seg = ((B_DILATIONS[2], ((MIX_W, "rope_q"),)),
                   (B_DILATIONS[1], ((MIX_W, "rope_q"),)),
                   (1, ((MIX_W, "rope_q"), (MIX_W, "rope_k"), (MIX_W + tail, "plain"))))
            hq2, hq1, h = _proj(x, jnp.concatenate([wq2, wq1, wq0, wrest], axis=1), tables, seg)
            o_mix = _attn_b(h, hq1, hq2)
            mq_col = 3 * MIX_W
        else:
            seg = ((1, ((MIX_W, "scale"), (2 * MIX_W + tail, "plain"))),)
            h, = _proj(x, w_in_c[j].astype(_bf16), tables, seg, subblock=True)
            o_mix = _attn_c(h, rpb_c[j])
            mq_col = 3 * MIX_W
        x = _out(o_mix, h, mq_col, kv_all[i], x, w_out[i].astype(_bf16), ln_g[i], ln_b[i], subblock=(kind == 2))
    return x
```
